```python
import math
import jax, jax.numpy as jnp
from jax import lax
import numpy as np

D_MODEL = 2048
BATCH = 16
SEQ = 256
DEPTH = 2
DEC_BATCH = 8
DEC_SEQ = 1024
PAST_LEN = 256

GRID_W = 64
N_BRANCH = 4
BRANCH_W = D_MODEL // 4
EPS = 1e-6
LRU_BLOCKS = 8
LRU_BS = BRANCH_W // LRU_BLOCKS
LRU_CONV_W = 4
LRU_PAD_L = 2
LRU_C = 8.0
S5_GROUP = 16
S5_NGROUP = BRANCH_W // S5_GROUP
S5_STATE = 64
HEAD_DIM = 64
N_Q_HEADS = BRANCH_W // HEAD_DIM
N_KV_HEADS = 2
Q_PER_KV = N_Q_HEADS // N_KV_HEADS
KV_W = N_KV_HEADS * HEAD_DIM
WINDOW = 128
BLOCK = 128
ROPE_BASE = 10000.0
ROPE_NF = HEAD_DIM // 4
ATTN_SCALE = 1.0 / math.sqrt(HEAD_DIM)
NEG_INF = -1e30
RWKV_HEAD = 64
RWKV_NH = BRANCH_W // RWKV_HEAD
W_LORA = 64
A_LORA = 64
G_LORA = 128
RWKV_GN_EPS = 64e-5
D_FF = 5504
FFN_CONV_W = 3
FFN_PAD_L = 1
IN_WIDTHS = (BRANCH_W, BRANCH_W,
             BRANCH_W,
             BRANCH_W, KV_W, KV_W,
             BRANCH_W, BRANCH_W, BRANCH_W,
             W_LORA, A_LORA, G_LORA,
             N_BRANCH * D_MODEL)
D_IN = sum(IN_WIDTHS)

kernel_name = 'hybrid_dit_prefix_step'


def rmsnorm(x, g):
    x32 = x.astype(jnp.float32)
    y = x32 * lax.rsqrt(jnp.mean(x32 * x32, axis=-1, keepdims=True) + EPS)
    return y.astype(x.dtype) * g


def split_cols(p):
    outs = []
    off = 0
    for w in IN_WIDTHS:
        outs.append(p[..., off:off + w])
        off += w
    return outs


def dwconv(x, w, pad_left):
    K = w.shape[0]
    L = x.shape[1]
    xp = jnp.pad(x, ((0, 0), (pad_left, K - 1 - pad_left), (0, 0)))
    out = xp[:, 0:L] * w[0]
    for j in range(1, K):
        out = out + xp[:, j:j + L] * w[j]
    return out


def _lin_combine(e1, e2):
    a1, b1 = e1
    a2, b2 = e2
    return a1 * a2, a2 * b1 + b2


def linear_scan(a, b, h0, reverse):
    if reverse:
        a = jnp.flip(a, 1)
        b = jnp.flip(b, 1)
    b = b.at[:, 0].add(a[:, 0] * h0)
    _, h = lax.associative_scan(_lin_combine, (a, b), axis=1)
    if reverse:
        h = jnp.flip(h, 1)
    return h


def _cplx_combine(e1, e2):
    ar1, ai1, br1, bi1 = e1
    ar2, ai2, br2, bi2 = e2
    return (ar2 * ar1 - ai2 * ai1, ar2 * ai1 + ai2 * ar1,
            ar2 * br1 - ai2 * bi1 + br2, ar2 * bi1 + ai2 * br1 + bi2)


def complex_scan(ar, ai, br, bi, h0r, h0i, reverse):
    ar = jnp.broadcast_to(ar, br.shape)
    ai = jnp.broadcast_to(ai, br.shape)
    if reverse:
        ar, ai, br, bi = (jnp.flip(t, 1) for t in (ar, ai, br, bi))
    br = br.at[:, 0].add(ar[:, 0] * h0r - ai[:, 0] * h0i)
    bi = bi.at[:, 0].add(ar[:, 0] * h0i + ai[:, 0] * h0r)
    _, _, hr, hi = lax.associative_scan(_cplx_combine, (ar, ai, br, bi), axis=1)
    if reverse:
        hr = jnp.flip(hr, 1)
        hi = jnp.flip(hi, 1)
    return hr, hi


def rglru(x, wa, ba, wx, bx, lam, h0, reverse):
    B, L, W = x.shape
    xb = x.reshape(B, L, LRU_BLOCKS, LRU_BS)
    r = jax.nn.sigmoid(jnp.einsum('blnc,ncd->blnd', xb, wa).reshape(B, L, W) + ba)
    i = jax.nn.sigmoid(jnp.einsum('blnc,ncd->blnd', xb, wx).reshape(B, L, W) + bx)
    log_a = -LRU_C * r * jax.nn.softplus(-lam)
    a = jnp.exp(log_a)
    b = jnp.sqrt(-jnp.expm1(2.0 * log_a)) * (i * x)
    return linear_scan(a, b, h0, reverse)


def s5_dir(u, a_re, a_im, log_dt, b_re, b_im, c_re, c_im, h0r, h0i, reverse):
    dt = jnp.exp(log_dt)[:, None]
    mag = jnp.exp(a_re * dt)
    abar_r = mag * jnp.cos(a_im * dt)
    abar_i = mag * jnp.sin(a_im * dt)
    den = a_re * a_re + a_im * a_im
    nr = abar_r - 1.0
    fr = (nr * a_re + abar_i * a_im) / den
    fi = (abar_i * a_re - nr * a_im) / den
    bbar_r = fr[..., None] * b_re - fi[..., None] * b_im
    bbar_i = fr[..., None] * b_im + fi[..., None] * b_re
    bu_r = jnp.einsum('blgp,gnp->blgn', u, bbar_r)
    bu_i = jnp.einsum('blgp,gnp->blgn', u, bbar_i)
    hr, hi = complex_scan(abar_r, abar_i, bu_r, bu_i, h0r, h0i, reverse)
    y = jnp.einsum('blgn,gpn->blgp', hr, c_re) - jnp.einsum('blgn,gpn->blgp', hi, c_im)
    fin = 0 if reverse else -1
    return y, hr[:, fin], hi[:, fin]


def axial_rope(x):
    L = x.shape[1]
    t = jnp.arange(L)
    rows = t // GRID_W
    cols = t % GRID_W
    inv = ROPE_BASE ** (-jnp.arange(ROPE_NF, dtype=jnp.float32) / ROPE_NF)
    shape = (1, L) + (1,) * (x.ndim - 3) + (ROPE_NF,)

    def rot(xa, pos):
        ang = (pos.astype(jnp.float32)[:, None] * inv).reshape(shape)
        cos = jnp.cos(ang).astype(x.dtype)
        sin = jnp.sin(ang).astype(x.dtype)
        x1, x2 = xa[..., :ROPE_NF], xa[..., ROPE_NF:]
        return jnp.concatenate([x1 * cos - x2 * sin, x1 * sin + x2 * cos], axis=-1)

    half = HEAD_DIM // 2
    return jnp.concatenate([rot(x[..., :half], rows), rot(x[..., half:], cols)], axis=-1)


def attn_block(q, k, v, bias, sink):
    s = jnp.einsum('bqhgd,bkhd->bhgqk', q, k).astype(jnp.float32) * ATTN_SCALE
    if bias is not None:
        s = s + bias
    sk = jnp.broadcast_to(sink.astype(jnp.float32)[None, :, :, None, None], s.shape[:-1] + (1,))
    pr = jax.nn.softmax(jnp.concatenate([s, sk], axis=-1), axis=-1)[..., :-1]
    return jnp.einsum('bhgqk,bkhd->bqhgd', pr.astype(v.dtype), v)


def ctx_attention(q, k, v, sink):
    B, L = q.shape[:2]
    nb = L // BLOCK
    qb = q.reshape(B, nb, BLOCK, N_KV_HEADS, Q_PER_KV, HEAD_DIM).swapaxes(0, 1)
    out = lax.map(lambda qi: attn_block(qi, k, v, None, sink), qb)
    return out.swapaxes(0, 1).reshape(B, L, N_KV_HEADS, Q_PER_KV, HEAD_DIM)


def lat_attention(q, k, v, k_ctx, v_ctx, sink):
    B, L = q.shape[:2]
    P = k_ctx.shape[1]
    nb = L // BLOCK
    kp = jnp.pad(k, ((0, 0), (WINDOW, WINDOW), (0, 0), (0, 0)))
    vp = jnp.pad(v, ((0, 0), (WINDOW, WINDOW), (0, 0), (0, 0)))
    qb = q.reshape(B, nb, BLOCK, N_KV_HEADS, Q_PER_KV, HEAD_DIM).swapaxes(0, 1)
    nkeys = BLOCK + 2 * WINDOW

    def body(args):
        j, qi = args
        kb = lax.dynamic_slice_in_dim(kp, j * BLOCK, nkeys, axis=1)
        vb = lax.dynamic_slice_in_dim(vp, j * BLOCK, nkeys, axis=1)
        kabs = j * BLOCK - WINDOW + jnp.arange(nkeys)
        qabs = j * BLOCK + jnp.arange(BLOCK)
        ok = (jnp.abs(kabs[None, :] - qabs[:, None]) <= WINDOW) & (kabs[None, :] >= 0) & (kabs[None, :] < L)
        band_bias = jnp.where(ok, 0.0, NEG_INF).astype(jnp.float32)
        bias = jnp.concatenate([jnp.zeros((BLOCK, P), jnp.float32), band_bias], axis=1)
        return attn_block(qi, jnp.concatenate([k_ctx, kb], axis=1),
                          jnp.concatenate([v_ctx, vb], axis=1), bias, sink)

    out = lax.map(body, (jnp.arange(nb), qb))
    return out.swapaxes(0, 1).reshape(B, L, N_KV_HEADS, Q_PER_KV, HEAD_DIM)


def token_shift(x, mu):
    prev = jnp.pad(x, ((0, 0), (1, 0), (0, 0)))[:, :-1]
    nxt = jnp.pad(x, ((0, 0), (0, 1), (0, 0)))[:, 1:]
    return x + mu * (0.5 * (prev + nxt) - x)


def rwkv_dir(r, w, k, v, kk, a, S0, reverse):
    xs = tuple(t.swapaxes(0, 1) for t in (r, w, k, v, kk, a))

    def step(S, inp):
        rt, wt, kt, vt, kkt, at = inp
        sa = jnp.einsum('bhvk,bhk->bhv', S, -kkt)
        S = S * wt[:, :, None, :] + sa[..., None] * (kkt * at)[:, :, None, :] + vt[..., None] * kt[:, :, None, :]
        return S, jnp.einsum('bhvk,bhk->bhv', S, rt)

    S, ys = lax.scan(step, S0, xs, reverse=reverse)
    return ys.swapaxes(0, 1), S


def head_norm(y, w, b):
    B, L = y.shape[:2]
    y32 = y.astype(jnp.float32)
    mu = jnp.mean(y32, axis=-1, keepdims=True)
    var = jnp.mean(jnp.square(y32 - mu), axis=-1, keepdims=True)
    yn = ((y32 - mu) * lax.rsqrt(var + RWKV_GN_EPS)).astype(y.dtype)
    return yn.reshape(B, L, BRANCH_W) * w + b


def mixer_block(h, p, l, ctx):
    lat = ctx is not None
    B, L, _ = h.shape
    (a_x, a_gate, b_u, c_q, c_k, c_v, d_r, d_k, d_v, d_wl, d_al, d_gl, merge) = split_cols(h @ p['w_in'][l])

    xa = dwconv(a_x, p['lru_conv_w'][l], LRU_PAD_L) + p['lru_conv_b'][l]
    lru0 = ctx['lru'] if lat else jnp.zeros((B, 2, BRANCH_W), h.dtype)
    hs = [rglru(xa, p['lru_wa'][l, d], p['lru_ba'][l, d], p['lru_wx'][l, d], p['lru_bx'][l, d],
                p['lru_lam'][l, d], lru0[:, d], d == 1) for d in range(2)]
    y_a = jax.nn.gelu(a_gate) * (hs[0] + hs[1])
    lru_fin = jnp.stack([hs[0][:, -1], hs[1][:, 0]], axis=1)

    u = b_u.reshape(B, L, S5_NGROUP, S5_GROUP)
    s50 = ctx['s5'] if lat else jnp.zeros((B, 2, 2, S5_NGROUP, S5_STATE), h.dtype)
    y_s = p['s5_d'][l].reshape(S5_NGROUP, S5_GROUP) * u
    s5_fin = []
    for d in range(2):
        y_dir, hr_f, hi_f = s5_dir(u, p['s5_a_re'][l, d], p['s5_a_im'][l, d], p['s5_log_dt'][l, d],
                                   p['s5_b_re'][l, d], p['s5_b_im'][l, d], p['s5_c_re'][l, d], p['s5_c_im'][l, d],
                                   s50[:, d, 0], s50[:, d, 1], d == 1)
        y_s = y_s + y_dir
        s5_fin.append(jnp.stack([hr_f, hi_f], axis=1))
    s5_fin = jnp.stack(s5_fin, axis=1)
    y_s = jax.nn.gelu(y_s.reshape(B, L, BRANCH_W))
    y_b = y_s * jax.nn.sigmoid(y_s @ p['s5_glu_w'][l] + p['s5_glu_b'][l])

    q = c_q.reshape(B, L, N_KV_HEADS, Q_PER_KV, HEAD_DIM)
    k = c_k.reshape(B, L, N_KV_HEADS, HEAD_DIM)
    v = c_v.reshape(B, L, N_KV_HEADS, HEAD_DIM)
    sink = p['attn_sink'][l]
    if lat:
        o = lat_attention(axial_rope(q), axial_rope(k), v, ctx['k'], ctx['v'], sink)
    else:
        o = ctx_attention(q, k, v, sink)
    y_c = o.reshape(B, L, BRANCH_W)

    mu = p['rwkv_mu'][l]
    shp = (B, L, RWKV_NH, RWKV_HEAD)
    r = token_shift(d_r, mu[0]).reshape(shp)
    kh = token_shift(d_k, mu[1]).reshape(shp)
    vh = token_shift(d_v, mu[2]).reshape(shp)
    g = jax.nn.sigmoid(d_gl) @ p['rwkv_g2'][l]
    kk = kh * p['rwkv_k_k'][l].reshape(RWKV_NH, RWKV_HEAD)
    kk32 = kk.astype(jnp.float32)
    kk = (kk32 * lax.rsqrt(jnp.sum(kk32 * kk32, axis=-1, keepdims=True) + 1e-12)).astype(kh.dtype)
    k_a = p['rwkv_k_a'][l].reshape(RWKV_NH, RWKV_HEAD)
    wkv0 = ctx['wkv'] if lat else jnp.zeros((B, 2, RWKV_NH, RWKV_HEAD, RWKV_HEAD), h.dtype)
    tw = jnp.tanh(d_wl)
    y_wkv = 0.0
    wkv_fin = []
    for d in range(2):
        z = p['rwkv_w0'][l, d] + tw @ p['rwkv_w2'][l, d]
        decay = jnp.exp(-jnp.exp(-jax.nn.softplus(-z) - 0.5)).reshape(shp)
        a = jax.nn.sigmoid(p['rwkv_a0'][l, d] + d_al @ p['rwkv_a2'][l, d]).reshape(shp)
        kd = kh * (1.0 + (a - 1.0) * k_a)
        y_dir, S_fin = rwkv_dir(r, decay, kd, vh, kk, a, wkv0[:, d], d == 1)
        y_wkv = y_wkv + y_dir
        wkv_fin.append(S_fin)
    wkv_fin = jnp.stack(wkv_fin, axis=1)
    y_d = head_norm(y_wkv, p['rwkv_ln_w'][l], p['rwkv_ln_b'][l])
    bonus = jnp.sum(r * kh * p['rwkv_r_k'][l], axis=-1, keepdims=True) * vh
    y_d = (y_d + bonus.reshape(B, L, BRANCH_W)) * g

    ys = jnp.stack([y_a, y_b, y_c, y_d], axis=2)
    proj = jnp.einsum('blnw,nwd->blnd', ys, p['branch_w'][l])
    gates = jax.nn.sigmoid(merge.reshape(B, L, N_BRANCH, D_MODEL))
    out = jnp.sum(gates * proj, axis=2) @ p['w_out'][l]
    new = None if lat else (k, v, lru_fin, s5_fin, wkv_fin)
    return out, new


def conv_ffn(h, p, l):
    u = h @ p['ffn_w_in'][l]
    gate, val = u[..., :D_FF], u[..., D_FF:]
    gate = dwconv(gate, p['ffn_conv_w'][l], FFN_PAD_L) + p['ffn_conv_b'][l]
    return (jax.nn.silu(gate) * val) @ p['ffn_w_out'][l]


def layer(x, mod, p, l, ctx):
    sh1, sc1, g1, sh2, sc2, g2 = jnp.split(mod[:, None, :], 6, axis=-1)
    h = rmsnorm(x, p['norm1'][l]) * (1.0 + sc1) + sh1
    out, new = mixer_block(h, p, l, ctx)
    x = x + g1 * out
    h = rmsnorm(x, p['norm2'][l]) * (1.0 + sc2) + sh2
    x = x + g2 * conv_ffn(h, p, l)
    return x, new


def setup_inputs(seed: int = 0) -> dict:
    key = jax.random.key(seed)
    cnt = [0]

    def nk():
        cnt[0] += 1
        return jax.random.fold_in(key, cnt[0])

    def nrm(shape, scale):
        return jax.random.normal(nk(), shape, jnp.float32) * scale

    def uni(shape, lo, hi):
        return jax.random.uniform(nk(), shape, jnp.float32, lo, hi)

    W, D, G, N, P = BRANCH_W, D_MODEL, S5_NGROUP, S5_STATE, S5_GROUP
    lru_a = uni((DEPTH, 2, W), 0.9, 0.999) ** (1.0 / LRU_C)
    return {
        'x_prompt': nrm((BATCH, SEQ, D), 1.0),
        'x_sample': nrm((DEC_BATCH, DEC_SEQ, D), 1.0),
        'cache_k': nrm((DEC_BATCH, DEPTH, PAST_LEN, N_KV_HEADS, HEAD_DIM), 1.0),
        'cache_v': nrm((DEC_BATCH, DEPTH, PAST_LEN, N_KV_HEADS, HEAD_DIM), 1.0),
        'state_lru': nrm((DEC_BATCH, DEPTH, 2, W), 0.5),
        'state_s5': nrm((DEC_BATCH, DEPTH, 2, 2, G, N), 0.5),
        'state_wkv': nrm((DEC_BATCH, DEPTH, 2, RWKV_NH, RWKV_HEAD, RWKV_HEAD), 0.5),
        'c': nrm((DEC_BATCH, D), 1.0),
        'c_ctx': nrm((D,), 1.0),
        'mod_w': nrm((DEPTH, D, 6 * D), 0.5 * D ** -0.5),
        'mod_b': nrm((DEPTH, 6 * D), 0.02),
        'norm1': 1.0 + nrm((DEPTH, D), 0.02),
        'norm2': 1.0 + nrm((DEPTH, D), 0.02),
        'norm_final': 1.0 + nrm((D,), 0.02),
        'w_in': nrm((DEPTH, D, D_IN), D ** -0.5),
        'lru_conv_w': nrm((DEPTH, LRU_CONV_W, W), LRU_CONV_W ** -0.5),
        'lru_conv_b': nrm((DEPTH, W), 0.02),
        'lru_wa': nrm((DEPTH, 2, LRU_BLOCKS, LRU_BS, LRU_BS), LRU_BS ** -0.5),
        'lru_ba': nrm((DEPTH, 2, W), 0.02),
        'lru_wx': nrm((DEPTH, 2, LRU_BLOCKS, LRU_BS, LRU_BS), LRU_BS ** -0.5),
        'lru_bx': nrm((DEPTH, 2, W), 0.02),
        'lru_lam': jnp.log(lru_a) - jnp.log1p(-lru_a),
        's5_a_re': -0.5 + nrm((DEPTH, 2, G, N), 0.01),
        's5_a_im': math.pi * jnp.arange(N, dtype=jnp.float32) + nrm((DEPTH, 2, G, N), 0.01),
        's5_log_dt': uni((DEPTH, 2, G), math.log(1e-3), math.log(1e-1)),
        's5_b_re': nrm((DEPTH, 2, G, N, P), (2 * P) ** -0.5),
        's5_b_im': nrm((DEPTH, 2, G, N, P), (2 * P) ** -0.5),
        's5_c_re': nrm((DEPTH, 2, G, P, N), (2 * N) ** -0.5),
        's5_c_im': nrm((DEPTH, 2, G, P, N), (2 * N) ** -0.5),
        's5_d': nrm((DEPTH, W), 1.0),
        's5_glu_w': nrm((DEPTH, W, W), W ** -0.5),
        's5_glu_b': nrm((DEPTH, W), 0.02),
        'attn_sink': nrm((DEPTH, N_KV_HEADS, Q_PER_KV), 0.5),
        'rwkv_mu': uni((DEPTH, 3, W), 0.0, 1.0),
        'rwkv_w0': uni((DEPTH, 2, W), -6.0, 0.0),
        'rwkv_w2': nrm((DEPTH, 2, W_LORA, W), 0.5 * W_LORA ** -0.5),
        'rwkv_a0': nrm((DEPTH, 2, W), 0.2),
        'rwkv_a2': nrm((DEPTH, 2, A_LORA, W), 0.5 * A_LORA ** -0.5),
        'rwkv_g2': nrm((DEPTH, G_LORA, W), G_LORA ** -0.5),
        'rwkv_k_k': 0.85 + nrm((DEPTH, W), 0.02),
        'rwkv_k_a': 1.0 + nrm((DEPTH, W), 0.02),
        'rwkv_r_k': nrm((DEPTH, RWKV_NH, RWKV_HEAD), 0.1),
        'rwkv_ln_w': 1.0 + nrm((DEPTH, W), 0.02),
        'rwkv_ln_b': nrm((DEPTH, W), 0.02),
        'branch_w': nrm((DEPTH, N_BRANCH, W, D), W ** -0.5),
        'w_out': nrm((DEPTH, D, D), D ** -0.5),
        'ffn_w_in': nrm((DEPTH, D, 2 * D_FF), D ** -0.5),
        'ffn_conv_w': nrm((DEPTH, FFN_CONV_W, D_FF), FFN_CONV_W ** -0.5),
        'ffn_conv_b': nrm((DEPTH, D_FF), 0.02),
        'ffn_w_out': nrm((DEPTH, D_FF, D), D_FF ** -0.5),
    }


def reference(x_prompt, x_sample, cache_k, cache_v, state_lru, state_s5, state_wkv, c, c_ctx,
              mod_w, mod_b, norm1, norm2, norm_final, w_in,
              lru_conv_w, lru_conv_b, lru_wa, lru_ba, lru_wx, lru_bx, lru_lam,
              s5_a_re, s5_a_im, s5_log_dt, s5_b_re, s5_b_im, s5_c_re, s5_c_im, s5_d, s5_glu_w, s5_glu_b,
              attn_sink,
              rwkv_mu, rwkv_w0, rwkv_w2, rwkv_a0, rwkv_a2, rwkv_g2, rwkv_k_k, rwkv_k_a, rwkv_r_k,
              rwkv_ln_w, rwkv_ln_b,
              branch_w, w_out, ffn_w_in, ffn_conv_w, ffn_conv_b, ffn_w_out):
    p = dict(norm1=norm1, norm2=norm2, w_in=w_in,
             lru_conv_w=lru_conv_w, lru_conv_b=lru_conv_b, lru_wa=lru_wa, lru_ba=lru_ba,
             lru_wx=lru_wx, lru_bx=lru_bx, lru_lam=lru_lam,
             s5_a_re=s5_a_re, s5_a_im=s5_a_im, s5_log_dt=s5_log_dt, s5_b_re=s5_b_re, s5_b_im=s5_b_im,
             s5_c_re=s5_c_re, s5_c_im=s5_c_im, s5_d=s5_d, s5_glu_w=s5_glu_w, s5_glu_b=s5_glu_b,
             attn_sink=attn_sink,
             rwkv_mu=rwkv_mu, rwkv_w0=rwkv_w0, rwkv_w2=rwkv_w2, rwkv_a0=rwkv_a0, rwkv_a2=rwkv_a2,
             rwkv_g2=rwkv_g2, rwkv_k_k=rwkv_k_k, rwkv_k_a=rwkv_k_a, rwkv_r_k=rwkv_r_k,
             rwkv_ln_w=rwkv_ln_w, rwkv_ln_b=rwkv_ln_b,
             branch_w=branch_w, w_out=w_out, ffn_w_in=ffn_w_in, ffn_conv_w=ffn_conv_w,
             ffn_conv_b=ffn_conv_b, ffn_w_out=ffn_w_out)

    xp = x_prompt
    ks, vs, lrus, s5s, wkvs = [], [], [], [], []
    for l in range(DEPTH):
        mod_ctx = jax.nn.silu(c_ctx)[None, :] @ mod_w[l] + mod_b[l]
        xp, new = layer(xp, mod_ctx, p, l, None)
        k_l, v_l, lru_l, s5_l, wkv_l = new
        ks.append(k_l)
        vs.append(v_l)
        lrus.append(lru_l)
        s5s.append(s5_l)
        wkvs.append(wkv_l)
    y_prompt = rmsnorm(xp, norm_final)

    xs = x_sample
    for l in range(DEPTH):
        mod_lat = jax.nn.silu(c) @ mod_w[l] + mod_b[l]
        ctx = dict(k=cache_k[:, l], v=cache_v[:, l], lru=state_lru[:, l],
                   s5=state_s5[:, l], wkv=state_wkv[:, l])
        xs, _ = layer(xs, mod_lat, p, l, ctx)
    y_sample = rmsnorm(xs, norm_final)

    new_cache_k = jnp.stack(ks, axis=1)
    new_cache_v = jnp.stack(vs, axis=1)
    new_state_lru = jnp.stack(lrus, axis=1)
    new_state_s5 = jnp.stack(s5s, axis=1)
    new_state_wkv = jnp.stack(wkvs, axis=1)
    return (y_prompt, y_sample, new_cache_k, new_cache_v, new_state_lru, new_state_s5, new_state_wkv)
```

```python
import functools
import math

import jax
import jax.numpy as jnp
from jax import lax
from jax.experimental import pallas as pl
from jax.experimental.pallas import tpu as pltpu

F32 = jnp.float32
BF16 = jnp.bfloat16
HIGHEST = lax.Precision.HIGHEST

D_MODEL = 2048
DEPTH = 2
PAST_LEN = 256
GRID_W = 64
N_BRANCH = 4
BRANCH_W = 512
EPS = 1e-6
LRU_BLOCKS = 8
LRU_BS = 64
LRU_CONV_W = 4
LRU_PAD_L = 2
LRU_C = 8.0
S5_GROUP = 16
S5_NGROUP = 32
S5_STATE = 64
HEAD_DIM = 64
N_Q_HEADS = 8
N_KV_HEADS = 2
Q_PER_KV = 4
KV_W = 128
WINDOW = 128
BLOCK = 128
ROPE_BASE = 10000.0
ROPE_NF = 16
ATTN_SCALE = 1.0 / math.sqrt(HEAD_DIM)
NEG_INF = -1e30
RWKV_HEAD = 64
RWKV_NH = 8
W_LORA = 64
A_LORA = 64
G_LORA = 128
RWKV_GN_EPS = 64e-5
D_FF = 5504
D_FF_PAD = 5632
LANES = 128
SUBLANES = 8
VMEM_LIMIT = 56 * 2 ** 20

COL_AX, COL_AGATE, COL_BU, COL_CQ = 0, 512, 1024, 1536
COL_DR, COL_DK, COL_DV, COL_LORA = 2048, 2560, 3072, 3584
COL_CK, COL_CV, COL_MERGE = 3840, 3968, 4096
D_IN = COL_MERGE + N_BRANCH * D_MODEL
LORA_W = W_LORA + A_LORA + G_LORA


def _cparams(n_grid):
    return pltpu.CompilerParams(dimension_semantics=("arbitrary",) * n_grid, vmem_limit_bytes=VMEM_LIMIT)


def _bdot(a, b):
    return jnp.dot(a.astype(BF16), b.astype(BF16), preferred_element_type=F32)


def _bdot_nt(a, b):
    return lax.dot_general(a.astype(BF16), b.astype(BF16), (((1,), (1,)), ((), ())),
                           preferred_element_type=F32)


def _gelu(x):
    return jax.nn.gelu(x)


def _mod_kernel(c_ref, w_ref, b_ref, o_ref):
    c = c_ref[...]
    s = c * jax.nn.sigmoid(c)
    o_ref[0] = jnp.dot(s, w_ref[0], preferred_element_type=F32, precision=HIGHEST) + b_ref[0]


def modulation(c_all, mod_w, mod_b):
    R, D = c_all.shape
    N = mod_w.shape[2]
    bn = 1024
    return pl.pallas_call(
        _mod_kernel,
        grid=(DEPTH, N // bn),
        in_specs=[pl.BlockSpec((R, D), lambda l, j: (0, 0)),
                  pl.BlockSpec((1, D, bn), lambda l, j: (l, 0, j)),
                  pl.BlockSpec((1, 1, bn), lambda l, j: (l, 0, j))],
        out_specs=pl.BlockSpec((1, R, bn), lambda l, j: (l, 0, j)),
        out_shape=jax.ShapeDtypeStruct((DEPTH, R, N), F32),
        compiler_params=_cparams(2),
        name="modulation",
    )(c_all, mod_w, mod_b.reshape(DEPTH, 1, N))


def _nmm_kernel(x_ref, g_ref, sc_ref, sh_ref, w_ref, o_ref, h_scr):
    @pl.when(pl.program_id(1) == 0)
    def _():
        x = x_ref[...]
        y = x * lax.rsqrt(jnp.mean(x * x, axis=-1, keepdims=True) + EPS) * g_ref[...]
        h_scr[...] = (y * (1.0 + sc_ref[0]) + sh_ref[0]).astype(BF16)

    o_ref[...] = jnp.dot(h_scr[...], w_ref[...], preferred_element_type=F32)


def norm_mod_matmul(x, g, sc, sh, w, L, bm, bn):
    M, D = x.shape
    N = w.shape[1]
    nb = sc.shape[0]
    if nb > 1:
        assert L % bm == 0
        per = L // bm
        bidx = lambda i, j: (i // per, 0, 0)
    else:
        bidx = lambda i, j: (0, 0, 0)
    return pl.pallas_call(
        _nmm_kernel,
        grid=(M // bm, N // bn),
        in_specs=[pl.BlockSpec((bm, D), lambda i, j: (i, 0)),
                  pl.BlockSpec((1, D), lambda i, j: (0, 0)),
                  pl.BlockSpec((1, 1, D), bidx),
                  pl.BlockSpec((1, 1, D), bidx),
                  pl.BlockSpec((D, bn), lambda i, j: (0, j))],
        out_specs=pl.BlockSpec((bm, bn), lambda i, j: (i, j)),
        out_shape=jax.ShapeDtypeStruct((M, N), F32),
        scratch_shapes=[pltpu.VMEM((bm, D), BF16)],
        compiler_params=_cparams(2),
        name="norm_mod_matmul",
    )(x, g.reshape(1, D), sc, sh, w)


def _lru_kernel(xp_ref, cw_ref, cb_ref, wa_ref, ba_ref, wx_ref, bx_ref, sp_ref, h0_ref,
                out_ref, fin_ref, a_scr, b_scr, *, B, L, Tc):
    nch = L // Tc
    R = Tc * B
    for d in range(2):
        sp = sp_ref[d]

        def chunk(ci, h, d=d, sp=sp):
            c = ci if d == 0 else nch - 1 - ci
            base = pl.multiple_of(c * R, R)
            xa = cb_ref[...] + xp_ref[pl.ds(base, R), :] * cw_ref[0:1, :]
            for j in range(1, LRU_CONV_W):
                xa = xa + xp_ref[pl.ds(base + j * B, R), :] * cw_ref[j:j + 1, :]
            r = jax.nn.sigmoid(_bdot(xa, wa_ref[d]) + ba_ref[d])
            i = jax.nn.sigmoid(_bdot(xa, wx_ref[d]) + bx_ref[d])
            log_a = -LRU_C * r * sp
            a = jnp.exp(log_a)
            a_scr[...] = a
            b_scr[...] = jnp.sqrt(1.0 - a * a) * (i * xa)

            def step(ti, h):
                t = ti if d == 0 else Tc - 1 - ti
                off = pl.multiple_of(t * B, B)
                h = a_scr[pl.ds(off, B), :] * h + b_scr[pl.ds(off, B), :]
                row = pl.ds(base + off, B)
                if d == 0:
                    out_ref[row, :] = h
                else:
                    out_ref[row, :] = out_ref[row, :] + h
                return h

            return lax.fori_loop(0, Tc, step, h, unroll=8)

        fin_ref[d] = lax.fori_loop(0, nch, chunk, h0_ref[d])


def lru_mixer(xp_tm, cw, cb, wa, ba, wx, bx, sp, h0, B, L):
    W = BRANCH_W
    Tc = 32
    kern = functools.partial(_lru_kernel, B=B, L=L, Tc=Tc)
    vec = lambda: pl.BlockSpec((2, 1, LANES), lambda s: (0, 0, s))
    return pl.pallas_call(
        kern,
        grid=(W // LANES,),
        in_specs=[pl.BlockSpec(((L + 3) * B, LANES), lambda s: (0, s)),
                  pl.BlockSpec((LRU_CONV_W, LANES), lambda s: (0, s)),
                  pl.BlockSpec((1, LANES), lambda s: (0, s)),
                  pl.BlockSpec((2, LANES, LANES), lambda s: (0, s, s)), vec(),
                  pl.BlockSpec((2, LANES, LANES), lambda s: (0, s, s)), vec(),
                  vec(),
                  pl.BlockSpec((2, B, LANES), lambda s: (0, 0, s))],
        out_specs=[pl.BlockSpec((L * B, LANES), lambda s: (0, s)),
                   pl.BlockSpec((2, B, LANES), lambda s: (0, 0, s))],
        out_shape=[jax.ShapeDtypeStruct((L * B, W), F32), jax.ShapeDtypeStruct((2, B, W), F32)],
        scratch_shapes=[pltpu.VMEM((Tc * B, LANES), F32), pltpu.VMEM((Tc * B, LANES), F32)],
        compiler_params=_cparams(1),
        name="lru_mixer",
    )(xp_tm, cw, cb.reshape(1, W), wa, ba.reshape(2, 1, W), wx, bx.reshape(2, 1, W), sp.reshape(2, 1, W), h0)


S5_SLICES = BRANCH_W // LANES
S5_HALF = (S5_NGROUP // S5_SLICES) * S5_STATE


def _s5_kernel(u_ref, wb_ref, ar_ref, ai_ref, wc_ref, dv_ref, h0_ref, out_ref, fin_ref, hs_scr, *, B, L, Tc):
    nch = L // Tc
    R = Tc * B
    H = S5_HALF
    out_ref[...] = u_ref[...] * dv_ref[...]
    for d in range(2):
        ar = ar_ref[0, d]
        ai = ai_ref[0, d]

        def chunk(ci, carry, d=d, ar=ar, ai=ai):
            c = ci if d == 0 else nch - 1 - ci
            base = pl.multiple_of(c * R, R)
            hs_scr[...] = _bdot(u_ref[pl.ds(base, R), :], wb_ref[0, d])

            def step(ti, carry):
                hr, hi = carry
                t = ti if d == 0 else Tc - 1 - ti
                row = pl.ds(pl.multiple_of(t * B, B), B)
                nr = ar * hr - ai * hi + hs_scr[row, 0:H]
                ni = ar * hi + ai * hr + hs_scr[row, H:2 * H]
                hs_scr[row, 0:H] = nr
                hs_scr[row, H:2 * H] = ni
                return nr, ni

            carry = lax.fori_loop(0, Tc, step, carry, unroll=4)
            rows = pl.ds(base, R)
            out_ref[rows, :] = out_ref[rows, :] + _bdot(hs_scr[...], wc_ref[0, d])
            return carry

        hr, hi = lax.fori_loop(0, nch, chunk, (h0_ref[0, d, :, 0:H], h0_ref[0, d, :, H:2 * H]))
        fin_ref[0, d, :, 0:H] = hr
        fin_ref[0, d, :, H:2 * H] = hi


def s5_mixer(u_tm, wb, ar, ai, wc, dvec, h0, B, L):
    W = BRANCH_W
    Tc = 32
    H2 = 2 * S5_HALF
    kern = functools.partial(_s5_kernel, B=B, L=L, Tc=Tc)
    return pl.pallas_call(
        kern,
        grid=(S5_SLICES,),
        in_specs=[pl.BlockSpec((L * B, LANES), lambda s: (0, s)),
                  pl.BlockSpec((1, 2, LANES, H2), lambda s: (s, 0, 0, 0)),
                  pl.BlockSpec((1, 2, 1, S5_HALF), lambda s: (s, 0, 0, 0)),
                  pl.BlockSpec((1, 2, 1, S5_HALF), lambda s: (s, 0, 0, 0)),
                  pl.BlockSpec((1, 2, H2, LANES), lambda s: (s, 0, 0, 0)),
                  pl.BlockSpec((1, LANES), lambda s: (0, s)),
                  pl.BlockSpec((1, 2, B, H2), lambda s: (s, 0, 0, 0))],
        out_specs=[pl.BlockSpec((L * B, LANES), lambda s: (0, s)),
                   pl.BlockSpec((1, 2, B, H2), lambda s: (s, 0, 0, 0))],
        out_shape=[jax.ShapeDtypeStruct((L * B, W), F32), jax.ShapeDtypeStruct((S5_SLICES, 2, B, H2), F32)],
        scratch_shapes=[pltpu.VMEM((Tc * B, H2), F32)],
        compiler_params=_cparams(1),
        name="s5_mixer",
    )(u_tm, wb, ar, ai, wc, dvec.reshape(1, W), h0)


def s5_params(a_re, a_im, log_dt, b_re, b_im, c_re, c_im):
    dt = jnp.exp(log_dt)[..., None]
    mag = jnp.exp(a_re * dt)
    abar_r = mag * jnp.cos(a_im * dt)
    abar_i = mag * jnp.sin(a_im * dt)
    den = a_re * a_re + a_im * a_im
    nr = abar_r - 1.0
    fr = (nr * a_re + abar_i * a_im) / den
    fi = (abar_i * a_re - nr * a_im) / den
    bbar_r = fr[..., None] * b_re - fi[..., None] * b_im
    bbar_i = fr[..., None] * b_im + fi[..., None] * b_re
    gs = S5_NGROUP // S5_SLICES
    eye = jnp.eye(gs, dtype=F32)
    bb = jnp.stack([bbar_r, bbar_i], axis=0).reshape(2, 2, S5_SLICES, gs, S5_STATE, S5_GROUP)
    wb = jnp.einsum('rdsgnp,gh->sdgprhn', bb, eye).reshape(S5_SLICES, 2, LANES, 2 * S5_HALF)
    cc = jnp.stack([c_re, -c_im], axis=0).reshape(2, 2, S5_SLICES, gs, S5_GROUP, S5_STATE)
    wc = jnp.einsum('rdsgpn,gh->sdrhngp', cc, eye).reshape(S5_SLICES, 2, 2 * S5_HALF, LANES)
    ar = abar_r.reshape(2, S5_SLICES, 1, S5_HALF).transpose(1, 0, 2, 3)
    ai = abar_i.reshape(2, S5_SLICES, 1, S5_HALF).transpose(1, 0, 2, 3)
    return wb.astype(BF16), ar, ai, wc.astype(BF16)


def _rope(x, cos, sina, sinb):
    return x * cos + pltpu.roll(x, ROPE_NF, 1) * sina + pltpu.roll(x, LANES - ROPE_NF, 1) * sinb


def _attend(qg, segs, sink_col):
    scores = []
    m = sink_col
    for k, _, mask in segs:
        s = _bdot_nt(qg, k) * ATTN_SCALE
        if mask is not None:
            s = jnp.where(mask, s, NEG_INF)
        scores.append(s)
        m = jnp.maximum(m, jnp.max(s, axis=-1, keepdims=True))
    den = jnp.exp(sink_col - m)
    o = None
    for s, (_, v, _) in zip(scores, segs):
        p = jnp.exp(s - m)
        den = den + jnp.sum(p, axis=-1, keepdims=True)
        pv = _bdot(p, v)
        o = pv if o is None else o + pv
    return o / den


def _sink_col(sink_ref, kvh):
    return jnp.concatenate([jnp.full((BLOCK, 1), sink_ref[kvh * Q_PER_KV + g], F32) for g in range(Q_PER_KV)], axis=0)


def _stack_heads(q, kvh):
    return jnp.concatenate([q[:, (kvh * Q_PER_KV + g) * HEAD_DIM:(kvh * Q_PER_KV + g + 1) * HEAD_DIM]
                            for g in range(Q_PER_KV)], axis=0)


def _store_heads(o_ref, og, kvh):
    for g in range(Q_PER_KV):
        h = kvh * Q_PER_KV + g
        o_ref[:, h * HEAD_DIM:(h + 1) * HEAD_DIM] = og[g * BLOCK:(g + 1) * BLOCK]


def _ctx_attn_kernel(sink_ref, q_ref, k_ref, v_ref, o_ref):
    q = q_ref[...]
    k = k_ref[...]
    v = v_ref[...]
    for kvh in range(N_KV_HEADS):
        ks = k[:, kvh * HEAD_DIM:(kvh + 1) * HEAD_DIM]
        vs = v[:, kvh * HEAD_DIM:(kvh + 1) * HEAD_DIM]
        og = _attend(_stack_heads(q, kvh), [(ks, vs, None)], _sink_col(sink_ref, kvh))
        _store_heads(o_ref, og, kvh)


def ctx_attention(P, sink, B, L):
    nb = L // BLOCK
    return pl.pallas_call(
        _ctx_attn_kernel,
        grid=(B, nb),
        in_specs=[pl.BlockSpec(memory_space=pltpu.SMEM),
                  pl.BlockSpec((BLOCK, BRANCH_W), lambda b, j: (b * nb + j, COL_CQ // BRANCH_W)),
                  pl.BlockSpec((L, KV_W), lambda b, j: (b, COL_CK // KV_W)),
                  pl.BlockSpec((L, KV_W), lambda b, j: (b, COL_CV // KV_W))],
        out_specs=pl.BlockSpec((BLOCK, BRANCH_W), lambda b, j: (b * nb + j, 0)),
        out_shape=jax.ShapeDtypeStruct((B * L, BRANCH_W), F32),
        compiler_params=_cparams(2),
        name="ctx_attention",
    )(sink.reshape(N_Q_HEADS), P, P, P)


def _lat_attn_kernel(sink_ref, q_ref, kp_ref, kc_ref, kn_ref, vp_ref, vc_ref, vn_ref, ck_ref, cv_ref,
                     cq_ref, saq_ref, sbq_ref, cp_ref, sap_ref, sbp_ref, cn_ref, san_ref, sbn_ref, o_ref, *, nb):
    j = pl.program_id(1)
    cq, saq, sbq = cq_ref[...], saq_ref[...], sbq_ref[...]
    q = jnp.concatenate([_rope(q_ref[:, c * LANES:(c + 1) * LANES], cq, saq, sbq)
                         for c in range(BRANCH_W // LANES)], axis=1)
    kp = _rope(kp_ref[...], cp_ref[...], sap_ref[...], sbp_ref[...])
    kc = _rope(kc_ref[...], cq, saq, sbq)
    kn = _rope(kn_ref[...], cn_ref[...], san_ref[...], sbn_ref[...])
    R = Q_PER_KV * BLOCK
    r = lax.broadcasted_iota(jnp.int32, (R, BLOCK), 0) % BLOCK
    c = lax.broadcasted_iota(jnp.int32, (R, BLOCK), 1)
    mask_p = (c >= r) & (j > 0)
    mask_n = (c <= r) & (j < nb - 1)
    ck = ck_ref[0, 0]
    cv = cv_ref[0, 0]
    for kvh in range(N_KV_HEADS):
        sl = slice(kvh * HEAD_DIM, (kvh + 1) * HEAD_DIM)
        segs = [(ck[:, sl], cv[:, sl], None),
                (kp[:, sl], vp_ref[:, sl], mask_p),
                (kc[:, sl], vc_ref[:, sl], None),
                (kn[:, sl], vn_ref[:, sl], mask_n)]
        og = _attend(_stack_heads(q, kvh), segs, _sink_col(sink_ref, kvh))
        _store_heads(o_ref, og, kvh)


def lat_attention(P, cache_k, cache_v, layer, sink, rope_tabs, B, L):
    nb = L // BLOCK
    kcol, vcol = COL_CK // KV_W, COL_CV // KV_W
    prev = lambda b, j: b * nb + jnp.maximum(j - 1, 0)
    nxt = lambda b, j: b * nb + jnp.minimum(j + 1, nb - 1)
    kv = lambda rowf, col: pl.BlockSpec((BLOCK, KV_W), lambda b, j: (rowf(b, j), col))
    cur = lambda b, j: b * nb + j
    tab = lambda f: pl.BlockSpec((BLOCK, LANES), f)
    tq = lambda b, j: (j, 0)
    tp = lambda b, j: (jnp.maximum(j - 1, 0), 0)
    tn = lambda b, j: (jnp.minimum(j + 1, nb - 1), 0)
    cos, sina, sinb = rope_tabs
    cache = pl.BlockSpec((1, 1, PAST_LEN, KV_W), lambda b, j: (b, layer, 0, 0))
    return pl.pallas_call(
        functools.partial(_lat_attn_kernel, nb=nb),
        grid=(B, nb),
        in_specs=[pl.BlockSpec(memory_space=pltpu.SMEM),
                  pl.BlockSpec((BLOCK, BRANCH_W), lambda b, j: (b * nb + j, COL_CQ // BRANCH_W)),
                  kv(prev, kcol), kv(cur, kcol), kv(nxt, kcol),
                  kv(prev, vcol), kv(cur, vcol), kv(nxt, vcol),
                  cache, cache,
                  tab(tq), tab(tq), tab(tq), tab(tp), tab(tp), tab(tp), tab(tn), tab(tn), tab(tn)],
        out_specs=pl.BlockSpec((BLOCK, BRANCH_W), lambda b, j: (b * nb + j, 0)),
        out_shape=jax.ShapeDtypeStruct((B * L, BRANCH_W), F32),
        compiler_params=_cparams(2),
        name="lat_attention",
    )(sink.reshape(N_Q_HEADS), P, P, P, P, P, P, P,
      cache_k.reshape(cache_k.shape[0], DEPTH, PAST_LEN, KV_W), cache_v.reshape(cache_v.shape[0], DEPTH, PAST_LEN, KV_W),
      cos, sina, sinb, cos, sina, sinb, cos, sina, sinb)


def rope_tables(L):
    t = jnp.arange(L)
    inv = ROPE_BASE ** (-jnp.arange(ROPE_NF, dtype=F32) / ROPE_NF)
    ang_r = (t // GRID_W).astype(F32)[:, None] * inv
    ang_c = (t % GRID_W).astype(F32)[:, None] * inv
    z = jnp.zeros((L, ROPE_NF), F32)
    cos = jnp.concatenate([jnp.cos(ang_r)] * 2 + [jnp.cos(ang_c)] * 2, axis=1)
    sina = jnp.concatenate([z, jnp.sin(ang_r), z, jnp.sin(ang_c)], axis=1)
    sinb = jnp.concatenate([-jnp.sin(ang_r), z, -jnp.sin(ang_c), z], axis=1)
    return tuple(jnp.tile(a, (1, LANES // HEAD_DIM)) for a in (cos, sina, sinb))


RWKV_CHUNK = 64


def _split(a):
    hi = a.astype(BF16)
    return hi, (a - hi.astype(F32)).astype(BF16)


def _bdot3(a, b):
    ah, al = _split(a)
    bh, bl = _split(b)
    return jnp.dot(jnp.concatenate([ah, al, ah], axis=1), jnp.concatenate([bh, bh, bl], axis=0),
                   preferred_element_type=F32)


_rdot_gram = _bdot_nt
_rdot_inv = _bdot3
_rdot_u = _bdot
_rdot_y = _bdot
_rdot_t = _bdot


def _head_sum(x, m0, m1):
    s0 = jnp.sum(x * m0, axis=-1, keepdims=True)
    s1 = jnp.sum(x * m1, axis=-1, keepdims=True)
    return s0 * m0 + s1 * m1


def _rwkv_kernel(xr_ref, xk_ref, xv_ref, lo_ref, mu_ref, w0_ref, w2_ref, a0_ref, a2_ref, g2_ref,
                 kk_ref, ka_ref, rk_ref, lnw_ref, lnb_ref, t0_ref,
                 y_ref, tfin_ref,
                 r_scr, kh_scr, v_scr, kn_scr, lw_scr, a_scr, kd_scr, y_scr, *, L):
    C = RWKV_CHUNK
    nch = L // C
    lane = lax.broadcasted_iota(jnp.int32, (1, LANES), 1)
    m0 = (lane < RWKV_HEAD).astype(F32)
    m1 = 1.0 - m0
    row1 = lax.broadcasted_iota(jnp.int32, (L, 1), 0)

    def shift(x, mu):
        prev = jnp.where(row1 == 0, 0.0, pltpu.roll(x, 1, 0))
        nxt = jnp.where(row1 == L - 1, 0.0, pltpu.roll(x, L - 1, 0))
        return x + mu * (0.5 * (prev + nxt) - x)

    r = shift(xr_ref[...], mu_ref[0:1, :])
    kh = shift(xk_ref[...], mu_ref[1:2, :])
    vh = shift(xv_ref[...], mu_ref[2:3, :])
    r_scr[...] = r
    kh_scr[...] = kh
    v_scr[...] = vh
    kk = kh * kk_ref[...]
    kn_scr[...] = kk * lax.rsqrt(_head_sum(kk * kk, m0, m1) + 1e-12)
    lo = lo_ref[...]
    tw = jnp.tanh(lo)

    ri = lax.broadcasted_iota(jnp.int32, (2 * C, 2 * C), 0)
    ci = lax.broadcasted_iota(jnp.int32, (2 * C, 2 * C), 1)
    same = (ri // C) == (ci // C)
    eye2 = (ri == ci).astype(F32)
    ti = lax.broadcasted_iota(jnp.int32, (C, C), 0)
    si = lax.broadcasted_iota(jnp.int32, (C, C), 1)

    def stack(x):
        return jnp.concatenate([x * m0, x * m1], axis=0)

    for d in range(2):
        z = w0_ref[d] + _bdot(tw, w2_ref[d])
        lw_scr[...] = -math.exp(-0.5) * jax.nn.sigmoid(z)
        a = jax.nn.sigmoid(a0_ref[d] + _bdot(lo, a2_ref[d]))
        a_scr[...] = a
        kd_scr[...] = kh_scr[...] * (1.0 + (a - 1.0) * ka_ref[...])
        if d == 0:
            strict = same & ((ci % C) < (ri % C))
            incl = same & ((ci % C) <= (ri % C))
            tri = (si <= ti).astype(F32)
        else:
            strict = same & ((ci % C) > (ri % C))
            incl = same & ((ci % C) >= (ri % C))
            tri = (si >= ti).astype(F32)

        def chunk(ci_, T, d=d, strict=strict, incl=incl, tri=tri):
            c = ci_ if d == 0 else nch - 1 - ci_
            rows = pl.ds(pl.multiple_of(c * C, C), C)
            lw = lw_scr[rows, :]
            cum = jnp.dot(tri, lw, preferred_element_type=F32, precision=HIGHEST)
            p_in = jnp.exp(cum)
            p_ex = jnp.exp(cum - lw)
            ip = jnp.exp(-cum)
            pc = p_in[C - 1:C, :] if d == 0 else p_in[0:1, :]
            kn = kn_scr[rows, :]
            KAP = stack(kn * p_ex)
            RT = stack(r_scr[rows, :] * p_in)
            bt = -(a_scr[rows, :] * kn) * ip
            kt = kd_scr[rows, :] * ip
            BT = stack(bt)
            KT = stack(kt)
            V = stack(v_scr[rows, :])
            G = _rdot_gram(jnp.concatenate([KAP, RT], axis=0), jnp.concatenate([BT, KT], axis=0))
            aab = jnp.where(strict, G[0:2 * C, 0:2 * C], 0.0)
            aak = jnp.where(strict, G[0:2 * C, 2 * C:4 * C], 0.0)
            grb = jnp.where(incl, G[2 * C:4 * C, 0:2 * C], 0.0)
            grk = jnp.where(incl, G[2 * C:4 * C, 2 * C:4 * C], 0.0)
            X = eye2 + aab
            Pw = aab
            for _ in range(5):
                Pw = _rdot_inv(Pw, Pw)
                X = X + _rdot_inv(X, Pw)
            U = _rdot_u(X, _rdot_u(KAP, T) + _rdot_u(aak, V))
            yst = _rdot_y(RT, T) + _rdot_y(grb, U) + _rdot_y(grk, V)
            y = yst[0:C] + yst[C:2 * C]
            if d == 0:
                y_scr[rows, :] = y
            else:
                y_scr[rows, :] = y_scr[rows, :] + y
            lhs = jnp.concatenate([BT * pc, KT * pc], axis=0)
            rhs = jnp.concatenate([U, V], axis=0)
            pc_col = jnp.sum(eye2 * pc, axis=1, keepdims=True)
            return pc_col * T + _rdot_t(lhs.T, rhs)

        tfin_ref[0, d] = lax.fori_loop(0, nch, chunk, t0_ref[0, d])

    y = y_scr[...]
    inv_n = 1.0 / RWKV_HEAD
    mu = _head_sum(y, m0, m1) * inv_n
    yc = y - mu
    var = _head_sum(yc * yc, m0, m1) * inv_n
    yn = yc * lax.rsqrt(var + RWKV_GN_EPS) * lnw_ref[...] + lnb_ref[...]
    bonus = _head_sum(r_scr[...] * kh_scr[...] * rk_ref[...], m0, m1) * v_scr[...]
    g = _bdot(jax.nn.sigmoid(lo), g2_ref[...])
    y_ref[...] = (yn + bonus) * g


def rwkv_mixer(P, prm, t0, B, L):
    HP = BRANCH_W // LANES
    nl = L
    col = lambda c0: (lambda b, h: (b, c0 // LANES + h))
    vecs = lambda n: pl.BlockSpec((n, LANES), lambda b, h: (0, h))
    dvec = pl.BlockSpec((2, 1, LANES), lambda b, h: (0, 0, h))
    lora_w = pl.BlockSpec((2, LORA_W, LANES), lambda b, h: (0, 0, h))
    seq = lambda: pltpu.VMEM((L, LANES), F32)
    return pl.pallas_call(
        functools.partial(_rwkv_kernel, L=L),
        grid=(B, HP),
        in_specs=[pl.BlockSpec((nl, LANES), col(COL_DR)),
                  pl.BlockSpec((nl, LANES), col(COL_DK)),
                  pl.BlockSpec((nl, LANES), col(COL_DV)),
                  pl.BlockSpec((nl, LORA_W), lambda b, h: (b, COL_LORA // LORA_W)),
                  vecs(3), dvec, lora_w, dvec, lora_w,
                  pl.BlockSpec((LORA_W, LANES), lambda b, h: (0, h)),
                  vecs(1), vecs(1), vecs(1), vecs(1), vecs(1),
                  pl.BlockSpec((1, 2, LANES, LANES), lambda b, h: (b * HP + h, 0, 0, 0))],
        out_specs=[pl.BlockSpec((nl, LANES), lambda b, h: (b, h)),
                   pl.BlockSpec((1, 2, LANES, LANES), lambda b, h: (b * HP + h, 0, 0, 0))],
        out_shape=[jax.ShapeDtypeStruct((B * L, BRANCH_W), F32),
                   jax.ShapeDtypeStruct((B * HP, 2, LANES, LANES), F32)],
        scratch_shapes=[seq() for _ in range(8)],
        compiler_params=_cparams(2),
        name="rwkv_mixer",
    )(P, P, P, P, prm['mu'], prm['w0'], prm['w2'], prm['a0'], prm['a2'], prm['g2'],
      prm['k_k'], prm['k_a'], prm['r_k'], prm['ln_w'], prm['ln_b'], t0)


def _merge_kernel(ag_ref, hs_ref, ys_ref, yc_ref, yd_ref, m0_ref, m1_ref, m2_ref, m3_ref, x_ref, g1_ref,
                  gw_ref, gb_ref, bw_ref, wo_ref, o_ref):
    y_a = _gelu(ag_ref[...]) * hs_ref[...]
    y_s = _gelu(ys_ref[...])
    y_b = y_s * jax.nn.sigmoid(_bdot(y_s, gw_ref[...]) + gb_ref[...])
    acc = None
    for n, (y, m_ref) in enumerate(((y_a, m0_ref), (y_b, m1_ref), (yc_ref[...], m2_ref), (yd_ref[...], m3_ref))):
        t = jax.nn.sigmoid(m_ref[...]) * _bdot(y, bw_ref[n])
        acc = t if acc is None else acc + t
    o_ref[...] = x_ref[...] + g1_ref[0] * _bdot(acc, wo_ref[...])


def merge_out(P, hs, ysum, yc, yd, x, g1, glu_w, glu_b, branch_w, w_out, L, bm):
    M, D = x.shape
    W = BRANCH_W
    nb = g1.shape[0]
    per = max(L // bm, 1)
    bidx = (lambda i: (i // per, 0, 0)) if nb > 1 else (lambda i: (0, 0, 0))
    row = lambda w, c: pl.BlockSpec((bm, w), lambda i: (i, c))
    const = lambda shape: pl.BlockSpec(shape, lambda i: (0,) * len(shape), pipeline_mode=pl.Buffered(1))
    mcol = COL_MERGE // D
    return pl.pallas_call(
        _merge_kernel,
        grid=(M // bm,),
        in_specs=[row(W, COL_AGATE // W), row(W, 0), row(W, 0), row(W, 0), row(W, 0),
                  row(D, mcol), row(D, mcol + 1), row(D, mcol + 2), row(D, mcol + 3),
                  row(D, 0), pl.BlockSpec((1, 1, D), bidx),
                  const((W, W)), const((1, W)), const((N_BRANCH, W, D)), const((D, D))],
        out_specs=row(D, 0),
        out_shape=jax.ShapeDtypeStruct((M, D), F32),
        compiler_params=_cparams(1),
        name="merge_out",
    )(P, hs, ysum, yc, yd, P, P, P, P, x, g1, glu_w, glu_b.reshape(1, W), branch_w, w_out)


def _ffn_out_kernel(gp_ref, gc_ref, gn_ref, val_ref, cw_ref, cb_ref, w_ref, x_ref, g2_ref, o_ref, acc_scr, *, L, bm):
    k = pl.program_id(1)

    @pl.when(k == 0)
    def _():
        acc_scr[...] = jnp.zeros_like(acc_scr)

    g = gc_ref[...]
    t = (pl.program_id(0) * bm + lax.broadcasted_iota(jnp.int32, (bm, 1), 0)) % L
    rloc = lax.broadcasted_iota(jnp.int32, (bm, 1), 0)
    prev = jnp.where(rloc == 0, gp_ref[SUBLANES - 1:SUBLANES, :], pltpu.roll(g, 1, 0))
    prev = jnp.where(t == 0, 0.0, prev)
    nxt = jnp.where(rloc == bm - 1, gn_ref[0:1, :], pltpu.roll(g, bm - 1, 0))
    nxt = jnp.where(t == L - 1, 0.0, nxt)
    gate = prev * cw_ref[0:1, :] + g * cw_ref[1:2, :] + nxt * cw_ref[2:3, :] + cb_ref[...]
    act = gate * jax.nn.sigmoid(gate) * val_ref[...]
    acc_scr[...] += _bdot(act, w_ref[...])

    @pl.when(k == pl.num_programs(1) - 1)
    def _():
        o_ref[...] = x_ref[...] + g2_ref[0] * acc_scr[...]


def ffn_out(U, x, g2, conv_w, conv_b, w, L, bm, bk):
    M, D = x.shape
    nk = D_FF_PAD // bk
    nb = g2.shape[0]
    per = max(L // bm, 1)
    bidx = (lambda i, k: (i // per, 0, 0)) if nb > 1 else (lambda i, k: (0, 0, 0))
    hb = bm // SUBLANES
    last = M // SUBLANES - 1
    return pl.pallas_call(
        functools.partial(_ffn_out_kernel, L=L, bm=bm),
        grid=(M // bm, nk),
        in_specs=[pl.BlockSpec((SUBLANES, bk), lambda i, k: (jnp.maximum(i * hb - 1, 0), k)),
                  pl.BlockSpec((bm, bk), lambda i, k: (i, k)),
                  pl.BlockSpec((SUBLANES, bk), lambda i, k: (jnp.minimum((i + 1) * hb, last), k)),
                  pl.BlockSpec((bm, bk), lambda i, k: (i, nk + k)),
                  pl.BlockSpec((3, bk), lambda i, k: (0, k)),
                  pl.BlockSpec((1, bk), lambda i, k: (0, k)),
                  pl.BlockSpec((bk, D), lambda i, k: (k, 0)),
                  pl.BlockSpec((bm, D), lambda i, k: (i, 0)),
                  pl.BlockSpec((1, 1, D), bidx)],
        out_specs=pl.BlockSpec((bm, D), lambda i, k: (i, 0)),
        out_shape=jax.ShapeDtypeStruct((M, D), F32),
        scratch_shapes=[pltpu.VMEM((bm, D), F32)],
        compiler_params=_cparams(2),
        name="ffn_out",
    )(U, U, U, U, conv_w, conv_b.reshape(1, D_FF_PAD), w, x, g2)


def _final_norm_kernel(x_ref, g_ref, o_ref):
    x = x_ref[...]
    o_ref[...] = x * lax.rsqrt(jnp.mean(x * x, axis=-1, keepdims=True) + EPS) * g_ref[...]


def final_norm(x, g, bm):
    M, D = x.shape
    return pl.pallas_call(
        _final_norm_kernel,
        grid=(M // bm,),
        in_specs=[pl.BlockSpec((bm, D), lambda i: (i, 0)), pl.BlockSpec((1, D), lambda i: (0, 0))],
        out_specs=pl.BlockSpec((bm, D), lambda i: (i, 0)),
        out_shape=jax.ShapeDtypeStruct((M, D), F32),
        compiler_params=_cparams(1),
        name="final_norm",
    )(x, g.reshape(1, D))


def _block_diag(w):
    n, bs, _ = w.shape
    return jnp.einsum('ncd,nm->ncmd', w, jnp.eye(n, dtype=w.dtype)).reshape(n * bs, n * bs)


def _pad_rows(w, start, total):
    k = w.shape[-2]
    pad = [(0, 0)] * (w.ndim - 2) + [(start, total - start - k), (0, 0)]
    return jnp.pad(w, pad)


def prepare_layer(l, w):
    W = BRANCH_W
    win = w['w_in'][l]
    offs = [0]
    for width in (W, W, W, W, KV_W, KV_W, W, W, W, W_LORA, A_LORA, G_LORA, N_BRANCH * D_MODEL):
        offs.append(offs[-1] + width)
    sec = lambda i: win[:, offs[i]:offs[i + 1]]
    order = (0, 1, 2, 3, 6, 7, 8, 9, 10, 11, 4, 5, 12)
    p = {'w_in': jnp.concatenate([sec(i) for i in order], axis=1).astype(BF16)}
    p['lru_wa'] = jnp.stack([_block_diag(w['lru_wa'][l, d]) for d in range(2)]).astype(BF16)
    p['lru_wx'] = jnp.stack([_block_diag(w['lru_wx'][l, d]) for d in range(2)]).astype(BF16)
    p['lru_sp'] = jax.nn.softplus(-w['lru_lam'][l])
    p['s5'] = s5_params(w['s5_a_re'][l], w['s5_a_im'][l], w['s5_log_dt'][l], w['s5_b_re'][l], w['s5_b_im'][l],
                        w['s5_c_re'][l], w['s5_c_im'][l])
    p['rwkv'] = dict(
        mu=w['rwkv_mu'][l], w0=w['rwkv_w0'][l].reshape(2, 1, W), a0=w['rwkv_a0'][l].reshape(2, 1, W),
        w2=_pad_rows(w['rwkv_w2'][l], 0, LORA_W).astype(BF16),
        a2=_pad_rows(w['rwkv_a2'][l], W_LORA, LORA_W).astype(BF16),
        g2=_pad_rows(w['rwkv_g2'][l], W_LORA + A_LORA, LORA_W).astype(BF16),
        k_k=w['rwkv_k_k'][l].reshape(1, W), k_a=w['rwkv_k_a'][l].reshape(1, W),
        r_k=w['rwkv_r_k'][l].reshape(1, W), ln_w=w['rwkv_ln_w'][l].reshape(1, W), ln_b=w['rwkv_ln_b'][l].reshape(1, W))
    p['glu_w'] = w['s5_glu_w'][l].astype(BF16)
    p['branch_w'] = w['branch_w'][l].astype(BF16)
    p['w_out'] = w['w_out'][l].astype(BF16)
    fpad = D_FF_PAD - D_FF
    fw = w['ffn_w_in'][l]
    p['ffn_w_in'] = jnp.concatenate([jnp.pad(fw[:, :D_FF], ((0, 0), (0, fpad))),
                                     jnp.pad(fw[:, D_FF:], ((0, 0), (0, fpad)))], axis=1).astype(BF16)
    p['ffn_conv_w'] = jnp.pad(w['ffn_conv_w'][l], ((0, 0), (0, fpad)))
    p['ffn_conv_b'] = jnp.pad(w['ffn_conv_b'][l], (0, fpad))
    p['ffn_w_out'] = jnp.pad(w['ffn_w_out'][l], ((0, fpad), (0, 0))).astype(BF16)
    return p


def _to_tm(a, B, L):
    return a.reshape(B, L, -1).transpose(1, 0, 2).reshape(L * B, -1)


def _to_bm(a, B, L):
    return a.reshape(L, B, -1).transpose(1, 0, 2).reshape(B * L, -1)


def layer_step(x, mod, p, w, l, B, L, ctx, rope_tabs, bm):
    W = BRANCH_W
    D = D_MODEL
    lat = ctx is not None
    nb = mod.shape[0]
    sh1, sc1, g1, sh2, sc2, g2 = (mod[:, i * D:(i + 1) * D].reshape(nb, 1, D) for i in range(6))
    P = norm_mod_matmul(x, w['norm1'][l], sc1, sh1, p['w_in'], L, bm, 1024)

    ax = P[:, COL_AX:COL_AX + W].reshape(B, L, W).transpose(1, 0, 2)
    ax = jnp.pad(ax, ((LRU_PAD_L, LRU_CONV_W - 1 - LRU_PAD_L), (0, 0), (0, 0))).reshape((L + 3) * B, W)
    lru0 = ctx['lru'].transpose(1, 0, 2) if lat else jnp.zeros((2, B, W), F32)
    hs_tm, lru_fin = lru_mixer(ax, w['lru_conv_w'][l], w['lru_conv_b'][l], p['lru_wa'], w['lru_ba'][l],
                               p['lru_wx'], w['lru_bx'][l], p['lru_sp'], lru0, B, L)
    hs = _to_bm(hs_tm, B, L)

    gs = S5_NGROUP // S5_SLICES
    if lat:
        s50 = ctx['s5'].reshape(B, 2, 2, S5_SLICES, gs * S5_STATE).transpose(3, 1, 0, 2, 4).reshape(
            S5_SLICES, 2, B, 2 * S5_HALF)
    else:
        s50 = jnp.zeros((S5_SLICES, 2, B, 2 * S5_HALF), F32)
    wb, ar, ai, wc = p['s5']
    ys_tm, s5_fin = s5_mixer(_to_tm(P[:, COL_BU:COL_BU + W], B, L), wb, ar, ai, wc, w['s5_d'][l], s50, B, L)
    ysum = _to_bm(ys_tm, B, L)

    sink = w['attn_sink'][l]
    if lat:
        yc = lat_attention(P, ctx['k_all'], ctx['v_all'], l, sink, rope_tabs, B, L)
    else:
        yc = ctx_attention(P, sink, B, L)

    HP = W // LANES
    if lat:
        s = ctx['wkv'].reshape(B, 2, HP, 2, RWKV_HEAD, RWKV_HEAD)
        eye = jnp.eye(2, dtype=F32)
        t0 = jnp.einsum('bdphvk,hg->bpdhkgv', s, eye).reshape(B * HP, 2, LANES, LANES)
    else:
        t0 = jnp.zeros((B * HP, 2, LANES, LANES), F32)
    yd, tfin = rwkv_mixer(P, p['rwkv'], t0, B, L)

    x = merge_out(P, hs, ysum, yc, yd, x, g1, p['glu_w'], w['s5_glu_b'][l], p['branch_w'], p['w_out'], L, 256)
    U = norm_mod_matmul(x, w['norm2'][l], sc2, sh2, p['ffn_w_in'], L, bm, 512)
    x = ffn_out(U, x, g2, p['ffn_conv_w'], p['ffn_conv_b'], p['ffn_w_out'], L, bm, 512)

    new = None
    if not lat:
        k_new = P[:, COL_CK:COL_CK + KV_W].reshape(B, L, N_KV_HEADS, HEAD_DIM)
        v_new = P[:, COL_CV:COL_CV + KV_W].reshape(B, L, N_KV_HEADS, HEAD_DIM)
        lru_new = lru_fin.transpose(1, 0, 2)
        s5_new = s5_fin.reshape(S5_SLICES, 2, B, 2, gs, S5_STATE).transpose(2, 1, 3, 0, 4, 5).reshape(
            B, 2, 2, S5_NGROUP, S5_STATE)
        tf = tfin.reshape(B, HP, 2, 2, RWKV_HEAD, 2, RWKV_HEAD)
        wkv_new = jnp.einsum('bpdhkhv->bdphvk', tf).reshape(B, 2, RWKV_NH, RWKV_HEAD, RWKV_HEAD)
        new = (k_new, v_new, lru_new, s5_new, wkv_new)
    return x, new


def kernel(x_prompt, x_sample, cache_k, cache_v, state_lru, state_s5, state_wkv, c, c_ctx, mod_w, mod_b, norm1, norm2, norm_final, w_in, lru_conv_w, lru_conv_b, lru_wa, lru_ba, lru_wx, lru_bx, lru_lam, s5_a_re, s5_a_im, s5_log_dt, s5_b_re, s5_b_im, s5_c_re, s5_c_im, s5_d, s5_glu_w, s5_glu_b, attn_sink, rwkv_mu, rwkv_w0, rwkv_w2, rwkv_a0, rwkv_a2, rwkv_g2, rwkv_k_k, rwkv_k_a, rwkv_r_k, rwkv_ln_w, rwkv_ln_b, branch_w, w_out, ffn_w_in, ffn_conv_w, ffn_conv_b, ffn_w_out):
    w = dict(norm1=norm1, norm2=norm2, w_in=w_in,
             lru_conv_w=lru_conv_w, lru_conv_b=lru_conv_b, lru_wa=lru_wa, lru_ba=lru_ba,
             lru_wx=lru_wx, lru_bx=lru_bx, lru_lam=lru_lam,
             s5_a_re=s5_a_re, s5_a_im=s5_a_im, s5_log_dt=s5_log_dt, s5_b_re=s5_b_re, s5_b_im=s5_b_im,
             s5_c_re=s5_c_re, s5_c_im=s5_c_im, s5_d=s5_d, s5_glu_w=s5_glu_w, s5_glu_b=s5_glu_b,
             attn_sink=attn_sink,
             rwkv_mu=rwkv_mu, rwkv_w0=rwkv_w0, rwkv_w2=rwkv_w2, rwkv_a0=rwkv_a0, rwkv_a2=rwkv_a2,
             rwkv_g2=rwkv_g2, rwkv_k_k=rwkv_k_k, rwkv_k_a=rwkv_k_a, rwkv_r_k=rwkv_r_k,
             rwkv_ln_w=rwkv_ln_w, rwkv_ln_b=rwkv_ln_b,
             branch_w=branch_w, w_out=w_out, ffn_w_in=ffn_w_in, ffn_conv_w=ffn_conv_w,
             ffn_conv_b=ffn_conv_b, ffn_w_out=ffn_w_out)
    Bc, Lc, D = x_prompt.shape
    Bl, Ll, _ = x_sample.shape
    nrow = 16
    c_all = jnp.concatenate([c_ctx[None, :], c, jnp.zeros((nrow - 1 - Bl, D), F32)], axis=0)
    mods = modulation(c_all, mod_w, mod_b)
    layers = [prepare_layer(l, w) for l in range(DEPTH)]
    rope_tabs = rope_tables(Ll)

    xp = x_prompt.reshape(Bc * Lc, D)
    news = []
    for l in range(DEPTH):
        xp, new = layer_step(xp, mods[l, 0:1], layers[l], w, l, Bc, Lc, None, None, 1024)
        news.append(new)
    y_prompt = final_norm(xp, norm_final, 512).reshape(Bc, Lc, D)

    xs = x_sample.reshape(Bl * Ll, D)
    for l in range(DEPTH):
        ctx = dict(k_all=cache_k, v_all=cache_v, lru=state_lru[:, l], s5=state_s5[:, l], wkv=state_wkv[:, l])
        xs, _ = layer_step(xs, mods[l, 1:1 + Bl], layers[l], w, l, Bl, Ll, ctx, rope_tabs, min(1024, Ll))
    y_sample = final_norm(xs, norm_final, 512).reshape(Bl, Ll, D)

    stack = lambda i: jnp.stack([n[i] for n in news], axis=1)
    return (y_prompt, y_sample, stack(0), stack(1), stack(2), stack(3), stack(4))
```

```python
import functools
import math

import jax
import jax.numpy as jnp
from jax import lax
from jax.experimental import pallas as pl
from jax.experimental.pallas import tpu as pltpu

F32 = jnp.float32
BF16 = jnp.bfloat16
HIGHEST = lax.Precision.HIGHEST

D_MODEL = 2048
DEPTH = 2
PAST_LEN = 256
GRID_W = 64
N_BRANCH = 4
BRANCH_W = 512
EPS = 1e-6
LRU_BLOCKS = 8
LRU_BS = 64
LRU_CONV_W = 4
LRU_PAD_L = 2
LRU_C = 8.0
S5_GROUP = 16
S5_NGROUP = 32
S5_STATE = 64
HEAD_DIM = 64
N_Q_HEADS = 8
N_KV_HEADS = 2
Q_PER_KV = 4
KV_W = 128
WINDOW = 128
BLOCK = 128
ROPE_BASE = 10000.0
ROPE_NF = 16
ATTN_SCALE = 1.0 / math.sqrt(HEAD_DIM)
NEG_INF = -1e30
RWKV_HEAD = 64
RWKV_NH = 8
W_LORA = 64
A_LORA = 64
G_LORA = 128
RWKV_GN_EPS = 64e-5
D_FF = 5504
D_FF_PAD = 5632
LANES = 128
SUBLANES = 8
VMEM_LIMIT = 56 * 2 ** 20

COL_AX, COL_AGATE, COL_BU, COL_CQ = 0, 512, 1024, 1536
COL_CK, COL_CV = 2048, 2176
COL_DR, COL_DK, COL_DV, COL_LORA = 2304, 2816, 3328, 3840
COL_MERGE = 4096
D_IN = COL_MERGE + N_BRANCH * D_MODEL
LORA_W = W_LORA + A_LORA + G_LORA


def _cparams(n_grid):
    return pltpu.CompilerParams(dimension_semantics=("arbitrary",) * n_grid, vmem_limit_bytes=VMEM_LIMIT)


def _bdot(a, b):
    return jnp.dot(a.astype(BF16), b.astype(BF16), preferred_element_type=F32)


def _bdot_nt(a, b):
    return lax.dot_general(a.astype(BF16), b.astype(BF16), (((1,), (1,)), ((), ())),
                           preferred_element_type=F32)


def _gelu(x):
    return jax.nn.gelu(x)


def _mod_kernel(c_ref, w_ref, b_ref, o_ref):
    c = c_ref[...]
    s = c * jax.nn.sigmoid(c)
    o_ref[0] = jnp.dot(s, w_ref[0], preferred_element_type=F32, precision=HIGHEST) + b_ref[0]


def modulation(c_all, mod_w, mod_b):
    R, D = c_all.shape
    N = mod_w.shape[2]
    bn = 1024
    return pl.pallas_call(
        _mod_kernel,
        grid=(DEPTH, N // bn),
        in_specs=[pl.BlockSpec((R, D), lambda l, j: (0, 0)),
                  pl.BlockSpec((1, D, bn), lambda l, j: (l, 0, j)),
                  pl.BlockSpec((1, 1, bn), lambda l, j: (l, 0, j))],
        out_specs=pl.BlockSpec((1, R, bn), lambda l, j: (l, 0, j)),
        out_shape=jax.ShapeDtypeStruct((DEPTH, R, N), F32),
        compiler_params=_cparams(2),
        name="modulation",
    )(c_all, mod_w, mod_b.reshape(DEPTH, 1, N))


def _nmm_kernel(x_ref, g_ref, sc_ref, sh_ref, w_ref, o_ref, h_scr):
    @pl.when(pl.program_id(1) == 0)
    def _():
        x = x_ref[...]
        y = x * lax.rsqrt(jnp.mean(x * x, axis=-1, keepdims=True) + EPS) * g_ref[...]
        h_scr[...] = (y * (1.0 + sc_ref[0]) + sh_ref[0]).astype(BF16)

    o_ref[...] = jnp.dot(h_scr[...], w_ref[0].astype(BF16), preferred_element_type=F32)


def norm_mod_matmul(x, g, sc, sh, w, l, L, bm, bn):
    M, D = x.shape
    N = w.shape[2]
    nb = sc.shape[0]
    if nb > 1:
        assert L % bm == 0
        per = L // bm
        bidx = lambda i, j: (i // per, 0, 0)
    else:
        bidx = lambda i, j: (0, 0, 0)
    return pl.pallas_call(
        _nmm_kernel,
        grid=(M // bm, N // bn),
        in_specs=[pl.BlockSpec((bm, D), lambda i, j: (i, 0)),
                  pl.BlockSpec((1, D), lambda i, j: (0, 0)),
                  pl.BlockSpec((1, 1, D), bidx),
                  pl.BlockSpec((1, 1, D), bidx),
                  pl.BlockSpec((1, D, bn), lambda i, j: (l, 0, j))],
        out_specs=pl.BlockSpec((bm, bn), lambda i, j: (i, j)),
        out_shape=jax.ShapeDtypeStruct((M, N), F32),
        scratch_shapes=[pltpu.VMEM((bm, D), BF16)],
        compiler_params=_cparams(2),
        name="norm_mod_matmul",
    )(x, g.reshape(1, D), sc, sh, w)


def _lru_kernel(xp_ref, cw_ref, cb_ref, wa_ref, ba_ref, wx_ref, bx_ref, sp_ref, h0_ref,
                out_ref, fin_ref, a_scr, b_scr, *, B, L, Tc):
    nch = L // Tc
    R = Tc * B
    for d in range(2):
        sp = sp_ref[d]

        def chunk(ci, h, d=d, sp=sp):
            c = ci if d == 0 else nch - 1 - ci
            base = pl.multiple_of(c * R, R)
            xa = cb_ref[...] + xp_ref[pl.ds(base, R), :] * cw_ref[0:1, :]
            for j in range(1, LRU_CONV_W):
                xa = xa + xp_ref[pl.ds(base + j * B, R), :] * cw_ref[j:j + 1, :]
            r = jax.nn.sigmoid(_bdot(xa, wa_ref[d]) + ba_ref[d])
            i = jax.nn.sigmoid(_bdot(xa, wx_ref[d]) + bx_ref[d])
            log_a = -LRU_C * r * sp
            a = jnp.exp(log_a)
            a_scr[...] = a
            b_scr[...] = jnp.sqrt(1.0 - a * a) * (i * xa)

            def step(ti, h):
                t = ti if d == 0 else Tc - 1 - ti
                off = pl.multiple_of(t * B, B)
                h = a_scr[pl.ds(off, B), :] * h + b_scr[pl.ds(off, B), :]
                row = pl.ds(base + off, B)
                if d == 0:
                    out_ref[row, :] = h
                else:
                    out_ref[row, :] = out_ref[row, :] + h
                return h

            return lax.fori_loop(0, Tc, step, h, unroll=8)

        fin_ref[d] = lax.fori_loop(0, nch, chunk, h0_ref[d])


def lru_mixer(xp_tm, cw, cb, wa, ba, wx, bx, sp, h0, B, L):
    W = BRANCH_W
    Tc = 32
    kern = functools.partial(_lru_kernel, B=B, L=L, Tc=Tc)
    vec = lambda: pl.BlockSpec((2, 1, LANES), lambda s: (0, 0, s))
    return pl.pallas_call(
        kern,
        grid=(W // LANES,),
        in_specs=[pl.BlockSpec(((L + 3) * B, LANES), lambda s: (0, s)),
                  pl.BlockSpec((LRU_CONV_W, LANES), lambda s: (0, s)),
                  pl.BlockSpec((1, LANES), lambda s: (0, s)),
                  pl.BlockSpec((2, LANES, LANES), lambda s: (0, s, s)), vec(),
                  pl.BlockSpec((2, LANES, LANES), lambda s: (0, s, s)), vec(),
                  vec(),
                  pl.BlockSpec((2, B, LANES), lambda s: (0, 0, s))],
        out_specs=[pl.BlockSpec((L * B, LANES), lambda s: (0, s)),
                   pl.BlockSpec((2, B, LANES), lambda s: (0, 0, s))],
        out_shape=[jax.ShapeDtypeStruct((L * B, W), F32), jax.ShapeDtypeStruct((2, B, W), F32)],
        scratch_shapes=[pltpu.VMEM((Tc * B, LANES), F32), pltpu.VMEM((Tc * B, LANES), F32)],
        compiler_params=_cparams(1),
        name="lru_mixer",
    )(xp_tm, cw, cb.reshape(1, W), wa, ba.reshape(2, 1, W), wx, bx.reshape(2, 1, W), sp.reshape(2, 1, W), h0)


S5_SLICES = BRANCH_W // LANES
S5_HALF = (S5_NGROUP // S5_SLICES) * S5_STATE


def _s5_kernel(u_ref, wb_ref, ar_ref, ai_ref, wc_ref, dv_ref, h0_ref, out_ref, fin_ref, hs_scr, *, B, L, Tc):
    nch = L // Tc
    R = Tc * B
    H = S5_HALF
    out_ref[...] = u_ref[...] * dv_ref[...]
    for d in range(2):
        ar = ar_ref[0, d]
        ai = ai_ref[0, d]

        def chunk(ci, carry, d=d, ar=ar, ai=ai):
            c = ci if d == 0 else nch - 1 - ci
            base = pl.multiple_of(c * R, R)
            hs_scr[...] = _bdot(u_ref[pl.ds(base, R), :], wb_ref[0, d])

            def step(ti, carry):
                hr, hi = carry
                t = ti if d == 0 else Tc - 1 - ti
                row = pl.ds(pl.multiple_of(t * B, B), B)
                nr = ar * hr - ai * hi + hs_scr[row, 0:H]
                ni = ar * hi + ai * hr + hs_scr[row, H:2 * H]
                hs_scr[row, 0:H] = nr
                hs_scr[row, H:2 * H] = ni
                return nr, ni

            carry = lax.fori_loop(0, Tc, step, carry, unroll=4)
            rows = pl.ds(base, R)
            out_ref[rows, :] = out_ref[rows, :] + _bdot(hs_scr[...], wc_ref[0, d])
            return carry

        hr, hi = lax.fori_loop(0, nch, chunk, (h0_ref[0, d, :, 0:H], h0_ref[0, d, :, H:2 * H]))
        fin_ref[0, d, :, 0:H] = hr
        fin_ref[0, d, :, H:2 * H] = hi


def s5_mixer(u_tm, wb, ar, ai, wc, dvec, h0, B, L):
    W = BRANCH_W
    Tc = 32
    H2 = 2 * S5_HALF
    kern = functools.partial(_s5_kernel, B=B, L=L, Tc=Tc)
    return pl.pallas_call(
        kern,
        grid=(S5_SLICES,),
        in_specs=[pl.BlockSpec((L * B, LANES), lambda s: (0, s)),
                  pl.BlockSpec((1, 2, LANES, H2), lambda s: (s, 0, 0, 0)),
                  pl.BlockSpec((1, 2, 1, S5_HALF), lambda s: (s, 0, 0, 0)),
                  pl.BlockSpec((1, 2, 1, S5_HALF), lambda s: (s, 0, 0, 0)),
                  pl.BlockSpec((1, 2, H2, LANES), lambda s: (s, 0, 0, 0)),
                  pl.BlockSpec((1, LANES), lambda s: (0, s)),
                  pl.BlockSpec((1, 2, B, H2), lambda s: (s, 0, 0, 0))],
        out_specs=[pl.BlockSpec((L * B, LANES), lambda s: (0, s)),
                   pl.BlockSpec((1, 2, B, H2), lambda s: (s, 0, 0, 0))],
        out_shape=[jax.ShapeDtypeStruct((L * B, W), F32), jax.ShapeDtypeStruct((S5_SLICES, 2, B, H2), F32)],
        scratch_shapes=[pltpu.VMEM((Tc * B, H2), F32)],
        compiler_params=_cparams(1),
        name="s5_mixer",
    )(u_tm, wb, ar, ai, wc, dvec.reshape(1, W), h0)


def s5_params(a_re, a_im, log_dt, b_re, b_im, c_re, c_im):
    dt = jnp.exp(log_dt)[..., None]
    mag = jnp.exp(a_re * dt)
    abar_r = mag * jnp.cos(a_im * dt)
    abar_i = mag * jnp.sin(a_im * dt)
    den = a_re * a_re + a_im * a_im
    nr = abar_r - 1.0
    fr = (nr * a_re + abar_i * a_im) / den
    fi = (abar_i * a_re - nr * a_im) / den
    bbar_r = fr[..., None] * b_re - fi[..., None] * b_im
    bbar_i = fr[..., None] * b_im + fi[..., None] * b_re
    gs = S5_NGROUP // S5_SLICES
    eye = jnp.eye(gs, dtype=F32)
    bb = jnp.stack([bbar_r, bbar_i], axis=0).reshape(2, 2, S5_SLICES, gs, S5_STATE, S5_GROUP)
    wb = jnp.einsum('rdsgnp,gh->sdgprhn', bb, eye).reshape(S5_SLICES, 2, LANES, 2 * S5_HALF)
    cc = jnp.stack([c_re, -c_im], axis=0).reshape(2, 2, S5_SLICES, gs, S5_GROUP, S5_STATE)
    wc = jnp.einsum('rdsgpn,gh->sdrhngp', cc, eye).reshape(S5_SLICES, 2, 2 * S5_HALF, LANES)
    ar = abar_r.reshape(2, S5_SLICES, 1, S5_HALF).transpose(1, 0, 2, 3)
    ai = abar_i.reshape(2, S5_SLICES, 1, S5_HALF).transpose(1, 0, 2, 3)
    return wb.astype(BF16), ar, ai, wc.astype(BF16)


def _rope(x, cos, sina, sinb):
    return x * cos + pltpu.roll(x, ROPE_NF, 1) * sina + pltpu.roll(x, LANES - ROPE_NF, 1) * sinb


def _attend(qg, segs, sink_col):
    scores = []
    m = sink_col
    for k, _, mask in segs:
        s = _bdot_nt(qg, k) * ATTN_SCALE
        if mask is not None:
            s = jnp.where(mask, s, NEG_INF)
        scores.append(s)
        m = jnp.maximum(m, jnp.max(s, axis=-1, keepdims=True))
    den = jnp.exp(sink_col - m)
    o = None
    for s, (_, v, _) in zip(scores, segs):
        p = jnp.exp(s - m)
        den = den + jnp.sum(p, axis=-1, keepdims=True)
        pv = _bdot(p, v)
        o = pv if o is None else o + pv
    return o / den


def _sink_col(sink_ref, kvh):
    return jnp.concatenate([jnp.full((BLOCK, 1), sink_ref[kvh * Q_PER_KV + g], F32) for g in range(Q_PER_KV)], axis=0)


def _stack_heads(q, kvh):
    return jnp.concatenate([q[:, (kvh * Q_PER_KV + g) * HEAD_DIM:(kvh * Q_PER_KV + g + 1) * HEAD_DIM]
                            for g in range(Q_PER_KV)], axis=0)


def _store_heads(o_ref, og, kvh):
    for g in range(Q_PER_KV):
        h = kvh * Q_PER_KV + g
        o_ref[:, h * HEAD_DIM:(h + 1) * HEAD_DIM] = og[g * BLOCK:(g + 1) * BLOCK]


def _ctx_attn_kernel(sink_ref, q_ref, k_ref, v_ref, o_ref):
    q = q_ref[...]
    k = k_ref[...]
    v = v_ref[...]
    for kvh in range(N_KV_HEADS):
        ks = k[:, kvh * HEAD_DIM:(kvh + 1) * HEAD_DIM]
        vs = v[:, kvh * HEAD_DIM:(kvh + 1) * HEAD_DIM]
        og = _attend(_stack_heads(q, kvh), [(ks, vs, None)], _sink_col(sink_ref, kvh))
        _store_heads(o_ref, og, kvh)


def ctx_attention(P, sink, B, L):
    nb = L // BLOCK
    return pl.pallas_call(
        _ctx_attn_kernel,
        grid=(B, nb),
        in_specs=[pl.BlockSpec(memory_space=pltpu.SMEM),
                  pl.BlockSpec((BLOCK, BRANCH_W), lambda b, j: (b * nb + j, COL_CQ // BRANCH_W)),
                  pl.BlockSpec((L, KV_W), lambda b, j: (b, COL_CK // KV_W)),
                  pl.BlockSpec((L, KV_W), lambda b, j: (b, COL_CV // KV_W))],
        out_specs=pl.BlockSpec((BLOCK, BRANCH_W), lambda b, j: (b * nb + j, 0)),
        out_shape=jax.ShapeDtypeStruct((B * L, BRANCH_W), F32),
        compiler_params=_cparams(2),
        name="ctx_attention",
    )(sink.reshape(N_Q_HEADS), P, P, P)


def _lat_attn_kernel(sink_ref, q_ref, kp_ref, kc_ref, kn_ref, vp_ref, vc_ref, vn_ref, ck_ref, cv_ref,
                     cq_ref, saq_ref, sbq_ref, cp_ref, sap_ref, sbp_ref, cn_ref, san_ref, sbn_ref, o_ref, *, nb):
    j = pl.program_id(1)
    cq, saq, sbq = cq_ref[...], saq_ref[...], sbq_ref[...]
    q = jnp.concatenate([_rope(q_ref[:, c * LANES:(c + 1) * LANES], cq, saq, sbq)
                         for c in range(BRANCH_W // LANES)], axis=1)
    kp = _rope(kp_ref[...], cp_ref[...], sap_ref[...], sbp_ref[...])
    kc = _rope(kc_ref[...], cq, saq, sbq)
    kn = _rope(kn_ref[...], cn_ref[...], san_ref[...], sbn_ref[...])
    R = Q_PER_KV * BLOCK
    r = lax.broadcasted_iota(jnp.int32, (R, BLOCK), 0) % BLOCK
    c = lax.broadcasted_iota(jnp.int32, (R, BLOCK), 1)
    mask_p = (c >= r) & (j > 0)
    mask_n = (c <= r) & (j < nb - 1)
    ck = ck_ref[0, 0]
    cv = cv_ref[0, 0]
    for kvh in range(N_KV_HEADS):
        sl = slice(kvh * HEAD_DIM, (kvh + 1) * HEAD_DIM)
        segs = [(ck[:, sl], cv[:, sl], None),
                (kp[:, sl], vp_ref[:, sl], mask_p),
                (kc[:, sl], vc_ref[:, sl], None),
                (kn[:, sl], vn_ref[:, sl], mask_n)]
        og = _attend(_stack_heads(q, kvh), segs, _sink_col(sink_ref, kvh))
        _store_heads(o_ref, og, kvh)


def lat_attention(P, cache_k, cache_v, layer, sink, rope_tabs, B, L):
    nb = L // BLOCK
    kcol, vcol = COL_CK // KV_W, COL_CV // KV_W
    prev = lambda b, j: b * nb + jnp.maximum(j - 1, 0)
    nxt = lambda b, j: b * nb + jnp.minimum(j + 1, nb - 1)
    kv = lambda rowf, col: pl.BlockSpec((BLOCK, KV_W), lambda b, j: (rowf(b, j), col))
    cur = lambda b, j: b * nb + j
    tab = lambda f: pl.BlockSpec((BLOCK, LANES), f)
    tq = lambda b, j: (j, 0)
    tp = lambda b, j: (jnp.maximum(j - 1, 0), 0)
    tn = lambda b, j: (jnp.minimum(j + 1, nb - 1), 0)
    cos, sina, sinb = rope_tabs
    cache = pl.BlockSpec((1, 1, PAST_LEN, KV_W), lambda b, j: (b, layer, 0, 0))
    return pl.pallas_call(
        functools.partial(_lat_attn_kernel, nb=nb),
        grid=(B, nb),
        in_specs=[pl.BlockSpec(memory_space=pltpu.SMEM),
                  pl.BlockSpec((BLOCK, BRANCH_W), lambda b, j: (b * nb + j, COL_CQ // BRANCH_W)),
                  kv(prev, kcol), kv(cur, kcol), kv(nxt, kcol),
                  kv(prev, vcol), kv(cur, vcol), kv(nxt, vcol),
                  cache, cache,
                  tab(tq), tab(tq), tab(tq), tab(tp), tab(tp), tab(tp), tab(tn), tab(tn), tab(tn)],
        out_specs=pl.BlockSpec((BLOCK, BRANCH_W), lambda b, j: (b * nb + j, 0)),
        out_shape=jax.ShapeDtypeStruct((B * L, BRANCH_W), F32),
        compiler_params=_cparams(2),
        name="lat_attention",
    )(sink.reshape(N_Q_HEADS), P, P, P, P, P, P, P,
      cache_k.reshape(cache_k.shape[0], DEPTH, PAST_LEN, KV_W), cache_v.reshape(cache_v.shape[0], DEPTH, PAST_LEN, KV_W),
      cos, sina, sinb, cos, sina, sinb, cos, sina, sinb)


def rope_tables(L):
    t = jnp.arange(L)
    inv = ROPE_BASE ** (-jnp.arange(ROPE_NF, dtype=F32) / ROPE_NF)
    ang_r = (t // GRID_W).astype(F32)[:, None] * inv
    ang_c = (t % GRID_W).astype(F32)[:, None] * inv
    z = jnp.zeros((L, ROPE_NF), F32)
    cos = jnp.concatenate([jnp.cos(ang_r)] * 2 + [jnp.cos(ang_c)] * 2, axis=1)
    sina = jnp.concatenate([z, jnp.sin(ang_r), z, jnp.sin(ang_c)], axis=1)
    sinb = jnp.concatenate([-jnp.sin(ang_r), z, -jnp.sin(ang_c), z], axis=1)
    return tuple(jnp.tile(a, (1, LANES // HEAD_DIM)) for a in (cos, sina, sinb))


RWKV_CHUNK = 64


def _split(a):
    hi = a.astype(BF16)
    return hi, (a - hi.astype(F32)).astype(BF16)


def _bdot3(a, b):
    ah, al = _split(a)
    bh, bl = _split(b)
    return jnp.dot(jnp.concatenate([ah, al, ah], axis=1), jnp.concatenate([bh, bh, bl], axis=0),
                   preferred_element_type=F32)


def _head_sum(x, m0, m1):
    s0 = jnp.sum(x * m0, axis=-1, keepdims=True)
    s1 = jnp.sum(x * m1, axis=-1, keepdims=True)
    return s0 * m0 + s1 * m1


def _unit_lower_inverse(ns, eye):
    xs = [eye + n for n in ns]
    pws = list(ns)
    for _ in range(5):
        pws = [_bdot(pw, pw) for pw in pws]
        xs = [x + _bdot(x, pw) for x, pw in zip(xs, pws)]
    resids = [eye - (x - _bdot3(n, x)) for n, x in zip(ns, xs)]
    return [x + _bdot(x, resid) for x, resid in zip(xs, resids)]


def _rwkv_kernel(xr_ref, xk_ref, xv_ref, lo_ref, mu_ref, w0_ref, w2_ref, a0_ref, a2_ref, g2_ref,
                 kk_ref, ka_ref, rk_ref, lnw_ref, lnb_ref, t0_ref,
                 y_ref, tfin_ref,
                 r_scr, kh_scr, v_scr, kn_scr, lw_scr, a_scr, kd_scr, yd_scr,
                 phi_scr, psi_scr, pcm_scr, rys_scr, gys_scr, *, L):
    C = RWKV_CHUNK
    nch = L // C
    lane = lax.broadcasted_iota(jnp.int32, (1, LANES), 1)
    m0 = (lane < RWKV_HEAD).astype(F32)
    m1 = 1.0 - m0
    row1 = lax.broadcasted_iota(jnp.int32, (L, 1), 0)

    def shift(x, mu):
        prev = jnp.where(row1 == 0, 0.0, pltpu.roll(x, 1, 0))
        nxt = jnp.where(row1 == L - 1, 0.0, pltpu.roll(x, L - 1, 0))
        return x + mu * (0.5 * (prev + nxt) - x)

    r = shift(xr_ref[...], mu_ref[0:1, :])
    kh = shift(xk_ref[...], mu_ref[1:2, :])
    vh = shift(xv_ref[...], mu_ref[2:3, :])
    r_scr[...] = r
    kh_scr[...] = kh
    v_scr[...] = vh
    kk = kh * kk_ref[...]
    kn_scr[...] = kk * lax.rsqrt(_head_sum(kk * kk, m0, m1) + 1e-12)
    lo = lo_ref[...]
    tw = jnp.tanh(lo)
    for d in range(2):
        z = w0_ref[d] + _bdot(tw, w2_ref[d])
        lw_scr[d] = -math.exp(-0.5) * jax.nn.sigmoid(z)
        a = jax.nn.sigmoid(a0_ref[d] + _bdot(lo, a2_ref[d]))
        a_scr[d] = a
        kd_scr[d] = kh * (1.0 + (a - 1.0) * ka_ref[...])

    ri = lax.broadcasted_iota(jnp.int32, (2 * C, 2 * C), 0)
    ci = lax.broadcasted_iota(jnp.int32, (2 * C, 2 * C), 1)
    same = (ri // C) == (ci // C)
    eye2 = (ri == ci).astype(F32)
    ti = lax.broadcasted_iota(jnp.int32, (C, C), 0)
    si = lax.broadcasted_iota(jnp.int32, (C, C), 1)
    rt, ct = ri % C, ci % C
    strict = (same & (ct < rt), same & (ct > rt))
    incl = (same & (ct <= rt), same & (ct >= rt))
    tri = ((si <= ti).astype(F32), (si >= ti).astype(F32))

    def stack(x):
        return jnp.concatenate([x * m0, x * m1], axis=0)

    def unstack(x):
        return x[0:C] + x[C:2 * C]

    def prepare(jobs):
        J = range(len(jobs))
        ds = [d for _, d in jobs]
        rows = [pl.ds(pl.multiple_of(c * C, C), C) for c, _ in jobs]
        lw = [lw_scr[ds[j], rows[j], :] for j in J]
        cum = [jnp.dot(tri[ds[j]], lw[j], preferred_element_type=F32, precision=HIGHEST) for j in J]
        p_in = [jnp.exp(cum[j]) for j in J]
        p_ex = [jnp.exp(cum[j] - lw[j]) for j in J]
        ip = [jnp.exp(-cum[j]) for j in J]
        pc = [p_in[j][C - 1:C, :] if ds[j] == 0 else p_in[j][0:1, :] for j in J]
        kn = [kn_scr[rows[j], :] for j in J]
        KAP = [stack(kn[j] * p_ex[j]) for j in J]
        RT = [stack(r_scr[rows[j], :] * p_in[j]) for j in J]
        BT = [stack(-(a_scr[ds[j], rows[j], :] * kn[j]) * ip[j]) for j in J]
        KT = [stack(kd_scr[ds[j], rows[j], :] * ip[j]) for j in J]
        V = [stack(v_scr[rows[j], :]) for j in J]
        G = [_bdot_nt(jnp.concatenate([KAP[j], RT[j]], axis=0), jnp.concatenate([BT[j], KT[j]], axis=0))
             for j in J]
        aab = [jnp.where(strict[ds[j]], G[j][0:2 * C, 0:2 * C], 0.0) for j in J]
        aak = [jnp.where(strict[ds[j]], G[j][0:2 * C, 2 * C:4 * C], 0.0) for j in J]
        grb = [jnp.where(incl[ds[j]], G[j][2 * C:4 * C, 0:2 * C], 0.0) for j in J]
        grk = [jnp.where(incl[ds[j]], G[j][2 * C:4 * C, 2 * C:4 * C], 0.0) for j in J]
        AV = [_bdot(aak[j], V[j]) for j in J]
        GV = [_bdot(grk[j], V[j]) for j in J]
        X = _unit_lower_inverse(aab, eye2)
        XK = [_bdot(X[j], KAP[j]) for j in J]
        XAV = [_bdot(X[j], AV[j]) for j in J]
        btp_t = [(BT[j] * pc[j]).T for j in J]
        ktp_t = [(KT[j] * pc[j]).T for j in J]
        phi = [_bdot(btp_t[j], XK[j]) for j in J]
        psi = [_bdot(btp_t[j], XAV[j]) + _bdot(ktp_t[j], V[j]) for j in J]
        rys = [unstack(RT[j] + _bdot(grb[j], XK[j])) for j in J]
        gys = [unstack(_bdot(grb[j], XAV[j]) + GV[j]) for j in J]
        for j, (c, d) in enumerate(jobs):
            phi_scr[d, c] = phi[j].astype(BF16)
            psi_scr[d, c] = psi[j]
            pcm_scr[d, c] = jnp.broadcast_to(pc[j], (LANES, LANES)).T
            rys_scr[d, c] = rys[j].astype(BF16)
            gys_scr[d, c] = gys[j]

    def prep_body(i, carry):
        prepare([(2 * i + u, d) for u in range(2) for d in range(2)])
        return carry

    lax.fori_loop(0, nch // 2, prep_body, 0)

    def seq_body(i, carry):
        out = []
        for d, T in enumerate(carry):
            c = i if d == 0 else nch - 1 - i
            rows = pl.ds(pl.multiple_of(c * C, C), C)
            tb = T.astype(BF16)
            yd_scr[d, rows, :] = jnp.dot(rys_scr[d, c], tb, preferred_element_type=F32) + gys_scr[d, c]
            out.append(pcm_scr[d, c] * T + jnp.dot(phi_scr[d, c], tb, preferred_element_type=F32) + psi_scr[d, c])
        return tuple(out)

    tf, tb_ = lax.fori_loop(0, nch, seq_body, (t0_ref[0, 0], t0_ref[0, 1]))
    tfin_ref[0, 0] = tf
    tfin_ref[0, 1] = tb_

    y = yd_scr[0] + yd_scr[1]
    inv_n = 1.0 / RWKV_HEAD
    mu = _head_sum(y, m0, m1) * inv_n
    yc = y - mu
    var = _head_sum(yc * yc, m0, m1) * inv_n
    yn = yc * lax.rsqrt(var + RWKV_GN_EPS) * lnw_ref[...] + lnb_ref[...]
    bonus = _head_sum(r * kh * rk_ref[...], m0, m1) * vh
    g = _bdot(jax.nn.sigmoid(lo), g2_ref[...])
    y_ref[...] = (yn + bonus) * g


def rwkv_mixer(P, prm, t0, B, L):
    HP = BRANCH_W // LANES
    nl = L
    col = lambda c0: (lambda b, h: (b, c0 // LANES + h))
    vecs = lambda n: pl.BlockSpec((n, LANES), lambda b, h: (0, h))
    dvec = pl.BlockSpec((2, 1, LANES), lambda b, h: (0, 0, h))
    lora_w = pl.BlockSpec((2, LORA_W, LANES), lambda b, h: (0, 0, h))
    seq = lambda: pltpu.VMEM((L, LANES), F32)
    seq2 = lambda: pltpu.VMEM((2, L, LANES), F32)
    nch = L // RWKV_CHUNK
    return pl.pallas_call(
        functools.partial(_rwkv_kernel, L=L),
        grid=(B, HP),
        in_specs=[pl.BlockSpec((nl, LANES), col(COL_DR)),
                  pl.BlockSpec((nl, LANES), col(COL_DK)),
                  pl.BlockSpec((nl, LANES), col(COL_DV)),
                  pl.BlockSpec((nl, LORA_W), lambda b, h: (b, COL_LORA // LORA_W)),
                  vecs(3), dvec, lora_w, dvec, lora_w,
                  pl.BlockSpec((LORA_W, LANES), lambda b, h: (0, h)),
                  vecs(1), vecs(1), vecs(1), vecs(1), vecs(1),
                  pl.BlockSpec((1, 2, LANES, LANES), lambda b, h: (b * HP + h, 0, 0, 0))],
        out_specs=[pl.BlockSpec((nl, LANES), lambda b, h: (b, h)),
                   pl.BlockSpec((1, 2, LANES, LANES), lambda b, h: (b * HP + h, 0, 0, 0))],
        out_shape=[jax.ShapeDtypeStruct((B * L, BRANCH_W), F32),
                   jax.ShapeDtypeStruct((B * HP, 2, LANES, LANES), F32)],
        scratch_shapes=[seq(), seq(), seq(), seq(), seq2(), seq2(), seq2(), seq2(),
                        pltpu.VMEM((2, nch, LANES, LANES), BF16), pltpu.VMEM((2, nch, LANES, LANES), F32),
                        pltpu.VMEM((2, nch, LANES, LANES), F32), pltpu.VMEM((2, nch, RWKV_CHUNK, LANES), BF16),
                        pltpu.VMEM((2, nch, RWKV_CHUNK, LANES), F32)],
        compiler_params=_cparams(2),
        name="rwkv_mixer",
    )(P, P, P, P, prm['mu'], prm['w0'], prm['w2'], prm['a0'], prm['a2'], prm['g2'],
      prm['k_k'], prm['k_a'], prm['r_k'], prm['ln_w'], prm['ln_b'], t0)


def _merge_kernel(ag_ref, hs_ref, ys_ref, yc_ref, yd_ref, m0_ref, m1_ref, m2_ref, m3_ref, x_ref, g1_ref,
                  gw_ref, gb_ref, bw_ref, wo_ref, o_ref):
    y_a = _gelu(ag_ref[...]) * hs_ref[...]
    y_s = _gelu(ys_ref[...])
    y_b = y_s * jax.nn.sigmoid(_bdot(y_s, gw_ref[...]) + gb_ref[...])
    acc = None
    for n, (y, m_ref) in enumerate(((y_a, m0_ref), (y_b, m1_ref), (yc_ref[...], m2_ref), (yd_ref[...], m3_ref))):
        t = jax.nn.sigmoid(m_ref[...]) * _bdot(y, bw_ref[n])
        acc = t if acc is None else acc + t
    o_ref[...] = x_ref[...] + g1_ref[0] * _bdot(acc, wo_ref[...])


def merge_out(P, hs, ysum, yc, yd, x, g1, glu_w, glu_b, branch_w, w_out, L, bm):
    M, D = x.shape
    W = BRANCH_W
    nb = g1.shape[0]
    per = max(L // bm, 1)
    bidx = (lambda i: (i // per, 0, 0)) if nb > 1 else (lambda i: (0, 0, 0))
    row = lambda w, c: pl.BlockSpec((bm, w), lambda i: (i, c))
    const = lambda shape: pl.BlockSpec(shape, lambda i: (0,) * len(shape), pipeline_mode=pl.Buffered(1))
    mcol = COL_MERGE // D
    return pl.pallas_call(
        _merge_kernel,
        grid=(M // bm,),
        in_specs=[row(W, COL_AGATE // W), row(W, 0), row(W, 0), row(W, 0), row(W, 0),
                  row(D, mcol), row(D, mcol + 1), row(D, mcol + 2), row(D, mcol + 3),
                  row(D, 0), pl.BlockSpec((1, 1, D), bidx),
                  const((W, W)), const((1, W)), const((N_BRANCH, W, D)), const((D, D))],
        out_specs=row(D, 0),
        out_shape=jax.ShapeDtypeStruct((M, D), F32),
        compiler_params=_cparams(1),
        name="merge_out",
    )(P, hs, ysum, yc, yd, P, P, P, P, x, g1, glu_w, glu_b.reshape(1, W), branch_w, w_out)


def _shift_rows(g, L, back):
    bm = g.shape[0]
    rolled = pltpu.roll(g, 1 if back else bm - 1, 0)
    r8 = lax.broadcasted_iota(jnp.int32, (SUBLANES, 1), 0)
    pieces = []
    for s in range(0, bm, L):
        if back:
            pieces += [jnp.where(r8 == 0, 0.0, rolled[s:s + SUBLANES]), rolled[s + SUBLANES:s + L]]
        else:
            e = s + L - SUBLANES
            pieces += [rolled[s:e], jnp.where(r8 == SUBLANES - 1, 0.0, rolled[e:s + L])]
    return jnp.concatenate(pieces, axis=0)


def _ffn_out_kernel(gc_ref, val_ref, cw_ref, cb_ref, w_ref, x_ref, g2_ref, o_ref, acc_scr, *, L):
    k = pl.program_id(1)

    @pl.when(k == 0)
    def _():
        acc_scr[...] = jnp.zeros_like(acc_scr)

    g = gc_ref[...]
    gate = (_shift_rows(g, L, True) * cw_ref[0:1, :] + g * cw_ref[1:2, :]
            + _shift_rows(g, L, False) * cw_ref[2:3, :] + cb_ref[...])
    half = 0.5 * gate
    act = (half + half * jnp.tanh(half)) * val_ref[...]
    acc_scr[...] += _bdot(act, w_ref[...])

    @pl.when(k == pl.num_programs(1) - 1)
    def _():
        o_ref[...] = x_ref[...] + g2_ref[0] * acc_scr[...]


def ffn_out(U, x, g2, conv_w, conv_b, w, L, bm, bk):
    M, D = x.shape
    nk = D_FF_PAD // bk
    nb = g2.shape[0]
    assert bm % L == 0 and nk >= 2 and (nb == 1 or bm == L)
    bidx = (lambda i, k: (i, 0, 0)) if nb > 1 else (lambda i, k: (0, 0, 0))
    return pl.pallas_call(
        functools.partial(_ffn_out_kernel, L=L),
        grid=(M // bm, nk),
        in_specs=[pl.BlockSpec((bm, bk), lambda i, k: (i, k)),
                  pl.BlockSpec((bm, bk), lambda i, k: (i, nk + k)),
                  pl.BlockSpec((3, bk), lambda i, k: (0, k)),
                  pl.BlockSpec((1, bk), lambda i, k: (0, k)),
                  pl.BlockSpec((bk, D), lambda i, k: (k, 0)),
                  pl.BlockSpec((bm, D), lambda i, k: (i, 0), pipeline_mode=pl.Buffered(1)),
                  pl.BlockSpec((1, 1, D), bidx)],
        out_specs=pl.BlockSpec((bm, D), lambda i, k: (i, 0)),
        out_shape=jax.ShapeDtypeStruct((M, D), F32),
        scratch_shapes=[pltpu.VMEM((bm, D), F32)],
        compiler_params=_cparams(2),
        name="ffn_out",
    )(U, U, conv_w, conv_b.reshape(1, D_FF_PAD), w, x, g2)


def _final_norm_kernel(x_ref, g_ref, o_ref):
    x = x_ref[...]
    o_ref[...] = x * lax.rsqrt(jnp.mean(x * x, axis=-1, keepdims=True) + EPS) * g_ref[...]


def final_norm(x, g, bm):
    M, D = x.shape
    return pl.pallas_call(
        _final_norm_kernel,
        grid=(M // bm,),
        in_specs=[pl.BlockSpec((bm, D), lambda i: (i, 0)), pl.BlockSpec((1, D), lambda i: (0, 0))],
        out_specs=pl.BlockSpec((bm, D), lambda i: (i, 0)),
        out_shape=jax.ShapeDtypeStruct((M, D), F32),
        compiler_params=_cparams(1),
        name="final_norm",
    )(x, g.reshape(1, D))


def _block_diag(w):
    n, bs, _ = w.shape
    return jnp.einsum('ncd,nm->ncmd', w, jnp.eye(n, dtype=w.dtype)).reshape(n * bs, n * bs)


def _pad_rows(w, start, total):
    k = w.shape[-2]
    pad = [(0, 0)] * (w.ndim - 2) + [(start, total - start - k), (0, 0)]
    return jnp.pad(w, pad)


def prepare_layer(l, w):
    W = BRANCH_W
    p = {}
    p['lru_wa'] = jnp.stack([_block_diag(w['lru_wa'][l, d]) for d in range(2)]).astype(BF16)
    p['lru_wx'] = jnp.stack([_block_diag(w['lru_wx'][l, d]) for d in range(2)]).astype(BF16)
    p['lru_sp'] = jax.nn.softplus(-w['lru_lam'][l])
    p['s5'] = s5_params(w['s5_a_re'][l], w['s5_a_im'][l], w['s5_log_dt'][l], w['s5_b_re'][l], w['s5_b_im'][l],
                        w['s5_c_re'][l], w['s5_c_im'][l])
    p['rwkv'] = dict(
        mu=w['rwkv_mu'][l], w0=w['rwkv_w0'][l].reshape(2, 1, W), a0=w['rwkv_a0'][l].reshape(2, 1, W),
        w2=_pad_rows(w['rwkv_w2'][l], 0, LORA_W).astype(BF16),
        a2=_pad_rows(w['rwkv_a2'][l], W_LORA, LORA_W).astype(BF16),
        g2=_pad_rows(w['rwkv_g2'][l], W_LORA + A_LORA, LORA_W).astype(BF16),
        k_k=w['rwkv_k_k'][l].reshape(1, W), k_a=w['rwkv_k_a'][l].reshape(1, W),
        r_k=w['rwkv_r_k'][l].reshape(1, W), ln_w=w['rwkv_ln_w'][l].reshape(1, W), ln_b=w['rwkv_ln_b'][l].reshape(1, W))
    p['glu_w'] = w['s5_glu_w'][l].astype(BF16)
    p['branch_w'] = w['branch_w'][l].astype(BF16)
    p['w_out'] = w['w_out'][l].astype(BF16)
    fpad = D_FF_PAD - D_FF
    fw = w['ffn_w_in'][l]
    p['ffn_w_in'] = jnp.concatenate([jnp.pad(fw[:, :D_FF], ((0, 0), (0, fpad))),
                                     jnp.pad(fw[:, D_FF:], ((0, 0), (0, fpad)))], axis=1).astype(BF16)
    p['ffn_conv_w'] = jnp.pad(w['ffn_conv_w'][l], ((0, 0), (0, fpad)))
    p['ffn_conv_b'] = jnp.pad(w['ffn_conv_b'][l], (0, fpad))
    p['ffn_w_out'] = jnp.pad(w['ffn_w_out'][l], ((0, fpad), (0, 0))).astype(BF16)
    return p


def _to_tm(a, B, L):
    return a.reshape(B, L, -1).transpose(1, 0, 2).reshape(L * B, -1)


def _to_bm(a, B, L):
    return a.reshape(L, B, -1).transpose(1, 0, 2).reshape(B * L, -1)


def layer_step(x, mod, p, w, l, B, L, ctx, rope_tabs, bm):
    W = BRANCH_W
    D = D_MODEL
    lat = ctx is not None
    nb = mod.shape[0]
    sh1, sc1, g1, sh2, sc2, g2 = (mod[:, i * D:(i + 1) * D].reshape(nb, 1, D) for i in range(6))
    P = norm_mod_matmul(x, w['norm1'][l], sc1, sh1, w['w_in'], l, L, bm, 1024)

    ax = P[:, COL_AX:COL_AX + W].reshape(B, L, W).transpose(1, 0, 2)
    ax = jnp.pad(ax, ((LRU_PAD_L, LRU_CONV_W - 1 - LRU_PAD_L), (0, 0), (0, 0))).reshape((L + 3) * B, W)
    lru0 = ctx['lru'].transpose(1, 0, 2) if lat else jnp.zeros((2, B, W), F32)
    hs_tm, lru_fin = lru_mixer(ax, w['lru_conv_w'][l], w['lru_conv_b'][l], p['lru_wa'], w['lru_ba'][l],
                               p['lru_wx'], w['lru_bx'][l], p['lru_sp'], lru0, B, L)
    hs = _to_bm(hs_tm, B, L)

    gs = S5_NGROUP // S5_SLICES
    if lat:
        s50 = ctx['s5'].reshape(B, 2, 2, S5_SLICES, gs * S5_STATE).transpose(3, 1, 0, 2, 4).reshape(
            S5_SLICES, 2, B, 2 * S5_HALF)
    else:
        s50 = jnp.zeros((S5_SLICES, 2, B, 2 * S5_HALF), F32)
    wb, ar, ai, wc = p['s5']
    ys_tm, s5_fin = s5_mixer(_to_tm(P[:, COL_BU:COL_BU + W], B, L), wb, ar, ai, wc, w['s5_d'][l], s50, B, L)
    ysum = _to_bm(ys_tm, B, L)

    sink = w['attn_sink'][l]
    if lat:
        yc = lat_attention(P, ctx['k_all'], ctx['v_all'], l, sink, rope_tabs, B, L)
    else:
        yc = ctx_attention(P, sink, B, L)

    HP = W // LANES
    if lat:
        s = ctx['wkv'].reshape(B, 2, HP, 2, RWKV_HEAD, RWKV_HEAD)
        eye = jnp.eye(2, dtype=F32)
        t0 = jnp.einsum('bdphvk,hg->bpdhkgv', s, eye).reshape(B * HP, 2, LANES, LANES)
    else:
        t0 = jnp.zeros((B * HP, 2, LANES, LANES), F32)
    yd, tfin = rwkv_mixer(P, p['rwkv'], t0, B, L)

    x = merge_out(P, hs, ysum, yc, yd, x, g1, p['glu_w'], w['s5_glu_b'][l], p['branch_w'], p['w_out'], L, 256)
    U = norm_mod_matmul(x, w['norm2'][l], sc2, sh2, p['ffn_w_in'][None], 0, L, bm, 512)
    x = ffn_out(U, x, g2, p['ffn_conv_w'], p['ffn_conv_b'], p['ffn_w_out'], L, bm, 512)

    new = None
    if not lat:
        k_new = P[:, COL_CK:COL_CK + KV_W].reshape(B, L, N_KV_HEADS, HEAD_DIM)
        v_new = P[:, COL_CV:COL_CV + KV_W].reshape(B, L, N_KV_HEADS, HEAD_DIM)
        lru_new = lru_fin.transpose(1, 0, 2)
        s5_new = s5_fin.reshape(S5_SLICES, 2, B, 2, gs, S5_STATE).transpose(2, 1, 3, 0, 4, 5).reshape(
            B, 2, 2, S5_NGROUP, S5_STATE)
        tf = tfin.reshape(B, HP, 2, 2, RWKV_HEAD, 2, RWKV_HEAD)
        wkv_new = jnp.einsum('bpdhkhv->bdphvk', tf).reshape(B, 2, RWKV_NH, RWKV_HEAD, RWKV_HEAD)
        new = (k_new, v_new, lru_new, s5_new, wkv_new)
    return x, new


def kernel(x_prompt, x_sample, cache_k, cache_v, state_lru, state_s5, state_wkv, c, c_ctx, mod_w, mod_b, norm1, norm2, norm_final, w_in, lru_conv_w, lru_conv_b, lru_wa, lru_ba, lru_wx, lru_bx, lru_lam, s5_a_re, s5_a_im, s5_log_dt, s5_b_re, s5_b_im, s5_c_re, s5_c_im, s5_d, s5_glu_w, s5_glu_b, attn_sink, rwkv_mu, rwkv_w0, rwkv_w2, rwkv_a0, rwkv_a2, rwkv_g2, rwkv_k_k, rwkv_k_a, rwkv_r_k, rwkv_ln_w, rwkv_ln_b, branch_w, w_out, ffn_w_in, ffn_conv_w, ffn_conv_b, ffn_w_out):
    w = dict(norm1=norm1, norm2=norm2, w_in=w_in,
             lru_conv_w=lru_conv_w, lru_conv_b=lru_conv_b, lru_wa=lru_wa, lru_ba=lru_ba,
             lru_wx=lru_wx, lru_bx=lru_bx, lru_lam=lru_lam,
             s5_a_re=s5_a_re, s5_a_im=s5_a_im, s5_log_dt=s5_log_dt, s5_b_re=s5_b_re, s5_b_im=s5_b_im,
             s5_c_re=s5_c_re, s5_c_im=s5_c_im, s5_d=s5_d, s5_glu_w=s5_glu_w, s5_glu_b=s5_glu_b,
             attn_sink=attn_sink,
             rwkv_mu=rwkv_mu, rwkv_w0=rwkv_w0, rwkv_w2=rwkv_w2, rwkv_a0=rwkv_a0, rwkv_a2=rwkv_a2,
             rwkv_g2=rwkv_g2, rwkv_k_k=rwkv_k_k, rwkv_k_a=rwkv_k_a, rwkv_r_k=rwkv_r_k,
             rwkv_ln_w=rwkv_ln_w, rwkv_ln_b=rwkv_ln_b,
             branch_w=branch_w, w_out=w_out, ffn_w_in=ffn_w_in, ffn_conv_w=ffn_conv_w,
             ffn_conv_b=ffn_conv_b, ffn_w_out=ffn_w_out)
    Bc, Lc, D = x_prompt.shape
    Bl, Ll, _ = x_sample.shape
    nrow = 16
    c_all = jnp.concatenate([c_ctx[None, :], c, jnp.zeros((nrow - 1 - Bl, D), F32)], axis=0)
    mods = modulation(c_all, mod_w, mod_b)
    layers = [prepare_layer(l, w) for l in range(DEPTH)]
    rope_tabs = rope_tables(Ll)

    xp = x_prompt.reshape(Bc * Lc, D)
    news = []
    for l in range(DEPTH):
        xp, new = layer_step(xp, mods[l, 0:1], layers[l], w, l, Bc, Lc, None, None, 1024)
        news.append(new)
    y_prompt = final_norm(xp, norm_final, 512).reshape(Bc, Lc, D)

    xs = x_sample.reshape(Bl * Ll, D)
    for l in range(DEPTH):
        ctx = dict(k_all=cache_k, v_all=cache_v, lru=state_lru[:, l], s5=state_s5[:, l], wkv=state_wkv[:, l])
        xs, _ = layer_step(xs, mods[l, 1:1 + Bl], layers[l], w, l, Bl, Ll, ctx, rope_tabs, min(1024, Ll))
    y_sample = final_norm(xs, norm_final, 512).reshape(Bl, Ll, D)

    stack = lambda i: jnp.stack([n[i] for n in news], axis=1)
    return (y_prompt, y_sample, stack(0), stack(1), stack(2), stack(3), stack(4))
```

```python
import functools
import math

import jax
import jax.numpy as jnp
from jax import lax
from jax.experimental import pallas as pl
from jax.experimental.pallas import tpu as pltpu

F32 = jnp.float32
BF16 = jnp.bfloat16
HIGHEST = lax.Precision.HIGHEST

D_MODEL = 2048
DEPTH = 2
PAST_LEN = 256
GRID_W = 64
N_BRANCH = 4
BRANCH_W = 512
EPS = 1e-6
LRU_BLOCKS = 8
LRU_BS = 64
LRU_CONV_W = 4
LRU_PAD_L = 2
LRU_C = 8.0
S5_GROUP = 16
S5_NGROUP = 32
S5_STATE = 64
HEAD_DIM = 64
N_Q_HEADS = 8
N_KV_HEADS = 2
Q_PER_KV = 4
KV_W = 128
WINDOW = 128
BLOCK = 128
ROPE_BASE = 10000.0
ROPE_NF = 16
ATTN_SCALE = 1.0 / math.sqrt(HEAD_DIM)
NEG_INF = -1e30
RWKV_HEAD = 64
RWKV_NH = 8
W_LORA = 64
A_LORA = 64
G_LORA = 128
RWKV_GN_EPS = 64e-5
D_FF = 5504
D_FF_PAD = 5632
LANES = 128
SUBLANES = 8
VMEM_LIMIT = 56 * 2 ** 20

COL_AX, COL_AGATE, COL_BU, COL_CQ = 0, 512, 1024, 1536
COL_CK, COL_CV = 2048, 2176
COL_DR, COL_DK, COL_DV, COL_LORA = 2304, 2816, 3328, 3840
COL_MERGE = 4096
D_IN = COL_MERGE + N_BRANCH * D_MODEL
LORA_W = W_LORA + A_LORA + G_LORA


def _cparams(n_grid):
    return pltpu.CompilerParams(dimension_semantics=("arbitrary",) * n_grid, vmem_limit_bytes=VMEM_LIMIT)


def _bdot(a, b):
    return jnp.dot(a.astype(BF16), b.astype(BF16), preferred_element_type=F32)


def _bdot_nt(a, b):
    return lax.dot_general(a.astype(BF16), b.astype(BF16), (((1,), (1,)), ((), ())),
                           preferred_element_type=F32)


def _gelu(x):
    return jax.nn.gelu(x)


def _sigmoid(x):
    return 0.5 * jnp.tanh(0.5 * x) + 0.5


def _mod_kernel(c_ref, w_ref, b_ref, o_ref):
    c = c_ref[...]
    s = c * _sigmoid(c)
    o_ref[0] = jnp.dot(s, w_ref[0], preferred_element_type=F32, precision=HIGHEST) + b_ref[0]


def modulation(c_all, mod_w, mod_b):
    R, D = c_all.shape
    N = mod_w.shape[2]
    bn = 1024
    return pl.pallas_call(
        _mod_kernel,
        grid=(DEPTH, N // bn),
        in_specs=[pl.BlockSpec((R, D), lambda l, j: (0, 0)),
                  pl.BlockSpec((1, D, bn), lambda l, j: (l, 0, j)),
                  pl.BlockSpec((1, 1, bn), lambda l, j: (l, 0, j))],
        out_specs=pl.BlockSpec((1, R, bn), lambda l, j: (l, 0, j)),
        out_shape=jax.ShapeDtypeStruct((DEPTH, R, N), F32),
        compiler_params=_cparams(2),
        name="modulation",
    )(c_all, mod_w, mod_b.reshape(DEPTH, 1, N))


def _nmm_kernel(x_ref, g_ref, sc_ref, sh_ref, w_ref, o_ref, h_scr):
    @pl.when(pl.program_id(1) == 0)
    def _():
        x = x_ref[...]
        y = x * lax.rsqrt(jnp.mean(x * x, axis=-1, keepdims=True) + EPS) * g_ref[...]
        h_scr[...] = (y * (1.0 + sc_ref[0]) + sh_ref[0]).astype(BF16)

    o_ref[...] = jnp.dot(h_scr[...], w_ref[0].astype(BF16), preferred_element_type=F32)


def norm_mod_matmul(x, g, sc, sh, w, l, L, bm, bn):
    M, D = x.shape
    N = w.shape[2]
    nb = sc.shape[0]
    if nb > 1:
        assert L % bm == 0
        per = L // bm
        bidx = lambda i, j: (i // per, 0, 0)
    else:
        bidx = lambda i, j: (0, 0, 0)
    return pl.pallas_call(
        _nmm_kernel,
        grid=(M // bm, N // bn),
        in_specs=[pl.BlockSpec((bm, D), lambda i, j: (i, 0)),
                  pl.BlockSpec((1, D), lambda i, j: (0, 0)),
                  pl.BlockSpec((1, 1, D), bidx),
                  pl.BlockSpec((1, 1, D), bidx),
                  pl.BlockSpec((1, D, bn), lambda i, j: (l, 0, j))],
        out_specs=pl.BlockSpec((bm, bn), lambda i, j: (i, j)),
        out_shape=jax.ShapeDtypeStruct((M, N), F32),
        scratch_shapes=[pltpu.VMEM((bm, D), BF16)],
        compiler_params=_cparams(2),
        name="norm_mod_matmul",
    )(x, g.reshape(1, D), sc, sh, w)


def _lru_kernel(xp_ref, cw_ref, cb_ref, wa_ref, ba_ref, wx_ref, bx_ref, sp_ref, h0_ref,
                out_ref, fin_ref, a_scr, b_scr, *, B, L, Tc):
    nch = L // Tc
    R = Tc * B
    out_ref[...] = jnp.zeros_like(out_ref)

    def chunk(ci, hs):
        bases = (pl.multiple_of(ci * R, R), pl.multiple_of((nch - 1 - ci) * R, R))
        for d in range(2):
            base = bases[d]
            xa = cb_ref[...] + xp_ref[pl.ds(base, R), :] * cw_ref[0:1, :]
            for j in range(1, LRU_CONV_W):
                xa = xa + xp_ref[pl.ds(base + j * B, R), :] * cw_ref[j:j + 1, :]
            r = _sigmoid(_bdot(xa, wa_ref[d]) + ba_ref[d])
            i = _sigmoid(_bdot(xa, wx_ref[d]) + bx_ref[d])
            a = jnp.exp(-LRU_C * r * sp_ref[d])
            a_scr[d] = a
            b_scr[d] = jnp.sqrt(1.0 - a * a) * (i * xa)

        def step(ti, hs):
            new = []
            for d in range(2):
                off = pl.multiple_of((ti if d == 0 else Tc - 1 - ti) * B, B)
                h = a_scr[d, pl.ds(off, B), :] * hs[d] + b_scr[d, pl.ds(off, B), :]
                row = pl.ds(bases[d] + off, B)
                out_ref[row, :] = out_ref[row, :] + h
                new.append(h)
            return tuple(new)

        return lax.fori_loop(0, Tc, step, hs, unroll=8)

    hf, hb = lax.fori_loop(0, nch, chunk, (h0_ref[0], h0_ref[1]))
    fin_ref[0] = hf
    fin_ref[1] = hb


def lru_mixer(xp_tm, cw, cb, wa, ba, wx, bx, sp, h0, B, L):
    W = BRANCH_W
    Tc = min(128, L // 2)
    kern = functools.partial(_lru_kernel, B=B, L=L, Tc=Tc)
    vec = lambda: pl.BlockSpec((2, 1, LANES), lambda s: (0, 0, s))
    return pl.pallas_call(
        kern,
        grid=(W // LANES,),
        in_specs=[pl.BlockSpec(((L + 3) * B, LANES), lambda s: (0, s)),
                  pl.BlockSpec((LRU_CONV_W, LANES), lambda s: (0, s)),
                  pl.BlockSpec((1, LANES), lambda s: (0, s)),
                  pl.BlockSpec((2, LANES, LANES), lambda s: (0, s, s)), vec(),
                  pl.BlockSpec((2, LANES, LANES), lambda s: (0, s, s)), vec(),
                  vec(),
                  pl.BlockSpec((2, B, LANES), lambda s: (0, 0, s))],
        out_specs=[pl.BlockSpec((L * B, LANES), lambda s: (0, s)),
                   pl.BlockSpec((2, B, LANES), lambda s: (0, 0, s))],
        out_shape=[jax.ShapeDtypeStruct((L * B, W), F32), jax.ShapeDtypeStruct((2, B, W), F32)],
        scratch_shapes=[pltpu.VMEM((2, Tc * B, LANES), F32), pltpu.VMEM((2, Tc * B, LANES), F32)],
        compiler_params=_cparams(1),
        name="lru_mixer",
    )(xp_tm, cw, cb.reshape(1, W), wa, ba.reshape(2, 1, W), wx, bx.reshape(2, 1, W), sp.reshape(2, 1, W), h0)


S5_SLICES = BRANCH_W // LANES
S5_HALF = (S5_NGROUP // S5_SLICES) * S5_STATE


def _s5_kernel(u_ref, wb_ref, ar_ref, ai_ref, wc_ref, dv_ref, h0_ref, out_ref, fin_ref, hs_scr, *, B, L, Tc):
    nch = L // Tc
    R = Tc * B
    H = S5_HALF
    out_ref[...] = u_ref[...] * dv_ref[...]
    ab = [(ar_ref[0, d], ai_ref[0, d]) for d in range(2)]

    def chunk(ci, carry):
        bases = (pl.multiple_of(ci * R, R), pl.multiple_of((nch - 1 - ci) * R, R))
        for d in range(2):
            hs_scr[d] = _bdot(u_ref[pl.ds(bases[d], R), :], wb_ref[0, d])

        def step(ti, carry):
            new = []
            for d in range(2):
                hr, hi = carry[d]
                ar, ai = ab[d]
                row = pl.ds(pl.multiple_of((ti if d == 0 else Tc - 1 - ti) * B, B), B)
                nr = ar * hr - ai * hi + hs_scr[d, row, 0:H]
                ni = ar * hi + ai * hr + hs_scr[d, row, H:2 * H]
                hs_scr[d, row, 0:H] = nr
                hs_scr[d, row, H:2 * H] = ni
                new.append((nr, ni))
            return tuple(new)

        carry = lax.fori_loop(0, Tc, step, carry, unroll=2)
        for d in range(2):
            rows = pl.ds(bases[d], R)
            out_ref[rows, :] = out_ref[rows, :] + _bdot(hs_scr[d], wc_ref[0, d])
        return carry

    init = tuple((h0_ref[0, d, :, 0:H], h0_ref[0, d, :, H:2 * H]) for d in range(2))
    fin = lax.fori_loop(0, nch, chunk, init)
    for d in range(2):
        fin_ref[0, d, :, 0:H] = fin[d][0]
        fin_ref[0, d, :, H:2 * H] = fin[d][1]


def s5_mixer(u_tm, wb, ar, ai, wc, dvec, h0, B, L):
    W = BRANCH_W
    Tc = min(128, L // 2, 1024 // B)
    H2 = 2 * S5_HALF
    kern = functools.partial(_s5_kernel, B=B, L=L, Tc=Tc)
    return pl.pallas_call(
        kern,
        grid=(S5_SLICES,),
        in_specs=[pl.BlockSpec((L * B, LANES), lambda s: (0, s)),
                  pl.BlockSpec((1, 2, LANES, H2), lambda s: (s, 0, 0, 0)),
                  pl.BlockSpec((1, 2, 1, S5_HALF), lambda s: (s, 0, 0, 0)),
                  pl.BlockSpec((1, 2, 1, S5_HALF), lambda s: (s, 0, 0, 0)),
                  pl.BlockSpec((1, 2, H2, LANES), lambda s: (s, 0, 0, 0)),
                  pl.BlockSpec((1, LANES), lambda s: (0, s)),
                  pl.BlockSpec((1, 2, B, H2), lambda s: (s, 0, 0, 0))],
        out_specs=[pl.BlockSpec((L * B, LANES), lambda s: (0, s)),
                   pl.BlockSpec((1, 2, B, H2), lambda s: (s, 0, 0, 0))],
        out_shape=[jax.ShapeDtypeStruct((L * B, W), F32), jax.ShapeDtypeStruct((S5_SLICES, 2, B, H2), F32)],
        scratch_shapes=[pltpu.VMEM((2, Tc * B, H2), F32)],
        compiler_params=_cparams(1),
        name="s5_mixer",
    )(u_tm, wb, ar, ai, wc, dvec.reshape(1, W), h0)


def s5_params(a_re, a_im, log_dt, b_re, b_im, c_re, c_im):
    dt = jnp.exp(log_dt)[..., None]
    mag = jnp.exp(a_re * dt)
    abar_r = mag * jnp.cos(a_im * dt)
    abar_i = mag * jnp.sin(a_im * dt)
    den = a_re * a_re + a_im * a_im
    nr = abar_r - 1.0
    fr = (nr * a_re + abar_i * a_im) / den
    fi = (abar_i * a_re - nr * a_im) / den
    bbar_r = fr[..., None] * b_re - fi[..., None] * b_im
    bbar_i = fr[..., None] * b_im + fi[..., None] * b_re
    gs = S5_NGROUP // S5_SLICES
    eye = jnp.eye(gs, dtype=F32)
    bb = jnp.stack([bbar_r, bbar_i], axis=0).reshape(2, 2, S5_SLICES, gs, S5_STATE, S5_GROUP)
    wb = jnp.einsum('rdsgnp,gh->sdgprhn', bb, eye).reshape(S5_SLICES, 2, LANES, 2 * S5_HALF)
    cc = jnp.stack([c_re, -c_im], axis=0).reshape(2, 2, S5_SLICES, gs, S5_GROUP, S5_STATE)
    wc = jnp.einsum('rdsgpn,gh->sdrhngp', cc, eye).reshape(S5_SLICES, 2, 2 * S5_HALF, LANES)
    ar = abar_r.reshape(2, S5_SLICES, 1, S5_HALF).transpose(1, 0, 2, 3)
    ai = abar_i.reshape(2, S5_SLICES, 1, S5_HALF).transpose(1, 0, 2, 3)
    return wb.astype(BF16), ar, ai, wc.astype(BF16)


def _rope(x, cos, sina, sinb):
    return x * cos + pltpu.roll(x, ROPE_NF, 1) * sina + pltpu.roll(x, LANES - ROPE_NF, 1) * sinb


def _attend(qg, segs, sink_col):
    scores = []
    m = sink_col
    for k, _, mask in segs:
        s = _bdot_nt(qg, k) * ATTN_SCALE
        if mask is not None:
            s = jnp.where(mask, s, NEG_INF)
        scores.append(s)
        m = jnp.maximum(m, jnp.max(s, axis=-1, keepdims=True))
    den = jnp.exp(sink_col - m)
    o = None
    for s, (_, v, _) in zip(scores, segs):
        p = jnp.exp(s - m)
        den = den + jnp.sum(p, axis=-1, keepdims=True)
        pv = _bdot(p, v)
        o = pv if o is None else o + pv
    return o / den


def _sink_col(sink_ref, kvh):
    return jnp.concatenate([jnp.full((BLOCK, 1), sink_ref[kvh * Q_PER_KV + g], F32) for g in range(Q_PER_KV)], axis=0)


def _stack_heads(q, kvh):
    return jnp.concatenate([q[:, (kvh * Q_PER_KV + g) * HEAD_DIM:(kvh * Q_PER_KV + g + 1) * HEAD_DIM]
                            for g in range(Q_PER_KV)], axis=0)


def _store_heads(o_ref, og, kvh):
    for g in range(Q_PER_KV):
        h = kvh * Q_PER_KV + g
        o_ref[:, h * HEAD_DIM:(h + 1) * HEAD_DIM] = og[g * BLOCK:(g + 1) * BLOCK]


def _ctx_attn_kernel(sink_ref, q_ref, k_ref, v_ref, o_ref):
    q = q_ref[...]
    k = k_ref[...]
    v = v_ref[...]
    for kvh in range(N_KV_HEADS):
        ks = k[:, kvh * HEAD_DIM:(kvh + 1) * HEAD_DIM]
        vs = v[:, kvh * HEAD_DIM:(kvh + 1) * HEAD_DIM]
        og = _attend(_stack_heads(q, kvh), [(ks, vs, None)], _sink_col(sink_ref, kvh))
        _store_heads(o_ref, og, kvh)


def ctx_attention(P, sink, B, L):
    nb = L // BLOCK
    return pl.pallas_call(
        _ctx_attn_kernel,
        grid=(B, nb),
        in_specs=[pl.BlockSpec(memory_space=pltpu.SMEM),
                  pl.BlockSpec((BLOCK, BRANCH_W), lambda b, j: (b * nb + j, COL_CQ // BRANCH_W)),
                  pl.BlockSpec((L, KV_W), lambda b, j: (b, COL_CK // KV_W)),
                  pl.BlockSpec((L, KV_W), lambda b, j: (b, COL_CV // KV_W))],
        out_specs=pl.BlockSpec((BLOCK, BRANCH_W), lambda b, j: (b * nb + j, 0)),
        out_shape=jax.ShapeDtypeStruct((B * L, BRANCH_W), F32),
        compiler_params=_cparams(2),
        name="ctx_attention",
    )(sink.reshape(N_Q_HEADS), P, P, P)


def _lat_attn_kernel(sink_ref, q_ref, kp_ref, kc_ref, kn_ref, vp_ref, vc_ref, vn_ref, ck_ref, cv_ref,
                     cq_ref, saq_ref, sbq_ref, cp_ref, sap_ref, sbp_ref, cn_ref, san_ref, sbn_ref, o_ref, *, nb):
    j = pl.program_id(1)
    cq, saq, sbq = cq_ref[...], saq_ref[...], sbq_ref[...]
    q = jnp.concatenate([_rope(q_ref[:, c * LANES:(c + 1) * LANES], cq, saq, sbq)
                         for c in range(BRANCH_W // LANES)], axis=1)
    kp = _rope(kp_ref[...], cp_ref[...], sap_ref[...], sbp_ref[...])
    kc = _rope(kc_ref[...], cq, saq, sbq)
    kn = _rope(kn_ref[...], cn_ref[...], san_ref[...], sbn_ref[...])
    R = Q_PER_KV * BLOCK
    r = lax.broadcasted_iota(jnp.int32, (R, BLOCK), 0) % BLOCK
    c = lax.broadcasted_iota(jnp.int32, (R, BLOCK), 1)
    mask_p = (c >= r) & (j > 0)
    mask_n = (c <= r) & (j < nb - 1)
    ck = ck_ref[0, 0]
    cv = cv_ref[0, 0]
    for kvh in range(N_KV_HEADS):
        sl = slice(kvh * HEAD_DIM, (kvh + 1) * HEAD_DIM)
        segs = [(ck[:, sl], cv[:, sl], None),
                (kp[:, sl], vp_ref[:, sl], mask_p),
                (kc[:, sl], vc_ref[:, sl], None),
                (kn[:, sl], vn_ref[:, sl], mask_n)]
        og = _attend(_stack_heads(q, kvh), segs, _sink_col(sink_ref, kvh))
        _store_heads(o_ref, og, kvh)


def lat_attention(P, cache_k, cache_v, layer, sink, rope_tabs, B, L):
    nb = L // BLOCK
    kcol, vcol = COL_CK // KV_W, COL_CV // KV_W
    prev = lambda b, j: b * nb + jnp.maximum(j - 1, 0)
    nxt = lambda b, j: b * nb + jnp.minimum(j + 1, nb - 1)
    kv = lambda rowf, col: pl.BlockSpec((BLOCK, KV_W), lambda b, j: (rowf(b, j), col))
    cur = lambda b, j: b * nb + j
    tab = lambda f: pl.BlockSpec((BLOCK, LANES), f)
    tq = lambda b, j: (j, 0)
    tp = lambda b, j: (jnp.maximum(j - 1, 0), 0)
    tn = lambda b, j: (jnp.minimum(j + 1, nb - 1), 0)
    cos, sina, sinb = rope_tabs
    cache = pl.BlockSpec((1, 1, PAST_LEN, KV_W), lambda b, j: (b, layer, 0, 0))
    return pl.pallas_call(
        functools.partial(_lat_attn_kernel, nb=nb),
        grid=(B, nb),
        in_specs=[pl.BlockSpec(memory_space=pltpu.SMEM),
                  pl.BlockSpec((BLOCK, BRANCH_W), lambda b, j: (b * nb + j, COL_CQ // BRANCH_W)),
                  kv(prev, kcol), kv(cur, kcol), kv(nxt, kcol),
                  kv(prev, vcol), kv(cur, vcol), kv(nxt, vcol),
                  cache, cache,
                  tab(tq), tab(tq), tab(tq), tab(tp), tab(tp), tab(tp), tab(tn), tab(tn), tab(tn)],
        out_specs=pl.BlockSpec((BLOCK, BRANCH_W), lambda b, j: (b * nb + j, 0)),
        out_shape=jax.ShapeDtypeStruct((B * L, BRANCH_W), F32),
        compiler_params=_cparams(2),
        name="lat_attention",
    )(sink.reshape(N_Q_HEADS), P, P, P, P, P, P, P,
      cache_k.reshape(cache_k.shape[0], DEPTH, PAST_LEN, KV_W), cache_v.reshape(cache_v.shape[0], DEPTH, PAST_LEN, KV_W),
      cos, sina, sinb, cos, sina, sinb, cos, sina, sinb)


def rope_tables(L):
    t = jnp.arange(L)
    inv = ROPE_BASE ** (-jnp.arange(ROPE_NF, dtype=F32) / ROPE_NF)
    ang_r = (t // GRID_W).astype(F32)[:, None] * inv
    ang_c = (t % GRID_W).astype(F32)[:, None] * inv
    z = jnp.zeros((L, ROPE_NF), F32)
    cos = jnp.concatenate([jnp.cos(ang_r)] * 2 + [jnp.cos(ang_c)] * 2, axis=1)
    sina = jnp.concatenate([z, jnp.sin(ang_r), z, jnp.sin(ang_c)], axis=1)
    sinb = jnp.concatenate([-jnp.sin(ang_r), z, -jnp.sin(ang_c), z], axis=1)
    return tuple(jnp.tile(a, (1, LANES // HEAD_DIM)) for a in (cos, sina, sinb))


RWKV_CHUNK = 64
RWKV_PREP_CHUNKS = 4


def _split(a):
    hi = a.astype(BF16)
    return hi, (a - hi.astype(F32)).astype(BF16)


def _bdot3(a, b):
    ah, al = _split(a)
    bh, bl = _split(b)
    return jnp.dot(jnp.concatenate([ah, al, ah], axis=1), jnp.concatenate([bh, bh, bl], axis=0),
                   preferred_element_type=F32)


def _head_sum(x, m0, m1):
    s0 = jnp.sum(x * m0, axis=-1, keepdims=True)
    s1 = jnp.sum(x * m1, axis=-1, keepdims=True)
    return s0 * m0 + s1 * m1


def _unit_lower_inverse(ns, eye):
    xs = [eye + n for n in ns]
    pws = list(ns)
    for _ in range(5):
        pws = [_bdot(pw, pw) for pw in pws]
        xs = [x + _bdot(x, pw) for x, pw in zip(xs, pws)]
    resids = [eye - (x - _bdot3(n, x)) for n, x in zip(ns, xs)]
    return [x + _bdot(x, resid) for x, resid in zip(xs, resids)]


def _rwkv_kernel(xr_ref, xk_ref, xv_ref, lo_ref, mu_ref, w0_ref, w2_ref, a0_ref, a2_ref, g2_ref,
                 kk_ref, ka_ref, rk_ref, lnw_ref, lnb_ref, t0_ref,
                 y_ref, tfin_ref,
                 r_scr, kh_scr, v_scr, kn_scr, lw_scr, a_scr, kd_scr, yd_scr,
                 phi_scr, psi_scr, pcm_scr, rys_scr, gys_scr, *, L):
    C = RWKV_CHUNK
    nch = L // C
    lane = lax.broadcasted_iota(jnp.int32, (1, LANES), 1)
    m0 = (lane < RWKV_HEAD).astype(F32)
    m1 = 1.0 - m0

    def shift(x, mu):
        return x + mu * (0.5 * (_shift_rows(x, L, True) + _shift_rows(x, L, False)) - x)

    r = shift(xr_ref[...], mu_ref[0:1, :])
    kh = shift(xk_ref[...], mu_ref[1:2, :])
    vh = shift(xv_ref[...], mu_ref[2:3, :])
    r_scr[...] = r
    kh_scr[...] = kh
    v_scr[...] = vh
    kk = kh * kk_ref[...]
    kn_scr[...] = kk * lax.rsqrt(_head_sum(kk * kk, m0, m1) + 1e-12)
    lo = lo_ref[...]
    tw = jnp.tanh(lo)
    for d in range(2):
        z = w0_ref[d] + _bdot(tw, w2_ref[d])
        lw_scr[d] = -math.exp(-0.5) * _sigmoid(z)
        a = _sigmoid(a0_ref[d] + _bdot(lo, a2_ref[d]))
        a_scr[d] = a
        kd_scr[d] = kh * (1.0 + (a - 1.0) * ka_ref[...])

    ri = lax.broadcasted_iota(jnp.int32, (2 * C, 2 * C), 0)
    ci = lax.broadcasted_iota(jnp.int32, (2 * C, 2 * C), 1)
    same = (ri // C) == (ci // C)
    eye2 = (ri == ci).astype(F32)
    ti = lax.broadcasted_iota(jnp.int32, (C, C), 0)
    si = lax.broadcasted_iota(jnp.int32, (C, C), 1)
    rt, ct = ri % C, ci % C
    strict = (same & (ct < rt), same & (ct > rt))
    incl = (same & (ct <= rt), same & (ct >= rt))
    tri = ((si <= ti).astype(F32), (si >= ti).astype(F32))

    def stack(x):
        return jnp.concatenate([x * m0, x * m1], axis=0)

    def unstack(x):
        return x[0:C] + x[C:2 * C]

    def prepare(jobs):
        J = range(len(jobs))
        ds = [d for _, d in jobs]
        rows = [pl.ds(pl.multiple_of(c * C, C), C) for c, _ in jobs]
        lw = [lw_scr[ds[j], rows[j], :] for j in J]
        cum = [jnp.dot(tri[ds[j]], lw[j], preferred_element_type=F32, precision=HIGHEST) for j in J]
        p_in = [jnp.exp(cum[j]) for j in J]
        p_ex = [jnp.exp(cum[j] - lw[j]) for j in J]
        ip = [jnp.exp(-cum[j]) for j in J]
        pc = [p_in[j][C - 1:C, :] if ds[j] == 0 else p_in[j][0:1, :] for j in J]
        kn = [kn_scr[rows[j], :] for j in J]
        KAP = [stack(kn[j] * p_ex[j]) for j in J]
        RT = [stack(r_scr[rows[j], :] * p_in[j]) for j in J]
        BT = [stack(-(a_scr[ds[j], rows[j], :] * kn[j]) * ip[j]) for j in J]
        KT = [stack(kd_scr[ds[j], rows[j], :] * ip[j]) for j in J]
        V = [stack(v_scr[rows[j], :]) for j in J]
        G = [_bdot_nt(jnp.concatenate([KAP[j], RT[j]], axis=0), jnp.concatenate([BT[j], KT[j]], axis=0))
             for j in J]
        aab = [jnp.where(strict[ds[j]], G[j][0:2 * C, 0:2 * C], 0.0) for j in J]
        aak = [jnp.where(strict[ds[j]], G[j][0:2 * C, 2 * C:4 * C], 0.0) for j in J]
        grb = [jnp.where(incl[ds[j]], G[j][2 * C:4 * C, 0:2 * C], 0.0) for j in J]
        grk = [jnp.where(incl[ds[j]], G[j][2 * C:4 * C, 2 * C:4 * C], 0.0) for j in J]
        AV = [_bdot(aak[j], V[j]) for j in J]
        GV = [_bdot(grk[j], V[j]) for j in J]
        X = _unit_lower_inverse(aab, eye2)
        XK = [_bdot(X[j], KAP[j]) for j in J]
        XAV = [_bdot(X[j], AV[j]) for j in J]
        btp_t = [(BT[j] * pc[j]).T for j in J]
        ktp_t = [(KT[j] * pc[j]).T for j in J]
        phi = [_bdot(btp_t[j], XK[j]) for j in J]
        psi = [_bdot(btp_t[j], XAV[j]) + _bdot(ktp_t[j], V[j]) for j in J]
        rys = [unstack(RT[j] + _bdot(grb[j], XK[j])) for j in J]
        gys = [unstack(_bdot(grb[j], XAV[j]) + GV[j]) for j in J]
        for j, (c, d) in enumerate(jobs):
            phi_scr[d, c] = phi[j].astype(BF16)
            psi_scr[d, c] = psi[j]
            pcm_scr[d, c] = jnp.broadcast_to(pc[j], (LANES, LANES)).T
            rys_scr[d, c] = rys[j].astype(BF16)
            gys_scr[d, c] = gys[j]

    def prep_body(i, carry):
        prepare([(RWKV_PREP_CHUNKS * i + u, d) for u in range(RWKV_PREP_CHUNKS) for d in range(2)])
        return carry

    lax.fori_loop(0, nch // RWKV_PREP_CHUNKS, prep_body, 0)

    def seq_body(i, carry):
        out = []
        for d, T in enumerate(carry):
            c = i if d == 0 else nch - 1 - i
            rows = pl.ds(pl.multiple_of(c * C, C), C)
            tb = T.astype(BF16)
            yd_scr[d, rows, :] = jnp.dot(rys_scr[d, c], tb, preferred_element_type=F32) + gys_scr[d, c]
            out.append(pcm_scr[d, c] * T + jnp.dot(phi_scr[d, c], tb, preferred_element_type=F32) + psi_scr[d, c])
        return tuple(out)

    tf, tb_ = lax.fori_loop(0, nch, seq_body, (t0_ref[0, 0], t0_ref[0, 1]))
    tfin_ref[0, 0] = tf
    tfin_ref[0, 1] = tb_

    y = yd_scr[0] + yd_scr[1]
    inv_n = 1.0 / RWKV_HEAD
    mu = _head_sum(y, m0, m1) * inv_n
    yc = y - mu
    var = _head_sum(yc * yc, m0, m1) * inv_n
    yn = yc * lax.rsqrt(var + RWKV_GN_EPS) * lnw_ref[...] + lnb_ref[...]
    bonus = _head_sum(r * kh * rk_ref[...], m0, m1) * vh
    g = _bdot(_sigmoid(lo), g2_ref[...])
    y_ref[...] = (yn + bonus) * g


def rwkv_mixer(P, prm, t0, B, L):
    HP = BRANCH_W // LANES
    nl = L
    col = lambda c0: (lambda b, h: (b, c0 // LANES + h))
    vecs = lambda n: pl.BlockSpec((n, LANES), lambda b, h: (0, h))
    dvec = pl.BlockSpec((2, 1, LANES), lambda b, h: (0, 0, h))
    lora_w = pl.BlockSpec((2, LORA_W, LANES), lambda b, h: (0, 0, h))
    seq = lambda: pltpu.VMEM((L, LANES), F32)
    seq2 = lambda: pltpu.VMEM((2, L, LANES), F32)
    nch = L // RWKV_CHUNK
    return pl.pallas_call(
        functools.partial(_rwkv_kernel, L=L),
        grid=(B, HP),
        in_specs=[pl.BlockSpec((nl, LANES), col(COL_DR)),
                  pl.BlockSpec((nl, LANES), col(COL_DK)),
                  pl.BlockSpec((nl, LANES), col(COL_DV)),
                  pl.BlockSpec((nl, LORA_W), lambda b, h: (b, COL_LORA // LORA_W)),
                  vecs(3), dvec, lora_w, dvec, lora_w,
                  pl.BlockSpec((LORA_W, LANES), lambda b, h: (0, h)),
                  vecs(1), vecs(1), vecs(1), vecs(1), vecs(1),
                  pl.BlockSpec((1, 2, LANES, LANES), lambda b, h: (b * HP + h, 0, 0, 0))],
        out_specs=[pl.BlockSpec((nl, LANES), lambda b, h: (b, h)),
                   pl.BlockSpec((1, 2, LANES, LANES), lambda b, h: (b * HP + h, 0, 0, 0))],
        out_shape=[jax.ShapeDtypeStruct((B * L, BRANCH_W), F32),
                   jax.ShapeDtypeStruct((B * HP, 2, LANES, LANES), F32)],
        scratch_shapes=[seq(), seq(), seq(), seq(), seq2(), seq2(), seq2(), seq2(),
                        pltpu.VMEM((2, nch, LANES, LANES), BF16), pltpu.VMEM((2, nch, LANES, LANES), F32),
                        pltpu.VMEM((2, nch, LANES, LANES), F32), pltpu.VMEM((2, nch, RWKV_CHUNK, LANES), BF16),
                        pltpu.VMEM((2, nch, RWKV_CHUNK, LANES), F32)],
        compiler_params=_cparams(2),
        name="rwkv_mixer",
    )(P, P, P, P, prm['mu'], prm['w0'], prm['w2'], prm['a0'], prm['a2'], prm['g2'],
      prm['k_k'], prm['k_a'], prm['r_k'], prm['ln_w'], prm['ln_b'], t0)


def _merge_kernel(ag_ref, hs_ref, ys_ref, yc_ref, yd_ref, m0_ref, m1_ref, m2_ref, m3_ref, x_ref, g1_ref,
                  gw_ref, gb_ref, bw_ref, wo_ref, o_ref):
    y_a = _gelu(ag_ref[...]) * hs_ref[...]
    y_s = _gelu(ys_ref[...])
    y_b = y_s * _sigmoid(_bdot(y_s, gw_ref[...]) + gb_ref[...])
    acc = None
    for n, (y, m_ref) in enumerate(((y_a, m0_ref), (y_b, m1_ref), (yc_ref[...], m2_ref), (yd_ref[...], m3_ref))):
        t = _sigmoid(m_ref[...]) * _bdot(y, bw_ref[n])
        acc = t if acc is None else acc + t
    o_ref[...] = x_ref[...] + g1_ref[0] * _bdot(acc, wo_ref[...])


def merge_out(P, hs_tm, ysum_tm, yc, yd, x, g1, glu_w, glu_b, branch_w, w_out, L, bm):
    M, D = x.shape
    W = BRANCH_W
    nb = g1.shape[0]
    assert L % bm == 0
    per = L // bm
    bidx = (lambda i: (i // per, 0, 0)) if nb > 1 else (lambda i: (0, 0, 0))
    row = lambda w, c: pl.BlockSpec((bm, w), lambda i: (i, c))
    tm = pl.BlockSpec((bm, W), lambda i: (i % per, i // per))
    hs, ysum = (a.reshape(L, (M // L) * W) for a in (hs_tm, ysum_tm))
    const = lambda shape: pl.BlockSpec(shape, lambda i: (0,) * len(shape), pipeline_mode=pl.Buffered(1))
    mcol = COL_MERGE // D
    return pl.pallas_call(
        _merge_kernel,
        grid=(M // bm,),
        in_specs=[row(W, COL_AGATE // W), tm, tm, row(W, 0), row(W, 0),
                  row(D, mcol), row(D, mcol + 1), row(D, mcol + 2), row(D, mcol + 3),
                  row(D, 0), pl.BlockSpec((1, 1, D), bidx),
                  const((W, W)), const((1, W)), const((N_BRANCH, W, D)), const((D, D))],
        out_specs=row(D, 0),
        out_shape=jax.ShapeDtypeStruct((M, D), F32),
        compiler_params=_cparams(1),
        name="merge_out",
    )(P, hs, ysum, yc, yd, P, P, P, P, x, g1, glu_w, glu_b.reshape(1, W), branch_w, w_out)


def _shift_rows(g, L, back):
    bm = g.shape[0]
    rolled = pltpu.roll(g, 1 if back else bm - 1, 0)
    r8 = lax.broadcasted_iota(jnp.int32, (SUBLANES, 1), 0)
    pieces = []
    for s in range(0, bm, L):
        if back:
            pieces += [jnp.where(r8 == 0, 0.0, rolled[s:s + SUBLANES]), rolled[s + SUBLANES:s + L]]
        else:
            e = s + L - SUBLANES
            pieces += [rolled[s:e], jnp.where(r8 == SUBLANES - 1, 0.0, rolled[e:s + L])]
    return jnp.concatenate(pieces, axis=0)


def _ffn_out_kernel(gc_ref, val_ref, cw_ref, cb_ref, w_ref, x_ref, g2_ref, o_ref, acc_scr, *, L):
    k = pl.program_id(1)

    @pl.when(k == 0)
    def _():
        acc_scr[...] = jnp.zeros_like(acc_scr)

    g = gc_ref[...]
    gate = (_shift_rows(g, L, True) * cw_ref[0:1, :] + g * cw_ref[1:2, :]
            + _shift_rows(g, L, False) * cw_ref[2:3, :] + cb_ref[...])
    half = 0.5 * gate
    act = (half + half * jnp.tanh(half)) * val_ref[...]
    acc_scr[...] += _bdot(act, w_ref[...])

    @pl.when(k == pl.num_programs(1) - 1)
    def _():
        o_ref[...] = x_ref[...] + g2_ref[0] * acc_scr[...]


def ffn_out(U, x, g2, conv_w, conv_b, w, L, bm, bk):
    M, D = x.shape
    nk = D_FF_PAD // bk
    nb = g2.shape[0]
    assert bm % L == 0 and nk >= 2 and (nb == 1 or bm == L)
    bidx = (lambda i, k: (i, 0, 0)) if nb > 1 else (lambda i, k: (0, 0, 0))
    return pl.pallas_call(
        functools.partial(_ffn_out_kernel, L=L),
        grid=(M // bm, nk),
        in_specs=[pl.BlockSpec((bm, bk), lambda i, k: (i, k)),
                  pl.BlockSpec((bm, bk), lambda i, k: (i, nk + k)),
                  pl.BlockSpec((3, bk), lambda i, k: (0, k)),
                  pl.BlockSpec((1, bk), lambda i, k: (0, k)),
                  pl.BlockSpec((bk, D), lambda i, k: (k, 0)),
                  pl.BlockSpec((bm, D), lambda i, k: (i, 0), pipeline_mode=pl.Buffered(1)),
                  pl.BlockSpec((1, 1, D), bidx)],
        out_specs=pl.BlockSpec((bm, D), lambda i, k: (i, 0)),
        out_shape=jax.ShapeDtypeStruct((M, D), F32),
        scratch_shapes=[pltpu.VMEM((bm, D), F32)],
        compiler_params=_cparams(2),
        name="ffn_out",
    )(U, U, conv_w, conv_b.reshape(1, D_FF_PAD), w, x, g2)


def _final_norm_kernel(x_ref, g_ref, o_ref):
    x = x_ref[...]
    o_ref[...] = x * lax.rsqrt(jnp.mean(x * x, axis=-1, keepdims=True) + EPS) * g_ref[...]


def final_norm(x, g, bm):
    M, D = x.shape
    return pl.pallas_call(
        _final_norm_kernel,
        grid=(M // bm,),
        in_specs=[pl.BlockSpec((bm, D), lambda i: (i, 0)), pl.BlockSpec((1, D), lambda i: (0, 0))],
        out_specs=pl.BlockSpec((bm, D), lambda i: (i, 0)),
        out_shape=jax.ShapeDtypeStruct((M, D), F32),
        compiler_params=_cparams(1),
        name="final_norm",
    )(x, g.reshape(1, D))


def _block_diag(w):
    n, bs, _ = w.shape
    return jnp.einsum('ncd,nm->ncmd', w, jnp.eye(n, dtype=w.dtype)).reshape(n * bs, n * bs)


def _pad_rows(w, start, total):
    k = w.shape[-2]
    pad = [(0, 0)] * (w.ndim - 2) + [(start, total - start - k), (0, 0)]
    return jnp.pad(w, pad)


def prepare_layer(l, w):
    W = BRANCH_W
    p = {}
    p['lru_wa'] = jnp.stack([_block_diag(w['lru_wa'][l, d]) for d in range(2)]).astype(BF16)
    p['lru_wx'] = jnp.stack([_block_diag(w['lru_wx'][l, d]) for d in range(2)]).astype(BF16)
    p['lru_sp'] = jax.nn.softplus(-w['lru_lam'][l])
    p['s5'] = s5_params(w['s5_a_re'][l], w['s5_a_im'][l], w['s5_log_dt'][l], w['s5_b_re'][l], w['s5_b_im'][l],
                        w['s5_c_re'][l], w['s5_c_im'][l])
    p['rwkv'] = dict(
        mu=w['rwkv_mu'][l], w0=w['rwkv_w0'][l].reshape(2, 1, W), a0=w['rwkv_a0'][l].reshape(2, 1, W),
        w2=_pad_rows(w['rwkv_w2'][l], 0, LORA_W).astype(BF16),
        a2=_pad_rows(w['rwkv_a2'][l], W_LORA, LORA_W).astype(BF16),
        g2=_pad_rows(w['rwkv_g2'][l], W_LORA + A_LORA, LORA_W).astype(BF16),
        k_k=w['rwkv_k_k'][l].reshape(1, W), k_a=w['rwkv_k_a'][l].reshape(1, W),
        r_k=w['rwkv_r_k'][l].reshape(1, W), ln_w=w['rwkv_ln_w'][l].reshape(1, W), ln_b=w['rwkv_ln_b'][l].reshape(1, W))
    p['glu_w'] = w['s5_glu_w'][l].astype(BF16)
    p['branch_w'] = w['branch_w'][l].astype(BF16)
    p['w_out'] = w['w_out'][l].astype(BF16)
    fpad = D_FF_PAD - D_FF
    fw = w['ffn_w_in'][l]
    p['ffn_w_in'] = jnp.concatenate([jnp.pad(fw[:, :D_FF], ((0, 0), (0, fpad))),
                                     jnp.pad(fw[:, D_FF:], ((0, 0), (0, fpad)))], axis=1).astype(BF16)
    p['ffn_conv_w'] = jnp.pad(w['ffn_conv_w'][l], ((0, 0), (0, fpad)))
    p['ffn_conv_b'] = jnp.pad(w['ffn_conv_b'][l], (0, fpad))
    p['ffn_w_out'] = jnp.pad(w['ffn_w_out'][l], ((0, fpad), (0, 0))).astype(BF16)
    return p


def _to_tm(a, B, L):
    return a.reshape(B, L, -1).transpose(1, 0, 2).reshape(L * B, -1)


def _to_bm(a, B, L):
    return a.reshape(L, B, -1).transpose(1, 0, 2).reshape(B * L, -1)


def layer_step(x, mod, p, w, l, B, L, ctx, rope_tabs, bm):
    W = BRANCH_W
    D = D_MODEL
    lat = ctx is not None
    nb = mod.shape[0]
    sh1, sc1, g1, sh2, sc2, g2 = (mod[:, i * D:(i + 1) * D].reshape(nb, 1, D) for i in range(6))
    P = norm_mod_matmul(x, w['norm1'][l], sc1, sh1, w['w_in'], l, L, bm, 1024)

    ax = P[:, COL_AX:COL_AX + W].reshape(B, L, W).transpose(1, 0, 2)
    ax = jnp.pad(ax, ((LRU_PAD_L, LRU_CONV_W - 1 - LRU_PAD_L), (0, 0), (0, 0))).reshape((L + 3) * B, W)
    lru0 = ctx['lru'].transpose(1, 0, 2) if lat else jnp.zeros((2, B, W), F32)
    hs_tm, lru_fin = lru_mixer(ax, w['lru_conv_w'][l], w['lru_conv_b'][l], p['lru_wa'], w['lru_ba'][l],
                               p['lru_wx'], w['lru_bx'][l], p['lru_sp'], lru0, B, L)

    gs = S5_NGROUP // S5_SLICES
    if lat:
        s50 = ctx['s5'].reshape(B, 2, 2, S5_SLICES, gs * S5_STATE).transpose(3, 1, 0, 2, 4).reshape(
            S5_SLICES, 2, B, 2 * S5_HALF)
    else:
        s50 = jnp.zeros((S5_SLICES, 2, B, 2 * S5_HALF), F32)
    wb, ar, ai, wc = p['s5']
    ys_tm, s5_fin = s5_mixer(_to_tm(P[:, COL_BU:COL_BU + W], B, L), wb, ar, ai, wc, w['s5_d'][l], s50, B, L)

    sink = w['attn_sink'][l]
    if lat:
        yc = lat_attention(P, ctx['k_all'], ctx['v_all'], l, sink, rope_tabs, B, L)
    else:
        yc = ctx_attention(P, sink, B, L)

    HP = W // LANES
    if lat:
        s = ctx['wkv'].reshape(B, 2, HP, 2, RWKV_HEAD, RWKV_HEAD)
        eye = jnp.eye(2, dtype=F32)
        t0 = jnp.einsum('bdphvk,hg->bpdhkgv', s, eye).reshape(B * HP, 2, LANES, LANES)
    else:
        t0 = jnp.zeros((B * HP, 2, LANES, LANES), F32)
    yd, tfin = rwkv_mixer(P, p['rwkv'], t0, B, L)

    x = merge_out(P, hs_tm, ys_tm, yc, yd, x, g1, p['glu_w'], w['s5_glu_b'][l], p['branch_w'], p['w_out'], L, 256)
    U = norm_mod_matmul(x, w['norm2'][l], sc2, sh2, p['ffn_w_in'][None], 0, L, bm, 512)
    x = ffn_out(U, x, g2, p['ffn_conv_w'], p['ffn_conv_b'], p['ffn_w_out'], L, bm, 512)

    new = None
    if not lat:
        k_new = P[:, COL_CK:COL_CK + KV_W].reshape(B, L, N_KV_HEADS, HEAD_DIM)
        v_new = P[:, COL_CV:COL_CV + KV_W].reshape(B, L, N_KV_HEADS, HEAD_DIM)
        lru_new = lru_fin.transpose(1, 0, 2)
        s5_new = s5_fin.reshape(S5_SLICES, 2, B, 2, gs, S5_STATE).transpose(2, 1, 3, 0, 4, 5).reshape(
            B, 2, 2, S5_NGROUP, S5_STATE)
        tf = tfin.reshape(B, HP, 2, 2, RWKV_HEAD, 2, RWKV_HEAD)
        wkv_new = jnp.einsum('bpdhkhv->bdphvk', tf).reshape(B, 2, RWKV_NH, RWKV_HEAD, RWKV_HEAD)
        new = (k_new, v_new, lru_new, s5_new, wkv_new)
    return x, new


def kernel(x_prompt, x_sample, cache_k, cache_v, state_lru, state_s5, state_wkv, c, c_ctx, mod_w, mod_b, norm1, norm2, norm_final, w_in, lru_conv_w, lru_conv_b, lru_wa, lru_ba, lru_wx, lru_bx, lru_lam, s5_a_re, s5_a_im, s5_log_dt, s5_b_re, s5_b_im, s5_c_re, s5_c_im, s5_d, s5_glu_w, s5_glu_b, attn_sink, rwkv_mu, rwkv_w0, rwkv_w2, rwkv_a0, rwkv_a2, rwkv_g2, rwkv_k_k, rwkv_k_a, rwkv_r_k, rwkv_ln_w, rwkv_ln_b, branch_w, w_out, ffn_w_in, ffn_conv_w, ffn_conv_b, ffn_w_out):
    w = dict(norm1=norm1, norm2=norm2, w_in=w_in,
             lru_conv_w=lru_conv_w, lru_conv_b=lru_conv_b, lru_wa=lru_wa, lru_ba=lru_ba,
             lru_wx=lru_wx, lru_bx=lru_bx, lru_lam=lru_lam,
             s5_a_re=s5_a_re, s5_a_im=s5_a_im, s5_log_dt=s5_log_dt, s5_b_re=s5_b_re, s5_b_im=s5_b_im,
             s5_c_re=s5_c_re, s5_c_im=s5_c_im, s5_d=s5_d, s5_glu_w=s5_glu_w, s5_glu_b=s5_glu_b,
             attn_sink=attn_sink,
             rwkv_mu=rwkv_mu, rwkv_w0=rwkv_w0, rwkv_w2=rwkv_w2, rwkv_a0=rwkv_a0, rwkv_a2=rwkv_a2,
             rwkv_g2=rwkv_g2, rwkv_k_k=rwkv_k_k, rwkv_k_a=rwkv_k_a, rwkv_r_k=rwkv_r_k,
             rwkv_ln_w=rwkv_ln_w, rwkv_ln_b=rwkv_ln_b,
             branch_w=branch_w, w_out=w_out, ffn_w_in=ffn_w_in, ffn_conv_w=ffn_conv_w,
             ffn_conv_b=ffn_conv_b, ffn_w_out=ffn_w_out)
    Bc, Lc, D = x_prompt.shape
    Bl, Ll, _ = x_sample.shape
    nrow = 16
    c_all = jnp.concatenate([c_ctx[None, :], c, jnp.zeros((nrow - 1 - Bl, D), F32)], axis=0)
    mods = modulation(c_all, mod_w, mod_b)
    layers = [prepare_layer(l, w) for l in range(DEPTH)]
    rope_tabs = rope_tables(Ll)

    xp = x_prompt.reshape(Bc * Lc, D)
    news = []
    for l in range(DEPTH):
        xp, new = layer_step(xp, mods[l, 0:1], layers[l], w, l, Bc, Lc, None, None, 1024)
        news.append(new)
    y_prompt = final_norm(xp, norm_final, 512).reshape(Bc, Lc, D)

    xs = x_sample.reshape(Bl * Ll, D)
    for l in range(DEPTH):
        ctx = dict(k_all=cache_k, v_all=cache_v, lru=state_lru[:, l], s5=state_s5[:, l], wkv=state_wkv[:, l])
        xs, _ = layer_step(xs, mods[l, 1:1 + Bl], layers[l], w, l, Bl, Ll, ctx, rope_tabs, min(1024, Ll))
    y_sample = final_norm(xs, norm_final, 512).reshape(Bl, Ll, D)

    stack = lambda i: jnp.stack([n[i] for n in news], axis=1)
    return (y_prompt, y_sample, stack(0), stack(1), stack(2), stack(3), stack(4))
```

```python
import functools
import math

import jax
import jax.numpy as jnp
from jax import lax
from jax.experimental import pallas as pl
from jax.experimental.pallas import tpu as pltpu

F32 = jnp.float32
BF16 = jnp.bfloat16
HIGHEST = lax.Precision.HIGHEST

D_MODEL = 2048
DEPTH = 2
PAST_LEN = 256
GRID_W = 64
N_BRANCH = 4
BRANCH_W = 512
EPS = 1e-6
LRU_BLOCKS = 8
LRU_BS = 64
LRU_CONV_W = 4
LRU_PAD_L = 2
LRU_C = 8.0
S5_GROUP = 16
S5_NGROUP = 32
S5_STATE = 64
HEAD_DIM = 64
N_Q_HEADS = 8
N_KV_HEADS = 2
Q_PER_KV = 4
KV_W = 128
WINDOW = 128
BLOCK = 128
ROPE_BASE = 10000.0
ROPE_NF = 16
ATTN_SCALE = 1.0 / math.sqrt(HEAD_DIM)
NEG_INF = -1e30
RWKV_HEAD = 64
RWKV_NH = 8
W_LORA = 64
A_LORA = 64
G_LORA = 128
RWKV_GN_EPS = 64e-5
D_FF = 5504
FFN_BLOCK = 512
D_FF_PAD = 5632
FFN_PIECES = 4
FFN_TAIL = 3
LANES = 128
SUBLANES = 8
VMEM_LIMIT = 56 * 2 ** 20

COL_AX, COL_AGATE, COL_BU, COL_CQ = 0, 512, 1024, 1536
COL_CK, COL_CV = 2048, 2176
COL_DR, COL_DK, COL_DV, COL_LORA = 2304, 2816, 3328, 3840
COL_MERGE = 4096
D_IN = COL_MERGE + N_BRANCH * D_MODEL
LORA_W = W_LORA + A_LORA + G_LORA


def _cparams(n_grid):
    return pltpu.CompilerParams(dimension_semantics=("arbitrary",) * n_grid, vmem_limit_bytes=VMEM_LIMIT)


def _bdot(a, b):
    return jnp.dot(a.astype(BF16), b.astype(BF16), preferred_element_type=F32)


def _bdot_nt(a, b):
    return lax.dot_general(a.astype(BF16), b.astype(BF16), (((1,), (1,)), ((), ())),
                           preferred_element_type=F32)


def _gelu(x):
    return jax.nn.gelu(x)


def _sigmoid(x):
    return 0.5 * jnp.tanh(0.5 * x) + 0.5


def _mod_kernel(c_ref, w_ref, b_ref, o_ref):
    c = c_ref[...]
    s = c * _sigmoid(c)
    o_ref[0] = jnp.dot(s, w_ref[0], preferred_element_type=F32, precision=HIGHEST) + b_ref[0]


def modulation(c_all, mod_w, mod_b):
    R, D = c_all.shape
    N = mod_w.shape[2]
    bn = 1024
    return pl.pallas_call(
        _mod_kernel,
        grid=(DEPTH, N // bn),
        in_specs=[pl.BlockSpec((R, D), lambda l, j: (0, 0)),
                  pl.BlockSpec((1, D, bn), lambda l, j: (l, 0, j)),
                  pl.BlockSpec((1, 1, bn), lambda l, j: (l, 0, j))],
        out_specs=pl.BlockSpec((1, R, bn), lambda l, j: (l, 0, j)),
        out_shape=jax.ShapeDtypeStruct((DEPTH, R, N), F32),
        compiler_params=_cparams(2),
        name="modulation",
    )(c_all, mod_w, mod_b.reshape(DEPTH, 1, N))


def _nmm_kernel(x_ref, g_ref, sc_ref, sh_ref, w_ref, o_ref, h_scr):
    @pl.when(pl.program_id(1) == 0)
    def _():
        x = x_ref[...]
        y = x * lax.rsqrt(jnp.mean(x * x, axis=-1, keepdims=True) + EPS) * g_ref[...]
        h_scr[...] = (y * (1.0 + sc_ref[0]) + sh_ref[0]).astype(BF16)

    o_ref[...] = jnp.dot(h_scr[...], w_ref[0].astype(BF16), preferred_element_type=F32)


def norm_mod_matmul(x, g, sc, sh, w, l, L, bm, bn):
    M, D = x.shape
    N = w.shape[2]
    nb = sc.shape[0]
    if nb > 1:
        assert L % bm == 0
        per = L // bm
        bidx = lambda i, j: (i // per, 0, 0)
    else:
        bidx = lambda i, j: (0, 0, 0)
    return pl.pallas_call(
        _nmm_kernel,
        grid=(M // bm, N // bn),
        in_specs=[pl.BlockSpec((bm, D), lambda i, j: (i, 0)),
                  pl.BlockSpec((1, D), lambda i, j: (0, 0)),
                  pl.BlockSpec((1, 1, D), bidx),
                  pl.BlockSpec((1, 1, D), bidx),
                  pl.BlockSpec((1, D, bn), lambda i, j: (l, 0, j))],
        out_specs=pl.BlockSpec((bm, bn), lambda i, j: (i, j)),
        out_shape=jax.ShapeDtypeStruct((M, N), F32),
        scratch_shapes=[pltpu.VMEM((bm, D), BF16)],
        compiler_params=_cparams(2),
        name="norm_mod_matmul",
    )(x, g.reshape(1, D), sc, sh, w)


def _ffn_cast_kernel(w0_ref, w1_ref, w2_ref, w3_ref, o_ref, *, axis, ntile):
    t = pl.program_id(1)
    for q, w_ref in enumerate((w0_ref, w1_ref, w2_ref, w3_ref)):
        piece = w_ref[0].astype(BF16)
        if axis == 1:
            o_ref[0, :, q * LANES:(q + 1) * LANES] = piece
        else:
            o_ref[0, q * LANES:(q + 1) * LANES, :] = piece

    @pl.when(t % ntile == ntile - 1)
    def _():
        pad = FFN_BLOCK - FFN_TAIL * LANES
        if axis == 1:
            o_ref[0, :, FFN_TAIL * LANES:] = jnp.zeros((o_ref.shape[1], pad), BF16)
        else:
            o_ref[0, FFN_TAIL * LANES:, :] = jnp.zeros((pad, o_ref.shape[2]), BF16)


def ffn_weights_bf16(w_in, w_out):
    nl, D, _ = w_in.shape
    ntile = D_FF_PAD // FFN_BLOCK
    half_pieces = D_FF // LANES

    def col_piece(q):
        last = 2 * half_pieces - 1
        return pl.BlockSpec((1, D, LANES), lambda l, j: (
            l, 0, jnp.minimum((j // ntile) * half_pieces + (j % ntile) * FFN_PIECES + q, last)))

    def row_piece(q):
        return pl.BlockSpec((1, LANES, D), lambda l, k: (l, jnp.minimum(k * FFN_PIECES + q, half_pieces - 1), 0))

    w_in_b = pl.pallas_call(
        functools.partial(_ffn_cast_kernel, axis=1, ntile=ntile),
        grid=(nl, 2 * ntile),
        in_specs=[col_piece(q) for q in range(FFN_PIECES)],
        out_specs=pl.BlockSpec((1, D, FFN_BLOCK), lambda l, j: (l, 0, j)),
        out_shape=jax.ShapeDtypeStruct((nl, D, 2 * D_FF_PAD), BF16),
        compiler_params=_cparams(2),
        name="ffn_w_in_cast",
    )(w_in, w_in, w_in, w_in)
    w_out_b = pl.pallas_call(
        functools.partial(_ffn_cast_kernel, axis=0, ntile=ntile),
        grid=(nl, ntile),
        in_specs=[row_piece(q) for q in range(FFN_PIECES)],
        out_specs=pl.BlockSpec((1, FFN_BLOCK, D), lambda l, k: (l, k, 0)),
        out_shape=jax.ShapeDtypeStruct((nl, D_FF_PAD, D), BF16),
        compiler_params=_cparams(2),
        name="ffn_w_out_cast",
    )(w_out, w_out, w_out, w_out)
    return w_in_b, w_out_b


def _lru_kernel(xp_ref, cw_ref, cb_ref, wa_ref, ba_ref, wx_ref, bx_ref, sp_ref, h0_ref,
                out_ref, fin_ref, a_scr, b_scr, *, B, L, Tc):
    nch = L // Tc
    R = Tc * B
    out_ref[...] = jnp.zeros_like(out_ref)

    def chunk(ci, hs):
        bases = (pl.multiple_of(ci * R, R), pl.multiple_of((nch - 1 - ci) * R, R))
        for d in range(2):
            base = bases[d]
            xa = cb_ref[...] + xp_ref[pl.ds(base, R), :] * cw_ref[0:1, :]
            for j in range(1, LRU_CONV_W):
                xa = xa + xp_ref[pl.ds(base + j * B, R), :] * cw_ref[j:j + 1, :]
            r = _sigmoid(_bdot(xa, wa_ref[d]) + ba_ref[d])
            i = _sigmoid(_bdot(xa, wx_ref[d]) + bx_ref[d])
            a = jnp.exp(-LRU_C * r * sp_ref[d])
            a_scr[d] = a
            b_scr[d] = jnp.sqrt(1.0 - a * a) * (i * xa)

        def step(ti, hs):
            new = []
            for d in range(2):
                off = pl.multiple_of((ti if d == 0 else Tc - 1 - ti) * B, B)
                h = a_scr[d, pl.ds(off, B), :] * hs[d] + b_scr[d, pl.ds(off, B), :]
                row = pl.ds(bases[d] + off, B)
                out_ref[row, :] = out_ref[row, :] + h
                new.append(h)
            return tuple(new)

        return lax.fori_loop(0, Tc, step, hs, unroll=8)

    hf, hb = lax.fori_loop(0, nch, chunk, (h0_ref[0], h0_ref[1]))
    fin_ref[0] = hf
    fin_ref[1] = hb


def lru_mixer(xp_tm, cw, cb, wa, ba, wx, bx, sp, h0, B, L):
    W = BRANCH_W
    Tc = min(128, L // 2)
    kern = functools.partial(_lru_kernel, B=B, L=L, Tc=Tc)
    vec = lambda: pl.BlockSpec((2, 1, LANES), lambda s: (0, 0, s))
    return pl.pallas_call(
        kern,
        grid=(W // LANES,),
        in_specs=[pl.BlockSpec(((L + 3) * B, LANES), lambda s: (0, s)),
                  pl.BlockSpec((LRU_CONV_W, LANES), lambda s: (0, s)),
                  pl.BlockSpec((1, LANES), lambda s: (0, s)),
                  pl.BlockSpec((2, LANES, LANES), lambda s: (0, s, s)), vec(),
                  pl.BlockSpec((2, LANES, LANES), lambda s: (0, s, s)), vec(),
                  vec(),
                  pl.BlockSpec((2, B, LANES), lambda s: (0, 0, s))],
        out_specs=[pl.BlockSpec((L * B, LANES), lambda s: (0, s)),
                   pl.BlockSpec((2, B, LANES), lambda s: (0, 0, s))],
        out_shape=[jax.ShapeDtypeStruct((L * B, W), F32), jax.ShapeDtypeStruct((2, B, W), F32)],
        scratch_shapes=[pltpu.VMEM((2, Tc * B, LANES), F32), pltpu.VMEM((2, Tc * B, LANES), F32)],
        compiler_params=_cparams(1),
        name="lru_mixer",
    )(xp_tm, cw, cb.reshape(1, W), wa, ba.reshape(2, 1, W), wx, bx.reshape(2, 1, W), sp.reshape(2, 1, W), h0)


S5_SLICES = BRANCH_W // LANES
S5_HALF = (S5_NGROUP // S5_SLICES) * S5_STATE


def _s5_kernel(u_ref, wb_ref, ar_ref, ai_ref, wc_ref, dv_ref, h0_ref, out_ref, fin_ref, hs_scr, *, B, L, Tc):
    nch = L // Tc
    R = Tc * B
    H = S5_HALF
    out_ref[...] = u_ref[...] * dv_ref[...]
    ab = [(ar_ref[0, d], ai_ref[0, d]) for d in range(2)]

    def chunk(ci, carry):
        bases = (pl.multiple_of(ci * R, R), pl.multiple_of((nch - 1 - ci) * R, R))
        for d in range(2):
            hs_scr[d] = _bdot(u_ref[pl.ds(bases[d], R), :], wb_ref[0, d])

        def step(ti, carry):
            new = []
            for d in range(2):
                hr, hi = carry[d]
                ar, ai = ab[d]
                row = pl.ds(pl.multiple_of((ti if d == 0 else Tc - 1 - ti) * B, B), B)
                nr = ar * hr - ai * hi + hs_scr[d, row, 0:H]
                ni = ar * hi + ai * hr + hs_scr[d, row, H:2 * H]
                hs_scr[d, row, 0:H] = nr
                hs_scr[d, row, H:2 * H] = ni
                new.append((nr, ni))
            return tuple(new)

        carry = lax.fori_loop(0, Tc, step, carry, unroll=2)
        for d in range(2):
            rows = pl.ds(bases[d], R)
            out_ref[rows, :] = out_ref[rows, :] + _bdot(hs_scr[d], wc_ref[0, d])
        return carry

    init = tuple((h0_ref[0, d, :, 0:H], h0_ref[0, d, :, H:2 * H]) for d in range(2))
    fin = lax.fori_loop(0, nch, chunk, init)
    for d in range(2):
        fin_ref[0, d, :, 0:H] = fin[d][0]
        fin_ref[0, d, :, H:2 * H] = fin[d][1]


def s5_mixer(u_tm, wb, ar, ai, wc, dvec, h0, B, L):
    W = BRANCH_W
    Tc = min(128, L // 2, 1024 // B)
    H2 = 2 * S5_HALF
    kern = functools.partial(_s5_kernel, B=B, L=L, Tc=Tc)
    return pl.pallas_call(
        kern,
        grid=(S5_SLICES,),
        in_specs=[pl.BlockSpec((L * B, LANES), lambda s: (0, s)),
                  pl.BlockSpec((1, 2, LANES, H2), lambda s: (s, 0, 0, 0)),
                  pl.BlockSpec((1, 2, 1, S5_HALF), lambda s: (s, 0, 0, 0)),
                  pl.BlockSpec((1, 2, 1, S5_HALF), lambda s: (s, 0, 0, 0)),
                  pl.BlockSpec((1, 2, H2, LANES), lambda s: (s, 0, 0, 0)),
                  pl.BlockSpec((1, LANES), lambda s: (0, s)),
                  pl.BlockSpec((1, 2, B, H2), lambda s: (s, 0, 0, 0))],
        out_specs=[pl.BlockSpec((L * B, LANES), lambda s: (0, s)),
                   pl.BlockSpec((1, 2, B, H2), lambda s: (s, 0, 0, 0))],
        out_shape=[jax.ShapeDtypeStruct((L * B, W), F32), jax.ShapeDtypeStruct((S5_SLICES, 2, B, H2), F32)],
        scratch_shapes=[pltpu.VMEM((2, Tc * B, H2), F32)],
        compiler_params=_cparams(1),
        name="s5_mixer",
    )(u_tm, wb, ar, ai, wc, dvec.reshape(1, W), h0)


def s5_params(a_re, a_im, log_dt, b_re, b_im, c_re, c_im):
    dt = jnp.exp(log_dt)[..., None]
    mag = jnp.exp(a_re * dt)
    abar_r = mag * jnp.cos(a_im * dt)
    abar_i = mag * jnp.sin(a_im * dt)
    den = a_re * a_re + a_im * a_im
    nr = abar_r - 1.0
    fr = (nr * a_re + abar_i * a_im) / den
    fi = (abar_i * a_re - nr * a_im) / den
    bbar_r = fr[..., None] * b_re - fi[..., None] * b_im
    bbar_i = fr[..., None] * b_im + fi[..., None] * b_re
    gs = S5_NGROUP // S5_SLICES
    eye = jnp.eye(gs, dtype=F32)
    bb = jnp.stack([bbar_r, bbar_i], axis=0).reshape(2, 2, S5_SLICES, gs, S5_STATE, S5_GROUP)
    wb = jnp.einsum('rdsgnp,gh->sdgprhn', bb, eye).reshape(S5_SLICES, 2, LANES, 2 * S5_HALF)
    cc = jnp.stack([c_re, -c_im], axis=0).reshape(2, 2, S5_SLICES, gs, S5_GROUP, S5_STATE)
    wc = jnp.einsum('rdsgpn,gh->sdrhngp', cc, eye).reshape(S5_SLICES, 2, 2 * S5_HALF, LANES)
    ar = abar_r.reshape(2, S5_SLICES, 1, S5_HALF).transpose(1, 0, 2, 3)
    ai = abar_i.reshape(2, S5_SLICES, 1, S5_HALF).transpose(1, 0, 2, 3)
    return wb.astype(BF16), ar, ai, wc.astype(BF16)


def _rope(x, cos, sina, sinb):
    return x * cos + pltpu.roll(x, ROPE_NF, 1) * sina + pltpu.roll(x, LANES - ROPE_NF, 1) * sinb


def _attend(qg, segs, sink_col):
    scores = []
    m = sink_col
    for k, _, mask in segs:
        s = _bdot_nt(qg, k) * ATTN_SCALE
        if mask is not None:
            s = jnp.where(mask, s, NEG_INF)
        scores.append(s)
        m = jnp.maximum(m, jnp.max(s, axis=-1, keepdims=True))
    den = jnp.exp(sink_col - m)
    o = None
    for s, (_, v, _) in zip(scores, segs):
        p = jnp.exp(s - m)
        den = den + jnp.sum(p, axis=-1, keepdims=True)
        pv = _bdot(p, v)
        o = pv if o is None else o + pv
    return o / den


def _sink_col(sink_ref, kvh):
    return jnp.concatenate([jnp.full((BLOCK, 1), sink_ref[kvh * Q_PER_KV + g], F32) for g in range(Q_PER_KV)], axis=0)


def _stack_heads(q, kvh):
    return jnp.concatenate([q[:, (kvh * Q_PER_KV + g) * HEAD_DIM:(kvh * Q_PER_KV + g + 1) * HEAD_DIM]
                            for g in range(Q_PER_KV)], axis=0)


def _store_heads(o_ref, og, kvh):
    for g in range(Q_PER_KV):
        h = kvh * Q_PER_KV + g
        o_ref[:, h * HEAD_DIM:(h + 1) * HEAD_DIM] = og[g * BLOCK:(g + 1) * BLOCK]


def _ctx_attn_kernel(sink_ref, q_ref, k_ref, v_ref, o_ref):
    q = q_ref[...]
    k = k_ref[...]
    v = v_ref[...]
    for kvh in range(N_KV_HEADS):
        ks = k[:, kvh * HEAD_DIM:(kvh + 1) * HEAD_DIM]
        vs = v[:, kvh * HEAD_DIM:(kvh + 1) * HEAD_DIM]
        og = _attend(_stack_heads(q, kvh), [(ks, vs, None)], _sink_col(sink_ref, kvh))
        _store_heads(o_ref, og, kvh)


def ctx_attention(P, sink, B, L):
    nb = L // BLOCK
    return pl.pallas_call(
        _ctx_attn_kernel,
        grid=(B, nb),
        in_specs=[pl.BlockSpec(memory_space=pltpu.SMEM),
                  pl.BlockSpec((BLOCK, BRANCH_W), lambda b, j: (b * nb + j, COL_CQ // BRANCH_W)),
                  pl.BlockSpec((L, KV_W), lambda b, j: (b, COL_CK // KV_W)),
                  pl.BlockSpec((L, KV_W), lambda b, j: (b, COL_CV // KV_W))],
        out_specs=pl.BlockSpec((BLOCK, BRANCH_W), lambda b, j: (b * nb + j, 0)),
        out_shape=jax.ShapeDtypeStruct((B * L, BRANCH_W), F32),
        compiler_params=_cparams(2),
        name="ctx_attention",
    )(sink.reshape(N_Q_HEADS), P, P, P)


def _lat_attn_kernel(sink_ref, q_ref, kp_ref, kc_ref, kn_ref, vp_ref, vc_ref, vn_ref, ck_ref, cv_ref,
                     cq_ref, saq_ref, sbq_ref, cp_ref, sap_ref, sbp_ref, cn_ref, san_ref, sbn_ref, o_ref, *, nb):
    j = pl.program_id(1)
    cq, saq, sbq = cq_ref[...], saq_ref[...], sbq_ref[...]
    q = jnp.concatenate([_rope(q_ref[:, c * LANES:(c + 1) * LANES], cq, saq, sbq)
                         for c in range(BRANCH_W // LANES)], axis=1)
    kp = _rope(kp_ref[...], cp_ref[...], sap_ref[...], sbp_ref[...])
    kc = _rope(kc_ref[...], cq, saq, sbq)
    kn = _rope(kn_ref[...], cn_ref[...], san_ref[...], sbn_ref[...])
    R = Q_PER_KV * BLOCK
    r = lax.broadcasted_iota(jnp.int32, (R, BLOCK), 0) % BLOCK
    c = lax.broadcasted_iota(jnp.int32, (R, BLOCK), 1)
    mask_p = (c >= r) & (j > 0)
    mask_n = (c <= r) & (j < nb - 1)
    ck = ck_ref[0, 0]
    cv = cv_ref[0, 0]
    for kvh in range(N_KV_HEADS):
        sl = slice(kvh * HEAD_DIM, (kvh + 1) * HEAD_DIM)
        segs = [(ck[:, sl], cv[:, sl], None),
                (kp[:, sl], vp_ref[:, sl], mask_p),
                (kc[:, sl], vc_ref[:, sl], None),
                (kn[:, sl], vn_ref[:, sl], mask_n)]
        og = _attend(_stack_heads(q, kvh), segs, _sink_col(sink_ref, kvh))
        _store_heads(o_ref, og, kvh)


def lat_attention(P, cache_k, cache_v, layer, sink, rope_tabs, B, L):
    nb = L // BLOCK
    kcol, vcol = COL_CK // KV_W, COL_CV // KV_W
    prev = lambda b, j: b * nb + jnp.maximum(j - 1, 0)
    nxt = lambda b, j: b * nb + jnp.minimum(j + 1, nb - 1)
    kv = lambda rowf, col: pl.BlockSpec((BLOCK, KV_W), lambda b, j: (rowf(b, j), col))
    cur = lambda b, j: b * nb + j
    tab = lambda f: pl.BlockSpec((BLOCK, LANES), f)
    tq = lambda b, j: (j, 0)
    tp = lambda b, j: (jnp.maximum(j - 1, 0), 0)
    tn = lambda b, j: (jnp.minimum(j + 1, nb - 1), 0)
    cos, sina, sinb = rope_tabs
    cache = pl.BlockSpec((1, 1, PAST_LEN, KV_W), lambda b, j: (b, layer, 0, 0))
    return pl.pallas_call(
        functools.partial(_lat_attn_kernel, nb=nb),
        grid=(B, nb),
        in_specs=[pl.BlockSpec(memory_space=pltpu.SMEM),
                  pl.BlockSpec((BLOCK, BRANCH_W), lambda b, j: (b * nb + j, COL_CQ // BRANCH_W)),
                  kv(prev, kcol), kv(cur, kcol), kv(nxt, kcol),
                  kv(prev, vcol), kv(cur, vcol), kv(nxt, vcol),
                  cache, cache,
                  tab(tq), tab(tq), tab(tq), tab(tp), tab(tp), tab(tp), tab(tn), tab(tn), tab(tn)],
        out_specs=pl.BlockSpec((BLOCK, BRANCH_W), lambda b, j: (b * nb + j, 0)),
        out_shape=jax.ShapeDtypeStruct((B * L, BRANCH_W), F32),
        compiler_params=_cparams(2),
        name="lat_attention",
    )(sink.reshape(N_Q_HEADS), P, P, P, P, P, P, P,
      cache_k.reshape(cache_k.shape[0], DEPTH, PAST_LEN, KV_W), cache_v.reshape(cache_v.shape[0], DEPTH, PAST_LEN, KV_W),
      cos, sina, sinb, cos, sina, sinb, cos, sina, sinb)


def rope_tables(L):
    t = jnp.arange(L)
    inv = ROPE_BASE ** (-jnp.arange(ROPE_NF, dtype=F32) / ROPE_NF)
    ang_r = (t // GRID_W).astype(F32)[:, None] * inv
    ang_c = (t % GRID_W).astype(F32)[:, None] * inv
    z = jnp.zeros((L, ROPE_NF), F32)
    cos = jnp.concatenate([jnp.cos(ang_r)] * 2 + [jnp.cos(ang_c)] * 2, axis=1)
    sina = jnp.concatenate([z, jnp.sin(ang_r), z, jnp.sin(ang_c)], axis=1)
    sinb = jnp.concatenate([-jnp.sin(ang_r), z, -jnp.sin(ang_c), z], axis=1)
    return tuple(jnp.tile(a, (1, LANES // HEAD_DIM)) for a in (cos, sina, sinb))


RWKV_CHUNK = 64
RWKV_PREP_CHUNKS = 4


def _split(a):
    hi = a.astype(BF16)
    return hi, (a - hi.astype(F32)).astype(BF16)


def _bdot3(a, b):
    ah, al = _split(a)
    bh, bl = _split(b)
    return jnp.dot(jnp.concatenate([ah, al, ah], axis=1), jnp.concatenate([bh, bh, bl], axis=0),
                   preferred_element_type=F32)


def _head_sum(x, m0, m1):
    s0 = jnp.sum(x * m0, axis=-1, keepdims=True)
    s1 = jnp.sum(x * m1, axis=-1, keepdims=True)
    return s0 * m0 + s1 * m1


def _unit_lower_inverse(ns, eye):
    xs = [eye + n for n in ns]
    pws = list(ns)
    for _ in range(5):
        pws = [_bdot(pw, pw) for pw in pws]
        xs = [x + _bdot(x, pw) for x, pw in zip(xs, pws)]
    resids = [eye - (x - _bdot3(n, x)) for n, x in zip(ns, xs)]
    return [x + _bdot(x, resid) for x, resid in zip(xs, resids)]


def _rwkv_kernel(xr_ref, xk_ref, xv_ref, lo_ref, mu_ref, w0_ref, w2_ref, a0_ref, a2_ref, g2_ref,
                 kk_ref, ka_ref, rk_ref, lnw_ref, lnb_ref, t0_ref,
                 y_ref, tfin_ref,
                 r_scr, kh_scr, v_scr, kn_scr, lw_scr, a_scr, kd_scr, yd_scr,
                 phi_scr, psi_scr, pcm_scr, rys_scr, gys_scr, *, L):
    C = RWKV_CHUNK
    nch = L // C
    lane = lax.broadcasted_iota(jnp.int32, (1, LANES), 1)
    m0 = (lane < RWKV_HEAD).astype(F32)
    m1 = 1.0 - m0

    def shift(x, mu):
        return x + mu * (0.5 * (_shift_rows(x, L, True) + _shift_rows(x, L, False)) - x)

    r = shift(xr_ref[...], mu_ref[0:1, :])
    kh = shift(xk_ref[...], mu_ref[1:2, :])
    vh = shift(xv_ref[...], mu_ref[2:3, :])
    r_scr[...] = r
    kh_scr[...] = kh
    v_scr[...] = vh
    kk = kh * kk_ref[...]
    kn_scr[...] = kk * lax.rsqrt(_head_sum(kk * kk, m0, m1) + 1e-12)
    lo = lo_ref[...]
    tw = jnp.tanh(lo)
    for d in range(2):
        z = w0_ref[d] + _bdot(tw, w2_ref[d])
        lw_scr[d] = -math.exp(-0.5) * _sigmoid(z)
        a = _sigmoid(a0_ref[d] + _bdot(lo, a2_ref[d]))
        a_scr[d] = a
        kd_scr[d] = kh * (1.0 + (a - 1.0) * ka_ref[...])

    ri = lax.broadcasted_iota(jnp.int32, (2 * C, 2 * C), 0)
    ci = lax.broadcasted_iota(jnp.int32, (2 * C, 2 * C), 1)
    same = (ri // C) == (ci // C)
    eye2 = (ri == ci).astype(F32)
    ti = lax.broadcasted_iota(jnp.int32, (C, C), 0)
    si = lax.broadcasted_iota(jnp.int32, (C, C), 1)
    rt, ct = ri % C, ci % C
    strict = (same & (ct < rt), same & (ct > rt))
    incl = (same & (ct <= rt), same & (ct >= rt))
    tri = ((si <= ti).astype(F32), (si >= ti).astype(F32))

    def stack(x):
        return jnp.concatenate([x * m0, x * m1], axis=0)

    def unstack(x):
        return x[0:C] + x[C:2 * C]

    def prepare(jobs):
        J = range(len(jobs))
        ds = [d for _, d in jobs]
        rows = [pl.ds(pl.multiple_of(c * C, C), C) for c, _ in jobs]
        lw = [lw_scr[ds[j], rows[j], :] for j in J]
        cum = [jnp.dot(tri[ds[j]], lw[j], preferred_element_type=F32, precision=HIGHEST) for j in J]
        p_in = [jnp.exp(cum[j]) for j in J]
        p_ex = [jnp.exp(cum[j] - lw[j]) for j in J]
        ip = [jnp.exp(-cum[j]) for j in J]
        pc = [p_in[j][C - 1:C, :] if ds[j] == 0 else p_in[j][0:1, :] for j in J]
        kn = [kn_scr[rows[j], :] for j in J]
        KAP = [stack(kn[j] * p_ex[j]) for j in J]
        RT = [stack(r_scr[rows[j], :] * p_in[j]) for j in J]
        BT = [stack(-(a_scr[ds[j], rows[j], :] * kn[j]) * ip[j]) for j in J]
        KT = [stack(kd_scr[ds[j], rows[j], :] * ip[j]) for j in J]
        V = [stack(v_scr[rows[j], :]) for j in J]
        G = [_bdot_nt(jnp.concatenate([KAP[j], RT[j]], axis=0), jnp.concatenate([BT[j], KT[j]], axis=0))
             for j in J]
        aab = [jnp.where(strict[ds[j]], G[j][0:2 * C, 0:2 * C], 0.0) for j in J]
        aak = [jnp.where(strict[ds[j]], G[j][0:2 * C, 2 * C:4 * C], 0.0) for j in J]
        grb = [jnp.where(incl[ds[j]], G[j][2 * C:4 * C, 0:2 * C], 0.0) for j in J]
        grk = [jnp.where(incl[ds[j]], G[j][2 * C:4 * C, 2 * C:4 * C], 0.0) for j in J]
        AV = [_bdot(aak[j], V[j]) for j in J]
        GV = [_bdot(grk[j], V[j]) for j in J]
        X = _unit_lower_inverse(aab, eye2)
        XK = [_bdot(X[j], KAP[j]) for j in J]
        XAV = [_bdot(X[j], AV[j]) for j in J]
        btp_t = [(BT[j] * pc[j]).T for j in J]
        ktp_t = [(KT[j] * pc[j]).T for j in J]
        phi = [_bdot(btp_t[j], XK[j]) for j in J]
        psi = [_bdot(btp_t[j], XAV[j]) + _bdot(ktp_t[j], V[j]) for j in J]
        rys = [unstack(RT[j] + _bdot(grb[j], XK[j])) for j in J]
        gys = [unstack(_bdot(grb[j], XAV[j]) + GV[j]) for j in J]
        for j, (c, d) in enumerate(jobs):
            phi_scr[d, c] = phi[j].astype(BF16)
            psi_scr[d, c] = psi[j]
            pcm_scr[d, c] = jnp.broadcast_to(pc[j], (LANES, LANES)).T
            rys_scr[d, c] = rys[j].astype(BF16)
            gys_scr[d, c] = gys[j]

    def prep_body(i, carry):
        prepare([(RWKV_PREP_CHUNKS * i + u, d) for u in range(RWKV_PREP_CHUNKS) for d in range(2)])
        return carry

    lax.fori_loop(0, nch // RWKV_PREP_CHUNKS, prep_body, 0)

    def seq_body(i, carry):
        out = []
        for d, T in enumerate(carry):
            c = i if d == 0 else nch - 1 - i
            rows = pl.ds(pl.multiple_of(c * C, C), C)
            tb = T.astype(BF16)
            yd_scr[d, rows, :] = jnp.dot(rys_scr[d, c], tb, preferred_element_type=F32) + gys_scr[d, c]
            out.append(pcm_scr[d, c] * T + jnp.dot(phi_scr[d, c], tb, preferred_element_type=F32) + psi_scr[d, c])
        return tuple(out)

    tf, tb_ = lax.fori_loop(0, nch, seq_body, (t0_ref[0, 0], t0_ref[0, 1]))
    tfin_ref[0, 0] = tf
    tfin_ref[0, 1] = tb_

    y = yd_scr[0] + yd_scr[1]
    inv_n = 1.0 / RWKV_HEAD
    mu = _head_sum(y, m0, m1) * inv_n
    yc = y - mu
    var = _head_sum(yc * yc, m0, m1) * inv_n
    yn = yc * lax.rsqrt(var + RWKV_GN_EPS) * lnw_ref[...] + lnb_ref[...]
    bonus = _head_sum(r * kh * rk_ref[...], m0, m1) * vh
    g = _bdot(_sigmoid(lo), g2_ref[...])
    y_ref[...] = (yn + bonus) * g


def rwkv_mixer(P, prm, t0, B, L):
    HP = BRANCH_W // LANES
    nl = L
    col = lambda c0: (lambda b, h: (b, c0 // LANES + h))
    vecs = lambda n: pl.BlockSpec((n, LANES), lambda b, h: (0, h))
    dvec = pl.BlockSpec((2, 1, LANES), lambda b, h: (0, 0, h))
    lora_w = pl.BlockSpec((2, LORA_W, LANES), lambda b, h: (0, 0, h))
    seq = lambda: pltpu.VMEM((L, LANES), F32)
    seq2 = lambda: pltpu.VMEM((2, L, LANES), F32)
    nch = L // RWKV_CHUNK
    return pl.pallas_call(
        functools.partial(_rwkv_kernel, L=L),
        grid=(B, HP),
        in_specs=[pl.BlockSpec((nl, LANES), col(COL_DR)),
                  pl.BlockSpec((nl, LANES), col(COL_DK)),
                  pl.BlockSpec((nl, LANES), col(COL_DV)),
                  pl.BlockSpec((nl, LORA_W), lambda b, h: (b, COL_LORA // LORA_W)),
                  vecs(3), dvec, lora_w, dvec, lora_w,
                  pl.BlockSpec((LORA_W, LANES), lambda b, h: (0, h)),
                  vecs(1), vecs(1), vecs(1), vecs(1), vecs(1),
                  pl.BlockSpec((1, 2, LANES, LANES), lambda b, h: (b * HP + h, 0, 0, 0))],
        out_specs=[pl.BlockSpec((nl, LANES), lambda b, h: (b, h)),
                   pl.BlockSpec((1, 2, LANES, LANES), lambda b, h: (b * HP + h, 0, 0, 0))],
        out_shape=[jax.ShapeDtypeStruct((B * L, BRANCH_W), F32),
                   jax.ShapeDtypeStruct((B * HP, 2, LANES, LANES), F32)],
        scratch_shapes=[seq(), seq(), seq(), seq(), seq2(), seq2(), seq2(), seq2(),
                        pltpu.VMEM((2, nch, LANES, LANES), BF16), pltpu.VMEM((2, nch, LANES, LANES), F32),
                        pltpu.VMEM((2, nch, LANES, LANES), F32), pltpu.VMEM((2, nch, RWKV_CHUNK, LANES), BF16),
                        pltpu.VMEM((2, nch, RWKV_CHUNK, LANES), F32)],
        compiler_params=_cparams(2),
        name="rwkv_mixer",
    )(P, P, P, P, prm['mu'], prm['w0'], prm['w2'], prm['a0'], prm['a2'], prm['g2'],
      prm['k_k'], prm['k_a'], prm['r_k'], prm['ln_w'], prm['ln_b'], t0)


def _merge_kernel(ag_ref, hs_ref, ys_ref, yc_ref, yd_ref, m0_ref, m1_ref, m2_ref, m3_ref, x_ref, g1_ref,
                  gw_ref, gb_ref, bw_ref, wo_ref, o_ref):
    y_a = _gelu(ag_ref[...]) * hs_ref[...]
    y_s = _gelu(ys_ref[...])
    y_b = y_s * _sigmoid(_bdot(y_s, gw_ref[...]) + gb_ref[...])
    acc = None
    for n, (y, m_ref) in enumerate(((y_a, m0_ref), (y_b, m1_ref), (yc_ref[...], m2_ref), (yd_ref[...], m3_ref))):
        t = _sigmoid(m_ref[...]) * _bdot(y, bw_ref[n])
        acc = t if acc is None else acc + t
    o_ref[...] = x_ref[...] + g1_ref[0] * _bdot(acc, wo_ref[...])


def merge_out(P, hs_tm, ysum_tm, yc, yd, x, g1, glu_w, glu_b, branch_w, w_out, L, bm):
    M, D = x.shape
    W = BRANCH_W
    nb = g1.shape[0]
    assert L % bm == 0
    per = L // bm
    bidx = (lambda i: (i // per, 0, 0)) if nb > 1 else (lambda i: (0, 0, 0))
    row = lambda w, c: pl.BlockSpec((bm, w), lambda i: (i, c))
    tm = pl.BlockSpec((bm, W), lambda i: (i % per, i // per))
    hs, ysum = (a.reshape(L, (M // L) * W) for a in (hs_tm, ysum_tm))
    const = lambda shape: pl.BlockSpec(shape, lambda i: (0,) * len(shape), pipeline_mode=pl.Buffered(1))
    mcol = COL_MERGE // D
    return pl.pallas_call(
        _merge_kernel,
        grid=(M // bm,),
        in_specs=[row(W, COL_AGATE // W), tm, tm, row(W, 0), row(W, 0),
                  row(D, mcol), row(D, mcol + 1), row(D, mcol + 2), row(D, mcol + 3),
                  row(D, 0), pl.BlockSpec((1, 1, D), bidx),
                  const((W, W)), const((1, W)), const((N_BRANCH, W, D)), const((D, D))],
        out_specs=row(D, 0),
        out_shape=jax.ShapeDtypeStruct((M, D), F32),
        compiler_params=_cparams(1),
        name="merge_out",
    )(P, hs, ysum, yc, yd, P, P, P, P, x, g1, glu_w, glu_b.reshape(1, W), branch_w, w_out)


def _shift_rows(g, L, back):
    bm = g.shape[0]
    rolled = pltpu.roll(g, 1 if back else bm - 1, 0)
    r8 = lax.broadcasted_iota(jnp.int32, (SUBLANES, 1), 0)
    pieces = []
    for s in range(0, bm, L):
        if back:
            pieces += [jnp.where(r8 == 0, 0.0, rolled[s:s + SUBLANES]), rolled[s + SUBLANES:s + L]]
        else:
            e = s + L - SUBLANES
            pieces += [rolled[s:e], jnp.where(r8 == SUBLANES - 1, 0.0, rolled[e:s + L])]
    return jnp.concatenate(pieces, axis=0)


def _ffn_out_kernel(gc_ref, val_ref, cw_ref, cb_ref, w_ref, x_ref, g2_ref, gf_ref, o_ref, *, L, final):
    k = pl.program_id(1)
    last = pl.num_programs(1) - 1

    @pl.when(k == 0)
    def _():
        o_ref[...] = jnp.zeros_like(o_ref)

    g = gc_ref[...]
    gate = (_shift_rows(g, L, True) * cw_ref[0:1, :] + g * cw_ref[1:2, :]
            + _shift_rows(g, L, False) * cw_ref[2:3, :] + cb_ref[...])
    half = 0.5 * gate
    act = (half + half * jnp.tanh(half)) * val_ref[...]
    o_ref[...] += jnp.dot(act.astype(BF16), w_ref[0], preferred_element_type=F32)

    @pl.when(k == last)
    def _():
        def rows_epilogue(r, carry):
            rows = pl.ds(pl.multiple_of(r * LANES, LANES), LANES)
            y = x_ref[rows, :] + g2_ref[0] * o_ref[rows, :]
            if final:
                y = y * lax.rsqrt(jnp.mean(y * y, axis=-1, keepdims=True) + EPS) * gf_ref[...]
            o_ref[rows, :] = y
            return carry

        lax.fori_loop(0, o_ref.shape[0] // LANES, rows_epilogue, 0)


def ffn_out(U, x, g2, conv_w, conv_b, w, l, gf, final, L, bm):
    M, D = x.shape
    bk = FFN_BLOCK
    nk = D_FF_PAD // bk
    nb = g2.shape[0]
    assert bm % L == 0 and (nb == 1 or bm == L)
    bidx = (lambda i, k: (i, 0, 0)) if nb > 1 else (lambda i, k: (0, 0, 0))
    return pl.pallas_call(
        functools.partial(_ffn_out_kernel, L=L, final=final),
        grid=(M // bm, nk),
        in_specs=[pl.BlockSpec((bm, bk), lambda i, k: (i, k)),
                  pl.BlockSpec((bm, bk), lambda i, k: (i, nk + k)),
                  pl.BlockSpec((3, bk), lambda i, k: (0, k)),
                  pl.BlockSpec((1, bk), lambda i, k: (0, k)),
                  pl.BlockSpec((1, bk, D), lambda i, k: (l, k, 0)),
                  pl.BlockSpec((bm, D), lambda i, k: (i, 0), pipeline_mode=pl.Buffered(1)),
                  pl.BlockSpec((1, 1, D), bidx),
                  pl.BlockSpec((1, D), lambda i, k: (0, 0))],
        out_specs=pl.BlockSpec((bm, D), lambda i, k: (i, 0)),
        out_shape=jax.ShapeDtypeStruct((M, D), F32),
        compiler_params=_cparams(2),
        name="ffn_out",
    )(U, U, conv_w, conv_b.reshape(1, D_FF_PAD), w, x, g2, gf.reshape(1, D))


def _block_diag(w):
    n, bs, _ = w.shape
    return jnp.einsum('ncd,nm->ncmd', w, jnp.eye(n, dtype=w.dtype)).reshape(n * bs, n * bs)


def _pad_rows(w, start, total):
    k = w.shape[-2]
    pad = [(0, 0)] * (w.ndim - 2) + [(start, total - start - k), (0, 0)]
    return jnp.pad(w, pad)


def prepare_layer(l, w):
    W = BRANCH_W
    p = {}
    p['lru_wa'] = jnp.stack([_block_diag(w['lru_wa'][l, d]) for d in range(2)]).astype(BF16)
    p['lru_wx'] = jnp.stack([_block_diag(w['lru_wx'][l, d]) for d in range(2)]).astype(BF16)
    p['lru_sp'] = jax.nn.softplus(-w['lru_lam'][l])
    p['s5'] = s5_params(w['s5_a_re'][l], w['s5_a_im'][l], w['s5_log_dt'][l], w['s5_b_re'][l], w['s5_b_im'][l],
                        w['s5_c_re'][l], w['s5_c_im'][l])
    p['rwkv'] = dict(
        mu=w['rwkv_mu'][l], w0=w['rwkv_w0'][l].reshape(2, 1, W), a0=w['rwkv_a0'][l].reshape(2, 1, W),
        w2=_pad_rows(w['rwkv_w2'][l], 0, LORA_W).astype(BF16),
        a2=_pad_rows(w['rwkv_a2'][l], W_LORA, LORA_W).astype(BF16),
        g2=_pad_rows(w['rwkv_g2'][l], W_LORA + A_LORA, LORA_W).astype(BF16),
        k_k=w['rwkv_k_k'][l].reshape(1, W), k_a=w['rwkv_k_a'][l].reshape(1, W),
        r_k=w['rwkv_r_k'][l].reshape(1, W), ln_w=w['rwkv_ln_w'][l].reshape(1, W), ln_b=w['rwkv_ln_b'][l].reshape(1, W))
    p['glu_w'] = w['s5_glu_w'][l].astype(BF16)
    p['branch_w'] = w['branch_w'][l].astype(BF16)
    p['w_out'] = w['w_out'][l].astype(BF16)
    fpad = D_FF_PAD - D_FF
    p['ffn_conv_w'] = jnp.pad(w['ffn_conv_w'][l], ((0, 0), (0, fpad)))
    p['ffn_conv_b'] = jnp.pad(w['ffn_conv_b'][l], (0, fpad))
    return p


def _to_tm(a, B, L):
    return a.reshape(B, L, -1).transpose(1, 0, 2).reshape(L * B, -1)


def _to_bm(a, B, L):
    return a.reshape(L, B, -1).transpose(1, 0, 2).reshape(B * L, -1)


def layer_step(x, mod, p, w, l, B, L, ctx, rope_tabs, bm):
    W = BRANCH_W
    D = D_MODEL
    lat = ctx is not None
    nb = mod.shape[0]
    sh1, sc1, g1, sh2, sc2, g2 = (mod[:, i * D:(i + 1) * D].reshape(nb, 1, D) for i in range(6))
    P = norm_mod_matmul(x, w['norm1'][l], sc1, sh1, w['w_in'], l, L, bm, 1024)

    ax = P[:, COL_AX:COL_AX + W].reshape(B, L, W).transpose(1, 0, 2)
    ax = jnp.pad(ax, ((LRU_PAD_L, LRU_CONV_W - 1 - LRU_PAD_L), (0, 0), (0, 0))).reshape((L + 3) * B, W)
    lru0 = ctx['lru'].transpose(1, 0, 2) if lat else jnp.zeros((2, B, W), F32)
    hs_tm, lru_fin = lru_mixer(ax, w['lru_conv_w'][l], w['lru_conv_b'][l], p['lru_wa'], w['lru_ba'][l],
                               p['lru_wx'], w['lru_bx'][l], p['lru_sp'], lru0, B, L)

    gs = S5_NGROUP // S5_SLICES
    if lat:
        s50 = ctx['s5'].reshape(B, 2, 2, S5_SLICES, gs * S5_STATE).transpose(3, 1, 0, 2, 4).reshape(
            S5_SLICES, 2, B, 2 * S5_HALF)
    else:
        s50 = jnp.zeros((S5_SLICES, 2, B, 2 * S5_HALF), F32)
    wb, ar, ai, wc = p['s5']
    ys_tm, s5_fin = s5_mixer(_to_tm(P[:, COL_BU:COL_BU + W], B, L), wb, ar, ai, wc, w['s5_d'][l], s50, B, L)

    sink = w['attn_sink'][l]
    if lat:
        yc = lat_attention(P, ctx['k_all'], ctx['v_all'], l, sink, rope_tabs, B, L)
    else:
        yc = ctx_attention(P, sink, B, L)

    HP = W // LANES
    if lat:
        st = jnp.swapaxes(ctx['wkv'].reshape(B, 2, HP, 2, RWKV_HEAD, RWKV_HEAD), -1, -2)
        z = jnp.zeros_like(st[:, :, :, 0])
        t0 = jnp.concatenate([jnp.concatenate([st[:, :, :, 0], z], axis=-1),
                              jnp.concatenate([z, st[:, :, :, 1]], axis=-1)], axis=-2)
        t0 = t0.transpose(0, 2, 1, 3, 4).reshape(B * HP, 2, LANES, LANES)
    else:
        t0 = jnp.zeros((B * HP, 2, LANES, LANES), F32)
    yd, tfin = rwkv_mixer(P, p['rwkv'], t0, B, L)

    x = merge_out(P, hs_tm, ys_tm, yc, yd, x, g1, p['glu_w'], w['s5_glu_b'][l], p['branch_w'], p['w_out'], L, 256)
    U = norm_mod_matmul(x, w['norm2'][l], sc2, sh2, w['ffn_w_in_b'], l, L, bm, FFN_BLOCK)
    x = ffn_out(U, x, g2, p['ffn_conv_w'], p['ffn_conv_b'], w['ffn_w_out_b'], l, w['norm_final'], l == DEPTH - 1,
                L, bm)

    new = None
    if not lat:
        k_new = P[:, COL_CK:COL_CK + KV_W].reshape(B, L, N_KV_HEADS, HEAD_DIM)
        v_new = P[:, COL_CV:COL_CV + KV_W].reshape(B, L, N_KV_HEADS, HEAD_DIM)
        lru_new = lru_fin.transpose(1, 0, 2)
        s5_new = s5_fin.reshape(S5_SLICES, 2, B, 2, gs, S5_STATE).transpose(2, 1, 3, 0, 4, 5).reshape(
            B, 2, 2, S5_NGROUP, S5_STATE)
        tf = tfin.reshape(B, HP, 2, LANES, LANES)
        heads = jnp.stack([tf[..., :RWKV_HEAD, :RWKV_HEAD], tf[..., RWKV_HEAD:, RWKV_HEAD:]], axis=3)
        wkv_new = jnp.swapaxes(heads, -1, -2).transpose(0, 2, 1, 3, 4, 5).reshape(
            B, 2, RWKV_NH, RWKV_HEAD, RWKV_HEAD)
        new = (k_new, v_new, lru_new, s5_new, wkv_new)
    return x, new


def kernel(x_prompt, x_sample, cache_k, cache_v, state_lru, state_s5, state_wkv, c, c_ctx, mod_w, mod_b, norm1, norm2, norm_final, w_in, lru_conv_w, lru_conv_b, lru_wa, lru_ba, lru_wx, lru_bx, lru_lam, s5_a_re, s5_a_im, s5_log_dt, s5_b_re, s5_b_im, s5_c_re, s5_c_im, s5_d, s5_glu_w, s5_glu_b, attn_sink, rwkv_mu, rwkv_w0, rwkv_w2, rwkv_a0, rwkv_a2, rwkv_g2, rwkv_k_k, rwkv_k_a, rwkv_r_k, rwkv_ln_w, rwkv_ln_b, branch_w, w_out, ffn_w_in, ffn_conv_w, ffn_conv_b, ffn_w_out):
    w = dict(norm1=norm1, norm2=norm2, norm_final=norm_final, w_in=w_in,
             lru_conv_w=lru_conv_w, lru_conv_b=lru_conv_b, lru_wa=lru_wa, lru_ba=lru_ba,
             lru_wx=lru_wx, lru_bx=lru_bx, lru_lam=lru_lam,
             s5_a_re=s5_a_re, s5_a_im=s5_a_im, s5_log_dt=s5_log_dt, s5_b_re=s5_b_re, s5_b_im=s5_b_im,
             s5_c_re=s5_c_re, s5_c_im=s5_c_im, s5_d=s5_d, s5_glu_w=s5_glu_w, s5_glu_b=s5_glu_b,
             attn_sink=attn_sink,
             rwkv_mu=rwkv_mu, rwkv_w0=rwkv_w0, rwkv_w2=rwkv_w2, rwkv_a0=rwkv_a0, rwkv_a2=rwkv_a2,
             rwkv_g2=rwkv_g2, rwkv_k_k=rwkv_k_k, rwkv_k_a=rwkv_k_a, rwkv_r_k=rwkv_r_k,
             rwkv_ln_w=rwkv_ln_w, rwkv_ln_b=rwkv_ln_b,
             branch_w=branch_w, w_out=w_out, ffn_w_in=ffn_w_in, ffn_conv_w=ffn_conv_w,
             ffn_conv_b=ffn_conv_b, ffn_w_out=ffn_w_out)
    Bc, Lc, D = x_prompt.shape
    Bl, Ll, _ = x_sample.shape
    nrow = 16
    c_all = jnp.concatenate([c_ctx[None, :], c, jnp.zeros((nrow - 1 - Bl, D), F32)], axis=0)
    mods = modulation(c_all, mod_w, mod_b)
    layers = [prepare_layer(l, w) for l in range(DEPTH)]
    w['ffn_w_in_b'], w['ffn_w_out_b'] = ffn_weights_bf16(ffn_w_in, ffn_w_out)
    rope_tabs = rope_tables(Ll)

    xp = x_prompt.reshape(Bc * Lc, D)
    news = []
    for l in range(DEPTH):
        xp, new = layer_step(xp, mods[l, 0:1], layers[l], w, l, Bc, Lc, None, None, 1024)
        news.append(new)
    y_prompt = xp.reshape(Bc, Lc, D)

    xs = x_sample.reshape(Bl * Ll, D)
    for l in range(DEPTH):
        ctx = dict(k_all=cache_k, v_all=cache_v, lru=state_lru[:, l], s5=state_s5[:, l], wkv=state_wkv[:, l])
        xs, _ = layer_step(xs, mods[l, 1:1 + Bl], layers[l], w, l, Bl, Ll, ctx, rope_tabs, min(1024, Ll))
    y_sample = xs.reshape(Bl, Ll, D)

    stack = lambda i: jnp.stack([n[i] for n in news], axis=1)
    return (y_prompt, y_sample, stack(0), stack(1), stack(2), stack(3), stack(4))
```

```python
import functools
import math

import jax
import jax.numpy as jnp
from jax import lax
from jax.experimental import pallas as pl
from jax.experimental.pallas import tpu as pltpu

F32 = jnp.float32
BF16 = jnp.bfloat16
HIGHEST = lax.Precision.HIGHEST

D_MODEL = 2048
DEPTH = 2
PAST_LEN = 256
GRID_W = 64
N_BRANCH = 4
BRANCH_W = 512
EPS = 1e-6
LRU_BLOCKS = 8
LRU_BS = 64
LRU_CONV_W = 4
LRU_PAD_L = 2
LRU_C = 8.0
S5_GROUP = 16
S5_NGROUP = 32
S5_STATE = 64
HEAD_DIM = 64
N_Q_HEADS = 8
N_KV_HEADS = 2
Q_PER_KV = 4
KV_W = 128
WINDOW = 128
BLOCK = 128
ROPE_BASE = 10000.0
ROPE_NF = 16
ATTN_SCALE = 1.0 / math.sqrt(HEAD_DIM)
NEG_INF = -1e30
RWKV_HEAD = 64
RWKV_NH = 8
W_LORA = 64
A_LORA = 64
G_LORA = 128
RWKV_GN_EPS = 64e-5
D_FF = 5504
FFN_BLOCK = 512
D_FF_PAD = 5632
FFN_PIECES = 4
FFN_TAIL = 3
FFN_ROW_CHUNK = 256
LANES = 128
SUBLANES = 8
VMEM_LIMIT = 56 * 2 ** 20

COL_AX, COL_AGATE, COL_BU, COL_CQ = 0, 512, 1024, 1536
COL_CK, COL_CV = 2048, 2176
COL_DR, COL_DK, COL_DV, COL_LORA = 2304, 2816, 3328, 3840
COL_MERGE = 4096
D_IN = COL_MERGE + N_BRANCH * D_MODEL
LORA_W = W_LORA + A_LORA + G_LORA


def _cparams(n_grid):
    return pltpu.CompilerParams(dimension_semantics=("arbitrary",) * n_grid, vmem_limit_bytes=VMEM_LIMIT)


def _bdot(a, b):
    return jnp.dot(a.astype(BF16), b.astype(BF16), preferred_element_type=F32)


def _bdot_nt(a, b):
    return lax.dot_general(a.astype(BF16), b.astype(BF16), (((1,), (1,)), ((), ())),
                           preferred_element_type=F32)


def _gelu(x):
    return jax.nn.gelu(x)


def _sigmoid(x):
    return 0.5 * jnp.tanh(0.5 * x) + 0.5


def _mod_kernel(c_ref, w_ref, b_ref, o_ref):
    c = c_ref[...]
    sh, sl = _split(c * _sigmoid(c))
    wh, wl = _split(w_ref[0])
    dot = functools.partial(jnp.dot, preferred_element_type=F32)
    o_ref[0] = dot(sh, wh) + dot(sl, wh) + dot(sh, wl) + b_ref[0]


def modulation(c_all, mod_w, mod_b):
    R, D = c_all.shape
    N = mod_w.shape[2]
    bn = 1024
    return pl.pallas_call(
        _mod_kernel,
        grid=(DEPTH, N // bn),
        in_specs=[pl.BlockSpec((R, D), lambda l, j: (0, 0)),
                  pl.BlockSpec((1, D, bn), lambda l, j: (l, 0, j)),
                  pl.BlockSpec((1, 1, bn), lambda l, j: (l, 0, j))],
        out_specs=pl.BlockSpec((1, R, bn), lambda l, j: (l, 0, j)),
        out_shape=jax.ShapeDtypeStruct((DEPTH, R, N), F32),
        compiler_params=_cparams(2),
        name="modulation",
    )(c_all, mod_w, mod_b.reshape(DEPTH, 1, N))


def _nmm_kernel(x_ref, g_ref, sc_ref, sh_ref, w_ref, o_ref, h_scr):
    @pl.when(pl.program_id(1) == 0)
    def _():
        x = x_ref[...]
        y = x * lax.rsqrt(jnp.mean(x * x, axis=-1, keepdims=True) + EPS) * g_ref[...]
        h_scr[...] = (y * (1.0 + sc_ref[0]) + sh_ref[0]).astype(BF16)

    o_ref[...] = jnp.dot(h_scr[...], w_ref[0].astype(BF16), preferred_element_type=F32)


def norm_mod_matmul(x, g, sc, sh, w, l, L, bm, bn):
    M, D = x.shape
    N = w.shape[2]
    nb = sc.shape[0]
    if nb > 1:
        assert L % bm == 0
        per = L // bm
        bidx = lambda i, j: (i // per, 0, 0)
    else:
        bidx = lambda i, j: (0, 0, 0)
    return pl.pallas_call(
        _nmm_kernel,
        grid=(M // bm, N // bn),
        in_specs=[pl.BlockSpec((bm, D), lambda i, j: (i, 0)),
                  pl.BlockSpec((1, D), lambda i, j: (0, 0)),
                  pl.BlockSpec((1, 1, D), bidx),
                  pl.BlockSpec((1, 1, D), bidx),
                  pl.BlockSpec((1, D, bn), lambda i, j: (l, 0, j))],
        out_specs=pl.BlockSpec((bm, bn), lambda i, j: (i, j)),
        out_shape=jax.ShapeDtypeStruct((M, N), F32),
        scratch_shapes=[pltpu.VMEM((bm, D), BF16)],
        compiler_params=_cparams(2),
        name="norm_mod_matmul",
    )(x, g.reshape(1, D), sc, sh, w)


def _ffn_cast_kernel(w0_ref, w1_ref, w2_ref, w3_ref, o_ref, *, axis, ntile):
    t = pl.program_id(1)
    for q, w_ref in enumerate((w0_ref, w1_ref, w2_ref, w3_ref)):
        piece = w_ref[0].astype(BF16)
        if axis == 1:
            o_ref[0, :, q * LANES:(q + 1) * LANES] = piece
        else:
            o_ref[0, q * LANES:(q + 1) * LANES, :] = piece

    @pl.when(t % ntile == ntile - 1)
    def _():
        pad = FFN_BLOCK - FFN_TAIL * LANES
        if axis == 1:
            o_ref[0, :, FFN_TAIL * LANES:] = jnp.zeros((o_ref.shape[1], pad), BF16)
        else:
            o_ref[0, FFN_TAIL * LANES:, :] = jnp.zeros((pad, o_ref.shape[2]), BF16)


def ffn_weights_bf16(w_in, w_out):
    nl, D, _ = w_in.shape
    ntile = D_FF_PAD // FFN_BLOCK
    half_pieces = D_FF // LANES

    def col_piece(q):
        last = 2 * half_pieces - 1
        return pl.BlockSpec((1, D, LANES), lambda l, j: (
            l, 0, jnp.minimum((j // ntile) * half_pieces + (j % ntile) * FFN_PIECES + q, last)))

    def row_piece(q):
        return pl.BlockSpec((1, LANES, D), lambda l, k: (l, jnp.minimum(k * FFN_PIECES + q, half_pieces - 1), 0))

    w_in_b = pl.pallas_call(
        functools.partial(_ffn_cast_kernel, axis=1, ntile=ntile),
        grid=(nl, 2 * ntile),
        in_specs=[col_piece(q) for q in range(FFN_PIECES)],
        out_specs=pl.BlockSpec((1, D, FFN_BLOCK), lambda l, j: (l, 0, j)),
        out_shape=jax.ShapeDtypeStruct((nl, D, 2 * D_FF_PAD), BF16),
        compiler_params=_cparams(2),
        name="ffn_w_in_cast",
    )(w_in, w_in, w_in, w_in)
    w_out_b = pl.pallas_call(
        functools.partial(_ffn_cast_kernel, axis=0, ntile=ntile),
        grid=(nl, ntile),
        in_specs=[row_piece(q) for q in range(FFN_PIECES)],
        out_specs=pl.BlockSpec((1, FFN_BLOCK, D), lambda l, k: (l, k, 0)),
        out_shape=jax.ShapeDtypeStruct((nl, D_FF_PAD, D), BF16),
        compiler_params=_cparams(2),
        name="ffn_w_out_cast",
    )(w_out, w_out, w_out, w_out)
    return w_in_b, w_out_b


def _lru_kernel(xp_ref, cw_ref, cb_ref, wa_ref, ba_ref, wx_ref, bx_ref, sp_ref, h0_ref,
                out_ref, fin_ref, a_scr, b_scr, *, B, L, Tc):
    nch = L // Tc
    R = Tc * B
    out_ref[...] = jnp.zeros_like(out_ref)

    def chunk(ci, hs):
        bases = (pl.multiple_of(ci * R, R), pl.multiple_of((nch - 1 - ci) * R, R))
        for d in range(2):
            base = bases[d]
            xa = cb_ref[...] + xp_ref[pl.ds(base, R), :] * cw_ref[0:1, :]
            for j in range(1, LRU_CONV_W):
                xa = xa + xp_ref[pl.ds(base + j * B, R), :] * cw_ref[j:j + 1, :]
            r = _sigmoid(_bdot(xa, wa_ref[d]) + ba_ref[d])
            i = _sigmoid(_bdot(xa, wx_ref[d]) + bx_ref[d])
            a = jnp.exp(-LRU_C * r * sp_ref[d])
            a_scr[d] = a
            b_scr[d] = jnp.sqrt(1.0 - a * a) * (i * xa)

        def step(ti, hs):
            new = []
            for d in range(2):
                off = pl.multiple_of((ti if d == 0 else Tc - 1 - ti) * B, B)
                h = a_scr[d, pl.ds(off, B), :] * hs[d] + b_scr[d, pl.ds(off, B), :]
                row = pl.ds(bases[d] + off, B)
                out_ref[row, :] = out_ref[row, :] + h
                new.append(h)
            return tuple(new)

        return lax.fori_loop(0, Tc, step, hs, unroll=8)

    hf, hb = lax.fori_loop(0, nch, chunk, (h0_ref[0], h0_ref[1]))
    fin_ref[0] = hf
    fin_ref[1] = hb


def lru_mixer(xp_tm, cw, cb, wa, ba, wx, bx, sp, h0, B, L):
    W = BRANCH_W
    Tc = min(128, L // 2)
    kern = functools.partial(_lru_kernel, B=B, L=L, Tc=Tc)
    vec = lambda: pl.BlockSpec((2, 1, LANES), lambda s: (0, 0, s))
    return pl.pallas_call(
        kern,
        grid=(W // LANES,),
        in_specs=[pl.BlockSpec(((L + 3) * B, LANES), lambda s: (0, s)),
                  pl.BlockSpec((LRU_CONV_W, LANES), lambda s: (0, s)),
                  pl.BlockSpec((1, LANES), lambda s: (0, s)),
                  pl.BlockSpec((2, LANES, LANES), lambda s: (0, s, s)), vec(),
                  pl.BlockSpec((2, LANES, LANES), lambda s: (0, s, s)), vec(),
                  vec(),
                  pl.BlockSpec((2, B, LANES), lambda s: (0, 0, s))],
        out_specs=[pl.BlockSpec((L * B, LANES), lambda s: (0, s)),
                   pl.BlockSpec((2, B, LANES), lambda s: (0, 0, s))],
        out_shape=[jax.ShapeDtypeStruct((L * B, W), F32), jax.ShapeDtypeStruct((2, B, W), F32)],
        scratch_shapes=[pltpu.VMEM((2, Tc * B, LANES), F32), pltpu.VMEM((2, Tc * B, LANES), F32)],
        compiler_params=_cparams(1),
        name="lru_mixer",
    )(xp_tm, cw, cb.reshape(1, W), wa, ba.reshape(2, 1, W), wx, bx.reshape(2, 1, W), sp.reshape(2, 1, W), h0)


S5_SLICES = BRANCH_W // LANES
S5_HALF = (S5_NGROUP // S5_SLICES) * S5_STATE


def _s5_kernel(u_ref, wb_ref, ar_ref, ai_ref, wc_ref, dv_ref, h0_ref, out_ref, fin_ref, hs_scr, *, B, L, Tc):
    nch = L // Tc
    R = Tc * B
    H = S5_HALF
    out_ref[...] = u_ref[...] * dv_ref[...]
    ab = [(ar_ref[0, d], ai_ref[0, d]) for d in range(2)]

    def chunk(ci, carry):
        bases = (pl.multiple_of(ci * R, R), pl.multiple_of((nch - 1 - ci) * R, R))
        for d in range(2):
            hs_scr[d] = _bdot(u_ref[pl.ds(bases[d], R), :], wb_ref[0, d])

        def step(ti, carry):
            new = []
            for d in range(2):
                hr, hi = carry[d]
                ar, ai = ab[d]
                row = pl.ds(pl.multiple_of((ti if d == 0 else Tc - 1 - ti) * B, B), B)
                nr = ar * hr - ai * hi + hs_scr[d, row, 0:H]
                ni = ar * hi + ai * hr + hs_scr[d, row, H:2 * H]
                hs_scr[d, row, 0:H] = nr
                hs_scr[d, row, H:2 * H] = ni
                new.append((nr, ni))
            return tuple(new)

        carry = lax.fori_loop(0, Tc, step, carry, unroll=2)
        for d in range(2):
            rows = pl.ds(bases[d], R)
            out_ref[rows, :] = out_ref[rows, :] + _bdot(hs_scr[d], wc_ref[0, d])
        return carry

    init = tuple((h0_ref[0, d, :, 0:H], h0_ref[0, d, :, H:2 * H]) for d in range(2))
    fin = lax.fori_loop(0, nch, chunk, init)
    for d in range(2):
        fin_ref[0, d, :, 0:H] = fin[d][0]
        fin_ref[0, d, :, H:2 * H] = fin[d][1]


def s5_mixer(u_tm, wb, ar, ai, wc, dvec, h0, B, L):
    W = BRANCH_W
    Tc = min(128, L // 2, 1024 // B)
    H2 = 2 * S5_HALF
    kern = functools.partial(_s5_kernel, B=B, L=L, Tc=Tc)
    return pl.pallas_call(
        kern,
        grid=(S5_SLICES,),
        in_specs=[pl.BlockSpec((L * B, LANES), lambda s: (0, s)),
                  pl.BlockSpec((1, 2, LANES, H2), lambda s: (s, 0, 0, 0)),
                  pl.BlockSpec((1, 2, 1, S5_HALF), lambda s: (s, 0, 0, 0)),
                  pl.BlockSpec((1, 2, 1, S5_HALF), lambda s: (s, 0, 0, 0)),
                  pl.BlockSpec((1, 2, H2, LANES), lambda s: (s, 0, 0, 0)),
                  pl.BlockSpec((1, LANES), lambda s: (0, s)),
                  pl.BlockSpec((1, 2, B, H2), lambda s: (s, 0, 0, 0))],
        out_specs=[pl.BlockSpec((L * B, LANES), lambda s: (0, s)),
                   pl.BlockSpec((1, 2, B, H2), lambda s: (s, 0, 0, 0))],
        out_shape=[jax.ShapeDtypeStruct((L * B, W), F32), jax.ShapeDtypeStruct((S5_SLICES, 2, B, H2), F32)],
        scratch_shapes=[pltpu.VMEM((2, Tc * B, H2), F32)],
        compiler_params=_cparams(1),
        name="s5_mixer",
    )(u_tm, wb, ar, ai, wc, dvec.reshape(1, W), h0)


def s5_params(a_re, a_im, log_dt, b_re, b_im, c_re, c_im):
    dt = jnp.exp(log_dt)[..., None]
    mag = jnp.exp(a_re * dt)
    abar_r = mag * jnp.cos(a_im * dt)
    abar_i = mag * jnp.sin(a_im * dt)
    den = a_re * a_re + a_im * a_im
    nr = abar_r - 1.0
    fr = (nr * a_re + abar_i * a_im) / den
    fi = (abar_i * a_re - nr * a_im) / den
    bbar_r = fr[..., None] * b_re - fi[..., None] * b_im
    bbar_i = fr[..., None] * b_im + fi[..., None] * b_re
    gs = S5_NGROUP // S5_SLICES
    eye = jnp.eye(gs, dtype=F32)
    bb = jnp.stack([bbar_r, bbar_i], axis=0).reshape(2, 2, S5_SLICES, gs, S5_STATE, S5_GROUP)
    wb = jnp.einsum('rdsgnp,gh->sdgprhn', bb, eye).reshape(S5_SLICES, 2, LANES, 2 * S5_HALF)
    cc = jnp.stack([c_re, -c_im], axis=0).reshape(2, 2, S5_SLICES, gs, S5_GROUP, S5_STATE)
    wc = jnp.einsum('rdsgpn,gh->sdrhngp', cc, eye).reshape(S5_SLICES, 2, 2 * S5_HALF, LANES)
    ar = abar_r.reshape(2, S5_SLICES, 1, S5_HALF).transpose(1, 0, 2, 3)
    ai = abar_i.reshape(2, S5_SLICES, 1, S5_HALF).transpose(1, 0, 2, 3)
    return wb.astype(BF16), ar, ai, wc.astype(BF16)


def _rope(x, cos, sina, sinb):
    return x * cos + pltpu.roll(x, ROPE_NF, 1) * sina + pltpu.roll(x, LANES - ROPE_NF, 1) * sinb


def _attend(qg, segs, sink_col):
    def lane_tiles(a):
        return [a[:, c * LANES:(c + 1) * LANES] for c in range(a.shape[1] // LANES)]

    def tree(op, xs):
        while len(xs) > 1:
            xs = [op(xs[i], xs[i + 1]) if i + 1 < len(xs) else xs[i] for i in range(0, len(xs), 2)]
        return xs[0]

    scores = []
    for k, _, mask in segs:
        s = _bdot_nt(qg, k) * ATTN_SCALE
        if mask is not None:
            s = jnp.where(mask, s, NEG_INF)
        scores.append(s)
    m = jnp.maximum(sink_col, jnp.max(tree(jnp.maximum, [t for s in scores for t in lane_tiles(s)]),
                                      axis=-1, keepdims=True))
    ps = [jnp.exp(s - m) for s in scores]
    den = jnp.exp(sink_col - m) + jnp.sum(tree(jnp.add, [t for p in ps for t in lane_tiles(p)]),
                                          axis=-1, keepdims=True)
    o = tree(jnp.add, [_bdot(p, v) for p, (_, v, _) in zip(ps, segs)])
    return o * (1.0 / den)


def _sink_col(sink_ref, kvh):
    return jnp.concatenate([jnp.full((BLOCK, 1), sink_ref[kvh * Q_PER_KV + g], F32) for g in range(Q_PER_KV)], axis=0)


def _stack_heads(q, kvh):
    return jnp.concatenate([q[:, (kvh * Q_PER_KV + g) * HEAD_DIM:(kvh * Q_PER_KV + g + 1) * HEAD_DIM]
                            for g in range(Q_PER_KV)], axis=0)


def _store_heads(o_ref, og, kvh):
    for g in range(Q_PER_KV):
        h = kvh * Q_PER_KV + g
        o_ref[:, h * HEAD_DIM:(h + 1) * HEAD_DIM] = og[g * BLOCK:(g + 1) * BLOCK]


def _ctx_attn_kernel(sink_ref, q_ref, k_ref, v_ref, o_ref):
    q = q_ref[...]
    k = k_ref[...]
    v = v_ref[...]
    for kvh in range(N_KV_HEADS):
        ks = k[:, kvh * HEAD_DIM:(kvh + 1) * HEAD_DIM]
        vs = v[:, kvh * HEAD_DIM:(kvh + 1) * HEAD_DIM]
        og = _attend(_stack_heads(q, kvh), [(ks, vs, None)], _sink_col(sink_ref, kvh))
        _store_heads(o_ref, og, kvh)


def ctx_attention(P, sink, B, L):
    nb = L // BLOCK
    return pl.pallas_call(
        _ctx_attn_kernel,
        grid=(B, nb),
        in_specs=[pl.BlockSpec(memory_space=pltpu.SMEM),
                  pl.BlockSpec((BLOCK, BRANCH_W), lambda b, j: (b * nb + j, COL_CQ // BRANCH_W)),
                  pl.BlockSpec((L, KV_W), lambda b, j: (b, COL_CK // KV_W)),
                  pl.BlockSpec((L, KV_W), lambda b, j: (b, COL_CV // KV_W))],
        out_specs=pl.BlockSpec((BLOCK, BRANCH_W), lambda b, j: (b * nb + j, 0)),
        out_shape=jax.ShapeDtypeStruct((B * L, BRANCH_W), F32),
        compiler_params=_cparams(2),
        name="ctx_attention",
    )(sink.reshape(N_Q_HEADS), P, P, P)


def _lat_attn_kernel(sink_ref, q_ref, kp_ref, kc_ref, kn_ref, vp_ref, vc_ref, vn_ref, ck_ref, cv_ref,
                     cq_ref, saq_ref, sbq_ref, cp_ref, sap_ref, sbp_ref, cn_ref, san_ref, sbn_ref, o_ref, *, nb):
    j = pl.program_id(1)
    cq, saq, sbq = cq_ref[...], saq_ref[...], sbq_ref[...]
    q = jnp.concatenate([_rope(q_ref[:, c * LANES:(c + 1) * LANES], cq, saq, sbq)
                         for c in range(BRANCH_W // LANES)], axis=1)
    kp = _rope(kp_ref[...], cp_ref[...], sap_ref[...], sbp_ref[...])
    kc = _rope(kc_ref[...], cq, saq, sbq)
    kn = _rope(kn_ref[...], cn_ref[...], san_ref[...], sbn_ref[...])
    R = Q_PER_KV * BLOCK
    r = lax.broadcasted_iota(jnp.int32, (R, BLOCK), 0) % BLOCK
    c = lax.broadcasted_iota(jnp.int32, (R, BLOCK), 1)
    mask_p = (c >= r) & (j > 0)
    mask_n = (c <= r) & (j < nb - 1)
    ck = ck_ref[0, 0]
    cv = cv_ref[0, 0]
    for kvh in range(N_KV_HEADS):
        sl = slice(kvh * HEAD_DIM, (kvh + 1) * HEAD_DIM)
        segs = [(ck[:, sl], cv[:, sl], None),
                (kp[:, sl], vp_ref[:, sl], mask_p),
                (kc[:, sl], vc_ref[:, sl], None),
                (kn[:, sl], vn_ref[:, sl], mask_n)]
        og = _attend(_stack_heads(q, kvh), segs, _sink_col(sink_ref, kvh))
        _store_heads(o_ref, og, kvh)


def lat_attention(P, cache_k, cache_v, layer, sink, rope_tabs, B, L):
    nb = L // BLOCK
    kcol, vcol = COL_CK // KV_W, COL_CV // KV_W
    prev = lambda b, j: b * nb + jnp.maximum(j - 1, 0)
    nxt = lambda b, j: b * nb + jnp.minimum(j + 1, nb - 1)
    kv = lambda rowf, col: pl.BlockSpec((BLOCK, KV_W), lambda b, j: (rowf(b, j), col))
    cur = lambda b, j: b * nb + j
    tab = lambda f: pl.BlockSpec((BLOCK, LANES), f)
    tq = lambda b, j: (j, 0)
    tp = lambda b, j: (jnp.maximum(j - 1, 0), 0)
    tn = lambda b, j: (jnp.minimum(j + 1, nb - 1), 0)
    cos, sina, sinb = rope_tabs
    cache = pl.BlockSpec((1, 1, PAST_LEN, KV_W), lambda b, j: (b, layer, 0, 0))
    return pl.pallas_call(
        functools.partial(_lat_attn_kernel, nb=nb),
        grid=(B, nb),
        in_specs=[pl.BlockSpec(memory_space=pltpu.SMEM),
                  pl.BlockSpec((BLOCK, BRANCH_W), lambda b, j: (b * nb + j, COL_CQ // BRANCH_W)),
                  kv(prev, kcol), kv(cur, kcol), kv(nxt, kcol),
                  kv(prev, vcol), kv(cur, vcol), kv(nxt, vcol),
                  cache, cache,
                  tab(tq), tab(tq), tab(tq), tab(tp), tab(tp), tab(tp), tab(tn), tab(tn), tab(tn)],
        out_specs=pl.BlockSpec((BLOCK, BRANCH_W), lambda b, j: (b * nb + j, 0)),
        out_shape=jax.ShapeDtypeStruct((B * L, BRANCH_W), F32),
        compiler_params=_cparams(2),
        name="lat_attention",
    )(sink.reshape(N_Q_HEADS), P, P, P, P, P, P, P,
      cache_k.reshape(cache_k.shape[0], DEPTH, PAST_LEN, KV_W), cache_v.reshape(cache_v.shape[0], DEPTH, PAST_LEN, KV_W),
      cos, sina, sinb, cos, sina, sinb, cos, sina, sinb)


def rope_tables(L):
    t = jnp.arange(L)
    inv = ROPE_BASE ** (-jnp.arange(ROPE_NF, dtype=F32) / ROPE_NF)
    ang_r = (t // GRID_W).astype(F32)[:, None] * inv
    ang_c = (t % GRID_W).astype(F32)[:, None] * inv
    z = jnp.zeros((L, ROPE_NF), F32)
    cos = jnp.concatenate([jnp.cos(ang_r)] * 2 + [jnp.cos(ang_c)] * 2, axis=1)
    sina = jnp.concatenate([z, jnp.sin(ang_r), z, jnp.sin(ang_c)], axis=1)
    sinb = jnp.concatenate([-jnp.sin(ang_r), z, -jnp.sin(ang_c), z], axis=1)
    return tuple(jnp.tile(a, (1, LANES // HEAD_DIM)) for a in (cos, sina, sinb))


RWKV_CHUNK = 64
RWKV_PREP_CHUNKS = 4


def _split(a):
    hi = a.astype(BF16)
    return hi, (a - hi.astype(F32)).astype(BF16)


def _bdot3(a, b):
    ah, al = _split(a)
    bh, bl = _split(b)
    return jnp.dot(jnp.concatenate([ah, al, ah], axis=1), jnp.concatenate([bh, bh, bl], axis=0),
                   preferred_element_type=F32)


def _head_sum(x, head_ones):
    return _bdot3(x, head_ones)


def _unit_lower_inverse(ns, eye):
    xs = [eye + n for n in ns]
    pws = list(ns)
    for _ in range(5):
        pws = [_bdot(pw, pw) for pw in pws]
        xs = [x + _bdot(x, pw) for x, pw in zip(xs, pws)]
    resids = [eye - (x - _bdot3(n, x)) for n, x in zip(ns, xs)]
    return [x + _bdot(x, resid) for x, resid in zip(xs, resids)]


def _rwkv_kernel(xr_ref, xk_ref, xv_ref, lo_ref, mu_ref, w0_ref, w2_ref, a0_ref, a2_ref, g2_ref,
                 kk_ref, ka_ref, rk_ref, lnw_ref, lnb_ref, t0_ref,
                 y_ref, tfin_ref,
                 r_scr, kh_scr, v_scr, kn_scr, lw_scr, a_scr, kd_scr, yd_scr,
                 phi_scr, psi_scr, pcm_scr, rys_scr, gys_scr, *, L):
    C = RWKV_CHUNK
    nch = L // C
    lane = lax.broadcasted_iota(jnp.int32, (1, LANES), 1)
    m0 = (lane < RWKV_HEAD).astype(F32)
    m1 = 1.0 - m0
    head_ones = (lax.broadcasted_iota(jnp.int32, (LANES, LANES), 0) // RWKV_HEAD
                 == lax.broadcasted_iota(jnp.int32, (LANES, LANES), 1) // RWKV_HEAD).astype(F32)

    def shift(x, mu):
        return x + mu * (0.5 * (_shift_rows(x, L, True) + _shift_rows(x, L, False)) - x)

    r = shift(xr_ref[...], mu_ref[0:1, :])
    kh = shift(xk_ref[...], mu_ref[1:2, :])
    vh = shift(xv_ref[...], mu_ref[2:3, :])
    r_scr[...] = r
    kh_scr[...] = kh
    v_scr[...] = vh
    kk = kh * kk_ref[...]
    kn_scr[...] = kk * lax.rsqrt(_head_sum(kk * kk, head_ones) + 1e-12)
    lo = lo_ref[...]
    tw = jnp.tanh(lo)
    for d in range(2):
        z = w0_ref[d] + _bdot(tw, w2_ref[d])
        lw_scr[d] = -math.exp(-0.5) * _sigmoid(z)
        a = _sigmoid(a0_ref[d] + _bdot(lo, a2_ref[d]))
        a_scr[d] = a
        kd_scr[d] = kh * (1.0 + (a - 1.0) * ka_ref[...])

    ri = lax.broadcasted_iota(jnp.int32, (2 * C, 2 * C), 0)
    ci = lax.broadcasted_iota(jnp.int32, (2 * C, 2 * C), 1)
    same = (ri // C) == (ci // C)
    eye2 = (ri == ci).astype(F32)
    ti = lax.broadcasted_iota(jnp.int32, (C, C), 0)
    si = lax.broadcasted_iota(jnp.int32, (C, C), 1)
    rt, ct = ri % C, ci % C
    strict = (same & (ct < rt), same & (ct > rt))
    incl = (same & (ct <= rt), same & (ct >= rt))
    tri = ((si <= ti).astype(F32), (si >= ti).astype(F32))

    def stack(x):
        return jnp.concatenate([x * m0, x * m1], axis=0)

    def unstack(x):
        return x[0:C] + x[C:2 * C]

    def prepare(jobs):
        J = range(len(jobs))
        ds = [d for _, d in jobs]
        rows = [pl.ds(pl.multiple_of(c * C, C), C) for c, _ in jobs]
        lw = [lw_scr[ds[j], rows[j], :] for j in J]
        cum = [jnp.dot(tri[ds[j]], lw[j], preferred_element_type=F32, precision=HIGHEST) for j in J]
        p_in = [jnp.exp(cum[j]) for j in J]
        p_ex = [jnp.exp(cum[j] - lw[j]) for j in J]
        ip = [jnp.exp(-cum[j]) for j in J]
        pc = [p_in[j][C - 1:C, :] if ds[j] == 0 else p_in[j][0:1, :] for j in J]
        kn = [kn_scr[rows[j], :] for j in J]
        KAP = [stack(kn[j] * p_ex[j]) for j in J]
        RT = [stack(r_scr[rows[j], :] * p_in[j]) for j in J]
        BT = [stack(-(a_scr[ds[j], rows[j], :] * kn[j]) * ip[j]) for j in J]
        KT = [stack(kd_scr[ds[j], rows[j], :] * ip[j]) for j in J]
        V = [stack(v_scr[rows[j], :]) for j in J]
        G = [_bdot_nt(jnp.concatenate([KAP[j], RT[j]], axis=0), jnp.concatenate([BT[j], KT[j]], axis=0))
             for j in J]
        aab = [jnp.where(strict[ds[j]], G[j][0:2 * C, 0:2 * C], 0.0) for j in J]
        aak = [jnp.where(strict[ds[j]], G[j][0:2 * C, 2 * C:4 * C], 0.0) for j in J]
        grb = [jnp.where(incl[ds[j]], G[j][2 * C:4 * C, 0:2 * C], 0.0) for j in J]
        grk = [jnp.where(incl[ds[j]], G[j][2 * C:4 * C, 2 * C:4 * C], 0.0) for j in J]
        AV = [_bdot(aak[j], V[j]) for j in J]
        GV = [_bdot(grk[j], V[j]) for j in J]
        X = _unit_lower_inverse(aab, eye2)
        XK = [_bdot(X[j], KAP[j]) for j in J]
        XAV = [_bdot(X[j], AV[j]) for j in J]
        btp_t = [(BT[j] * pc[j]).T for j in J]
        ktp_t = [(KT[j] * pc[j]).T for j in J]
        phi = [_bdot(btp_t[j], XK[j]) for j in J]
        psi = [_bdot(btp_t[j], XAV[j]) + _bdot(ktp_t[j], V[j]) for j in J]
        rys = [unstack(RT[j] + _bdot(grb[j], XK[j])) for j in J]
        gys = [unstack(_bdot(grb[j], XAV[j]) + GV[j]) for j in J]
        for j, (c, d) in enumerate(jobs):
            phi_scr[d, c] = phi[j].astype(BF16)
            psi_scr[d, c] = psi[j]
            pcm_scr[d, c] = jnp.broadcast_to(pc[j], (LANES, LANES)).T
            rys_scr[d, c] = rys[j].astype(BF16)
            gys_scr[d, c] = gys[j]

    def prep_body(i, carry):
        prepare([(RWKV_PREP_CHUNKS * i + u, d) for u in range(RWKV_PREP_CHUNKS) for d in range(2)])
        return carry

    lax.fori_loop(0, nch // RWKV_PREP_CHUNKS, prep_body, 0)

    def seq_body(i, carry):
        out = []
        for d, T in enumerate(carry):
            c = i if d == 0 else nch - 1 - i
            rows = pl.ds(pl.multiple_of(c * C, C), C)
            tb = T.astype(BF16)
            yd_scr[d, rows, :] = jnp.dot(rys_scr[d, c], tb, preferred_element_type=F32) + gys_scr[d, c]
            out.append(pcm_scr[d, c] * T + jnp.dot(phi_scr[d, c], tb, preferred_element_type=F32) + psi_scr[d, c])
        return tuple(out)

    tf, tb_ = lax.fori_loop(0, nch, seq_body, (t0_ref[0, 0], t0_ref[0, 1]))
    tfin_ref[0, 0] = tf
    tfin_ref[0, 1] = tb_

    y = yd_scr[0] + yd_scr[1]
    inv_n = 1.0 / RWKV_HEAD
    mu = _head_sum(y, head_ones) * inv_n
    yc = y - mu
    var = _head_sum(yc * yc, head_ones) * inv_n
    yn = yc * lax.rsqrt(var + RWKV_GN_EPS) * lnw_ref[...] + lnb_ref[...]
    bonus = _head_sum(r * kh * rk_ref[...], head_ones) * vh
    g = _bdot(_sigmoid(lo), g2_ref[...])
    y_ref[...] = (yn + bonus) * g


def rwkv_mixer(P, prm, t0, B, L):
    HP = BRANCH_W // LANES
    nl = L
    col = lambda c0: (lambda b, h: (b, c0 // LANES + h))
    vecs = lambda n: pl.BlockSpec((n, LANES), lambda b, h: (0, h))
    dvec = pl.BlockSpec((2, 1, LANES), lambda b, h: (0, 0, h))
    lora_w = pl.BlockSpec((2, LORA_W, LANES), lambda b, h: (0, 0, h))
    seq = lambda: pltpu.VMEM((L, LANES), F32)
    seq2 = lambda: pltpu.VMEM((2, L, LANES), F32)
    nch = L // RWKV_CHUNK
    return pl.pallas_call(
        functools.partial(_rwkv_kernel, L=L),
        grid=(B, HP),
        in_specs=[pl.BlockSpec((nl, LANES), col(COL_DR)),
                  pl.BlockSpec((nl, LANES), col(COL_DK)),
                  pl.BlockSpec((nl, LANES), col(COL_DV)),
                  pl.BlockSpec((nl, LORA_W), lambda b, h: (b, COL_LORA // LORA_W)),
                  vecs(3), dvec, lora_w, dvec, lora_w,
                  pl.BlockSpec((LORA_W, LANES), lambda b, h: (0, h)),
                  vecs(1), vecs(1), vecs(1), vecs(1), vecs(1),
                  pl.BlockSpec((1, 2, LANES, LANES), lambda b, h: (b * HP + h, 0, 0, 0))],
        out_specs=[pl.BlockSpec((nl, LANES), lambda b, h: (b, h)),
                   pl.BlockSpec((1, 2, LANES, LANES), lambda b, h: (b * HP + h, 0, 0, 0))],
        out_shape=[jax.ShapeDtypeStruct((B * L, BRANCH_W), F32),
                   jax.ShapeDtypeStruct((B * HP, 2, LANES, LANES), F32)],
        scratch_shapes=[seq(), seq(), seq(), seq(), seq2(), seq2(), seq2(), seq2(),
                        pltpu.VMEM((2, nch, LANES, LANES), BF16), pltpu.VMEM((2, nch, LANES, LANES), F32),
                        pltpu.VMEM((2, nch, LANES, LANES), F32), pltpu.VMEM((2, nch, RWKV_CHUNK, LANES), BF16),
                        pltpu.VMEM((2, nch, RWKV_CHUNK, LANES), F32)],
        compiler_params=_cparams(2),
        name="rwkv_mixer",
    )(P, P, P, P, prm['mu'], prm['w0'], prm['w2'], prm['a0'], prm['a2'], prm['g2'],
      prm['k_k'], prm['k_a'], prm['r_k'], prm['ln_w'], prm['ln_b'], t0)


def _merge_kernel(ag_ref, hs_ref, ys_ref, yc_ref, yd_ref, m0_ref, m1_ref, m2_ref, m3_ref, x_ref, g1_ref,
                  gw_ref, gb_ref, bw_ref, wo_ref, o_ref):
    y_a = _gelu(ag_ref[...]) * hs_ref[...]
    y_s = _gelu(ys_ref[...])
    y_b = y_s * _sigmoid(_bdot(y_s, gw_ref[...]) + gb_ref[...])
    acc = None
    for n, (y, m_ref) in enumerate(((y_a, m0_ref), (y_b, m1_ref), (yc_ref[...], m2_ref), (yd_ref[...], m3_ref))):
        t = _sigmoid(m_ref[...]) * _bdot(y, bw_ref[n])
        acc = t if acc is None else acc + t
    o_ref[...] = x_ref[...] + g1_ref[0] * _bdot(acc, wo_ref[...])


def merge_out(P, hs_tm, ysum_tm, yc, yd, x, g1, glu_w, glu_b, branch_w, w_out, L, bm):
    M, D = x.shape
    W = BRANCH_W
    nb = g1.shape[0]
    assert L % bm == 0
    per = L // bm
    bidx = (lambda i: (i // per, 0, 0)) if nb > 1 else (lambda i: (0, 0, 0))
    row = lambda w, c: pl.BlockSpec((bm, w), lambda i: (i, c))
    tm = pl.BlockSpec((bm, W), lambda i: (i % per, i // per))
    hs, ysum = (a.reshape(L, (M // L) * W) for a in (hs_tm, ysum_tm))
    const = lambda shape: pl.BlockSpec(shape, lambda i: (0,) * len(shape), pipeline_mode=pl.Buffered(1))
    mcol = COL_MERGE // D
    return pl.pallas_call(
        _merge_kernel,
        grid=(M // bm,),
        in_specs=[row(W, COL_AGATE // W), tm, tm, row(W, 0), row(W, 0),
                  row(D, mcol), row(D, mcol + 1), row(D, mcol + 2), row(D, mcol + 3),
                  row(D, 0), pl.BlockSpec((1, 1, D), bidx),
                  const((W, W)), const((1, W)), const((N_BRANCH, W, D)), const((D, D))],
        out_specs=row(D, 0),
        out_shape=jax.ShapeDtypeStruct((M, D), F32),
        compiler_params=_cparams(1),
        name="merge_out",
    )(P, hs, ysum, yc, yd, P, P, P, P, x, g1, glu_w, glu_b.reshape(1, W), branch_w, w_out)


def _shift_rows(g, L, back):
    bm = g.shape[0]
    rolled = pltpu.roll(g, 1 if back else bm - 1, 0)
    r8 = lax.broadcasted_iota(jnp.int32, (SUBLANES, 1), 0)
    pieces = []
    for s in range(0, bm, L):
        if back:
            pieces += [jnp.where(r8 == 0, 0.0, rolled[s:s + SUBLANES]), rolled[s + SUBLANES:s + L]]
        else:
            e = s + L - SUBLANES
            pieces += [rolled[s:e], jnp.where(r8 == SUBLANES - 1, 0.0, rolled[e:s + L])]
    return jnp.concatenate(pieces, axis=0)


def _shifted_chunk(ref, r0, n, L, back):
    r8 = lax.broadcasted_iota(jnp.int32, (SUBLANES, 1), 0)
    if back:
        if r0 % L == 0:
            body = pltpu.roll(ref[r0:r0 + n, :], 1, 0)
            return jnp.concatenate([jnp.where(r8 == 0, 0.0, body[0:SUBLANES]), body[SUBLANES:]], axis=0)
        return pltpu.roll(ref[r0 - SUBLANES:r0 + n, :], 1, 0)[SUBLANES:]
    if (r0 + n) % L == 0:
        body = pltpu.roll(ref[r0:r0 + n, :], n - 1, 0)
        return jnp.concatenate([body[:n - SUBLANES], jnp.where(r8 == SUBLANES - 1, 0.0, body[n - SUBLANES:])], axis=0)
    return pltpu.roll(ref[r0:r0 + n + SUBLANES, :], n + SUBLANES - 1, 0)[:n]


def _ffn_out_kernel(gc_ref, val_ref, cw_ref, cb_ref, w_ref, x_ref, g2_ref, gf_ref, o_ref, *, L, final):
    k = pl.program_id(1)
    last = pl.num_programs(1) - 1

    @pl.when(k == 0)
    def _():
        o_ref[...] = jnp.zeros_like(o_ref)

    n = min(FFN_ROW_CHUNK, L)
    for r0 in range(0, gc_ref.shape[0], n):
        rows = slice(r0, r0 + n)
        gate = (_shifted_chunk(gc_ref, r0, n, L, True) * cw_ref[0:1, :] + gc_ref[rows, :] * cw_ref[1:2, :]
                + _shifted_chunk(gc_ref, r0, n, L, False) * cw_ref[2:3, :] + cb_ref[...])
        half = 0.5 * gate
        act = (half + half * jnp.tanh(half)) * val_ref[rows, :]
        o_ref[rows, :] += jnp.dot(act.astype(BF16), w_ref[0], preferred_element_type=F32)

    @pl.when(k == last)
    def _():
        def rows_epilogue(r, carry):
            rows = pl.ds(pl.multiple_of(r * LANES, LANES), LANES)
            y = x_ref[rows, :] + g2_ref[0] * o_ref[rows, :]
            if final:
                y = y * lax.rsqrt(jnp.mean(y * y, axis=-1, keepdims=True) + EPS) * gf_ref[...]
            o_ref[rows, :] = y
            return carry

        lax.fori_loop(0, o_ref.shape[0] // LANES, rows_epilogue, 0)


def ffn_out(U, x, g2, conv_w, conv_b, w, l, gf, final, L, bm):
    M, D = x.shape
    bk = FFN_BLOCK
    nk = D_FF_PAD // bk
    nb = g2.shape[0]
    assert bm % L == 0 and (nb == 1 or bm == L)
    bidx = (lambda i, k: (i, 0, 0)) if nb > 1 else (lambda i, k: (0, 0, 0))
    return pl.pallas_call(
        functools.partial(_ffn_out_kernel, L=L, final=final),
        grid=(M // bm, nk),
        in_specs=[pl.BlockSpec((bm, bk), lambda i, k: (i, k)),
                  pl.BlockSpec((bm, bk), lambda i, k: (i, nk + k)),
                  pl.BlockSpec((3, bk), lambda i, k: (0, k)),
                  pl.BlockSpec((1, bk), lambda i, k: (0, k)),
                  pl.BlockSpec((1, bk, D), lambda i, k: (l, k, 0)),
                  pl.BlockSpec((bm, D), lambda i, k: (i, 0), pipeline_mode=pl.Buffered(1)),
                  pl.BlockSpec((1, 1, D), bidx),
                  pl.BlockSpec((1, D), lambda i, k: (0, 0))],
        out_specs=pl.BlockSpec((bm, D), lambda i, k: (i, 0)),
        out_shape=jax.ShapeDtypeStruct((M, D), F32),
        compiler_params=_cparams(2),
        name="ffn_out",
    )(U, U, conv_w, conv_b.reshape(1, D_FF_PAD), w, x, g2, gf.reshape(1, D))


def _block_diag(w):
    n, bs, _ = w.shape
    return jnp.einsum('ncd,nm->ncmd', w, jnp.eye(n, dtype=w.dtype)).reshape(n * bs, n * bs)


def _pad_rows(w, start, total):
    k = w.shape[-2]
    pad = [(0, 0)] * (w.ndim - 2) + [(start, total - start - k), (0, 0)]
    return jnp.pad(w, pad)


def prepare_layer(l, w):
    W = BRANCH_W
    p = {}
    p['lru_wa'] = jnp.stack([_block_diag(w['lru_wa'][l, d]) for d in range(2)]).astype(BF16)
    p['lru_wx'] = jnp.stack([_block_diag(w['lru_wx'][l, d]) for d in range(2)]).astype(BF16)
    p['lru_sp'] = jax.nn.softplus(-w['lru_lam'][l])
    p['s5'] = s5_params(w['s5_a_re'][l], w['s5_a_im'][l], w['s5_log_dt'][l], w['s5_b_re'][l], w['s5_b_im'][l],
                        w['s5_c_re'][l], w['s5_c_im'][l])
    p['rwkv'] = dict(
        mu=w['rwkv_mu'][l], w0=w['rwkv_w0'][l].reshape(2, 1, W), a0=w['rwkv_a0'][l].reshape(2, 1, W),
        w2=_pad_rows(w['rwkv_w2'][l], 0, LORA_W).astype(BF16),
        a2=_pad_rows(w['rwkv_a2'][l], W_LORA, LORA_W).astype(BF16),
        g2=_pad_rows(w['rwkv_g2'][l], W_LORA + A_LORA, LORA_W).astype(BF16),
        k_k=w['rwkv_k_k'][l].reshape(1, W), k_a=w['rwkv_k_a'][l].reshape(1, W),
        r_k=w['rwkv_r_k'][l].reshape(1, W), ln_w=w['rwkv_ln_w'][l].reshape(1, W), ln_b=w['rwkv_ln_b'][l].reshape(1, W))
    p['glu_w'] = w['s5_glu_w'][l].astype(BF16)
    p['branch_w'] = w['branch_w'][l].astype(BF16)
    p['w_out'] = w['w_out'][l].astype(BF16)
    fpad = D_FF_PAD - D_FF
    p['ffn_conv_w'] = jnp.pad(w['ffn_conv_w'][l], ((0, 0), (0, fpad)))
    p['ffn_conv_b'] = jnp.pad(w['ffn_conv_b'][l], (0, fpad))
    return p


def _to_tm(a, B, L):
    return a.reshape(B, L, -1).transpose(1, 0, 2).reshape(L * B, -1)


def _to_bm(a, B, L):
    return a.reshape(L, B, -1).transpose(1, 0, 2).reshape(B * L, -1)


def layer_step(x, mod, p, w, l, B, L, ctx, rope_tabs, bm):
    W = BRANCH_W
    D = D_MODEL
    lat = ctx is not None
    nb = mod.shape[0]
    sh1, sc1, g1, sh2, sc2, g2 = (mod[:, i * D:(i + 1) * D].reshape(nb, 1, D) for i in range(6))
    P = norm_mod_matmul(x, w['norm1'][l], sc1, sh1, w['w_in'], l, L, bm, 1024)

    ax = P[:, COL_AX:COL_AX + W].reshape(B, L, W).transpose(1, 0, 2)
    ax = jnp.pad(ax, ((LRU_PAD_L, LRU_CONV_W - 1 - LRU_PAD_L), (0, 0), (0, 0))).reshape((L + 3) * B, W)
    lru0 = ctx['lru'].transpose(1, 0, 2) if lat else jnp.zeros((2, B, W), F32)
    hs_tm, lru_fin = lru_mixer(ax, w['lru_conv_w'][l], w['lru_conv_b'][l], p['lru_wa'], w['lru_ba'][l],
                               p['lru_wx'], w['lru_bx'][l], p['lru_sp'], lru0, B, L)

    gs = S5_NGROUP // S5_SLICES
    if lat:
        s50 = ctx['s5'].reshape(B, 2, 2, S5_SLICES, gs * S5_STATE).transpose(3, 1, 0, 2, 4).reshape(
            S5_SLICES, 2, B, 2 * S5_HALF)
    else:
        s50 = jnp.zeros((S5_SLICES, 2, B, 2 * S5_HALF), F32)
    wb, ar, ai, wc = p['s5']
    ys_tm, s5_fin = s5_mixer(_to_tm(P[:, COL_BU:COL_BU + W], B, L), wb, ar, ai, wc, w['s5_d'][l], s50, B, L)

    sink = w['attn_sink'][l]
    if lat:
        yc = lat_attention(P, ctx['k_all'], ctx['v_all'], l, sink, rope_tabs, B, L)
    else:
        yc = ctx_attention(P, sink, B, L)

    HP = W // LANES
    if lat:
        st = jnp.swapaxes(ctx['wkv'].reshape(B, 2, HP, 2, RWKV_HEAD, RWKV_HEAD), -1, -2)
        z = jnp.zeros_like(st[:, :, :, 0])
        t0 = jnp.concatenate([jnp.concatenate([st[:, :, :, 0], z], axis=-1),
                              jnp.concatenate([z, st[:, :, :, 1]], axis=-1)], axis=-2)
        t0 = t0.transpose(0, 2, 1, 3, 4).reshape(B * HP, 2, LANES, LANES)
    else:
        t0 = jnp.zeros((B * HP, 2, LANES, LANES), F32)
    yd, tfin = rwkv_mixer(P, p['rwkv'], t0, B, L)

    x = merge_out(P, hs_tm, ys_tm, yc, yd, x, g1, p['glu_w'], w['s5_glu_b'][l], p['branch_w'], p['w_out'], L, 256)
    U = norm_mod_matmul(x, w['norm2'][l], sc2, sh2, w['ffn_w_in_b'], l, L, bm, FFN_BLOCK)
    x = ffn_out(U, x, g2, p['ffn_conv_w'], p['ffn_conv_b'], w['ffn_w_out_b'], l, w['norm_final'], l == DEPTH - 1,
                L, bm)

    new = None
    if not lat:
        k_new = P[:, COL_CK:COL_CK + KV_W].reshape(B, L, N_KV_HEADS, HEAD_DIM)
        v_new = P[:, COL_CV:COL_CV + KV_W].reshape(B, L, N_KV_HEADS, HEAD_DIM)
        lru_new = lru_fin.transpose(1, 0, 2)
        s5_new = s5_fin.reshape(S5_SLICES, 2, B, 2, gs, S5_STATE).transpose(2, 1, 3, 0, 4, 5).reshape(
            B, 2, 2, S5_NGROUP, S5_STATE)
        tf = tfin.reshape(B, HP, 2, LANES, LANES)
        heads = jnp.stack([tf[..., :RWKV_HEAD, :RWKV_HEAD], tf[..., RWKV_HEAD:, RWKV_HEAD:]], axis=3)
        wkv_new = jnp.swapaxes(heads, -1, -2).transpose(0, 2, 1, 3, 4, 5).reshape(
            B, 2, RWKV_NH, RWKV_HEAD, RWKV_HEAD)
        new = (k_new, v_new, lru_new, s5_new, wkv_new)
    return x, new


def kernel(x_prompt, x_sample, cache_k, cache_v, state_lru, state_s5, state_wkv, c, c_ctx, mod_w, mod_b, norm1, norm2, norm_final, w_in, lru_conv_w, lru_conv_b, lru_wa, lru_ba, lru_wx, lru_bx, lru_lam, s5_a_re, s5_a_im, s5_log_dt, s5_b_re, s5_b_im, s5_c_re, s5_c_im, s5_d, s5_glu_w, s5_glu_b, attn_sink, rwkv_mu, rwkv_w0, rwkv_w2, rwkv_a0, rwkv_a2, rwkv_g2, rwkv_k_k, rwkv_k_a, rwkv_r_k, rwkv_ln_w, rwkv_ln_b, branch_w, w_out, ffn_w_in, ffn_conv_w, ffn_conv_b, ffn_w_out):
    w = dict(norm1=norm1, norm2=norm2, norm_final=norm_final, w_in=w_in,
             lru_conv_w=lru_conv_w, lru_conv_b=lru_conv_b, lru_wa=lru_wa, lru_ba=lru_ba,
             lru_wx=lru_wx, lru_bx=lru_bx, lru_lam=lru_lam,
             s5_a_re=s5_a_re, s5_a_im=s5_a_im, s5_log_dt=s5_log_dt, s5_b_re=s5_b_re, s5_b_im=s5_b_im,
             s5_c_re=s5_c_re, s5_c_im=s5_c_im, s5_d=s5_d, s5_glu_w=s5_glu_w, s5_glu_b=s5_glu_b,
             attn_sink=attn_sink,
             rwkv_mu=rwkv_mu, rwkv_w0=rwkv_w0, rwkv_w2=rwkv_w2, rwkv_a0=rwkv_a0, rwkv_a2=rwkv_a2,
             rwkv_g2=rwkv_g2, rwkv_k_k=rwkv_k_k, rwkv_k_a=rwkv_k_a, rwkv_r_k=rwkv_r_k,
             rwkv_ln_w=rwkv_ln_w, rwkv_ln_b=rwkv_ln_b,
             branch_w=branch_w, w_out=w_out, ffn_w_in=ffn_w_in, ffn_conv_w=ffn_conv_w,
             ffn_conv_b=ffn_conv_b, ffn_w_out=ffn_w_out)
    Bc, Lc, D = x_prompt.shape
    Bl, Ll, _ = x_sample.shape
    nrow = 16
    c_all = jnp.concatenate([c_ctx[None, :], c, jnp.zeros((nrow - 1 - Bl, D), F32)], axis=0)
    mods = modulation(c_all, mod_w, mod_b)
    layers = [prepare_layer(l, w) for l in range(DEPTH)]
    w['ffn_w_in_b'], w['ffn_w_out_b'] = ffn_weights_bf16(ffn_w_in, ffn_w_out)
    rope_tabs = rope_tables(Ll)

    xp = x_prompt.reshape(Bc * Lc, D)
    news = []
    for l in range(DEPTH):
        xp, new = layer_step(xp, mods[l, 0:1], layers[l], w, l, Bc, Lc, None, None, 1024)
        news.append(new)
    y_prompt = xp.reshape(Bc, Lc, D)

    xs = x_sample.reshape(Bl * Ll, D)
    for l in range(DEPTH):
        ctx = dict(k_all=cache_k, v_all=cache_v, lru=state_lru[:, l], s5=state_s5[:, l], wkv=state_wkv[:, l])
        xs, _ = layer_step(xs, mods[l, 1:1 + Bl], layers[l], w, l, Bl, Ll, ctx, rope_tabs, min(1024, Ll))
    y_sample = xs.reshape(Bl, Ll, D)

    stack = lambda i: jnp.stack([n[i] for n in news], axis=1)
    return (y_prompt, y_sample, stack(0), stack(1), stack(2), stack(3), stack(4))
```

```python
import functools
import math

import jax
import jax.numpy as jnp
from jax import lax
from jax.experimental import pallas as pl
from jax.experimental.pallas import tpu as pltpu

F32 = jnp.float32
BF16 = jnp.bfloat16
HIGHEST = lax.Precision.HIGHEST

D_MODEL = 2048
DEPTH = 2
PAST_LEN = 256
GRID_W = 64
N_BRANCH = 4
BRANCH_W = 512
EPS = 1e-6
LRU_BLOCKS = 8
LRU_BS = 64
LRU_CONV_W = 4
LRU_PAD_L = 2
LRU_C = 8.0
S5_GROUP = 16
S5_NGROUP = 32
S5_STATE = 64
HEAD_DIM = 64
N_Q_HEADS = 8
N_KV_HEADS = 2
Q_PER_KV = 4
KV_W = 128
WINDOW = 128
BLOCK = 128
ROPE_BASE = 10000.0
ROPE_NF = 16
ATTN_SCALE = 1.0 / math.sqrt(HEAD_DIM)
NEG_INF = -1e30
RWKV_HEAD = 64
RWKV_NH = 8
W_LORA = 64
A_LORA = 64
G_LORA = 128
RWKV_GN_EPS = 64e-5
D_FF = 5504
FFN_BLOCK = 512
D_FF_PAD = 5632
FFN_PIECES = 4
FFN_TAIL = 3
FFN_ROW_CHUNK = 256
LANES = 128
SUBLANES = 8
VMEM_LIMIT = 56 * 2 ** 20

COL_AX, COL_AGATE, COL_BU, COL_CQ = 0, 512, 1024, 1536
COL_CK, COL_CV = 2048, 2176
COL_DR, COL_DK, COL_DV, COL_LORA = 2304, 2816, 3328, 3840
COL_MERGE = 4096
D_IN = COL_MERGE + N_BRANCH * D_MODEL
LORA_W = W_LORA + A_LORA + G_LORA


def _cparams(n_grid):
    return pltpu.CompilerParams(dimension_semantics=("arbitrary",) * n_grid, vmem_limit_bytes=VMEM_LIMIT)


def _bdot(a, b):
    return jnp.dot(a.astype(BF16), b.astype(BF16), preferred_element_type=F32)


def _bdot_nt(a, b):
    return lax.dot_general(a.astype(BF16), b.astype(BF16), (((1,), (1,)), ((), ())),
                           preferred_element_type=F32)


def _gelu(x):
    return jax.nn.gelu(x)


def _sigmoid(x):
    return 0.5 * jnp.tanh(0.5 * x) + 0.5


def _mod_kernel(c_ref, w_ref, b_ref, o_ref):
    c = c_ref[...]
    sh, sl = _split(c * _sigmoid(c))
    wh, wl = _split(w_ref[0])
    dot = functools.partial(jnp.dot, preferred_element_type=F32)
    o_ref[0] = dot(sh, wh) + dot(sl, wh) + dot(sh, wl) + b_ref[0]


def modulation(c_all, mod_w, mod_b):
    R, D = c_all.shape
    N = mod_w.shape[2]
    bn = 1024
    return pl.pallas_call(
        _mod_kernel,
        grid=(DEPTH, N // bn),
        in_specs=[pl.BlockSpec((R, D), lambda l, j: (0, 0)),
                  pl.BlockSpec((1, D, bn), lambda l, j: (l, 0, j)),
                  pl.BlockSpec((1, 1, bn), lambda l, j: (l, 0, j))],
        out_specs=pl.BlockSpec((1, R, bn), lambda l, j: (l, 0, j)),
        out_shape=jax.ShapeDtypeStruct((DEPTH, R, N), F32),
        compiler_params=_cparams(2),
        name="modulation",
    )(c_all, mod_w, mod_b.reshape(DEPTH, 1, N))


def _nmm_kernel(x_ref, g_ref, sc_ref, sh_ref, w_ref, o_ref, h_scr):
    @pl.when(pl.program_id(1) == 0)
    def _():
        x = x_ref[...]
        y = x * lax.rsqrt(jnp.mean(x * x, axis=-1, keepdims=True) + EPS) * g_ref[...]
        h_scr[...] = (y * (1.0 + sc_ref[0]) + sh_ref[0]).astype(BF16)

    o_ref[...] = jnp.dot(h_scr[...], w_ref[0].astype(BF16), preferred_element_type=F32)


def norm_mod_matmul(x, g, sc, sh, w, l, L, bm, bn):
    M, D = x.shape
    N = w.shape[2]
    nb = sc.shape[0]
    if nb > 1:
        assert L % bm == 0
        per = L // bm
        bidx = lambda i, j: (i // per, 0, 0)
    else:
        bidx = lambda i, j: (0, 0, 0)
    return pl.pallas_call(
        _nmm_kernel,
        grid=(M // bm, N // bn),
        in_specs=[pl.BlockSpec((bm, D), lambda i, j: (i, 0)),
                  pl.BlockSpec((1, D), lambda i, j: (0, 0)),
                  pl.BlockSpec((1, 1, D), bidx),
                  pl.BlockSpec((1, 1, D), bidx),
                  pl.BlockSpec((1, D, bn), lambda i, j: (l, 0, j))],
        out_specs=pl.BlockSpec((bm, bn), lambda i, j: (i, j)),
        out_shape=jax.ShapeDtypeStruct((M, N), F32),
        scratch_shapes=[pltpu.VMEM((bm, D), BF16)],
        compiler_params=_cparams(2),
        name="norm_mod_matmul",
    )(x, g.reshape(1, D), sc, sh, w)


def _ffn_cast_kernel(w0_ref, w1_ref, w2_ref, w3_ref, o_ref, *, axis, ntile):
    t = pl.program_id(1)
    for q, w_ref in enumerate((w0_ref, w1_ref, w2_ref, w3_ref)):
        piece = w_ref[0].astype(BF16)
        if axis == 1:
            o_ref[0, :, q * LANES:(q + 1) * LANES] = piece
        else:
            o_ref[0, q * LANES:(q + 1) * LANES, :] = piece

    @pl.when(t % ntile == ntile - 1)
    def _():
        pad = FFN_BLOCK - FFN_TAIL * LANES
        if axis == 1:
            o_ref[0, :, FFN_TAIL * LANES:] = jnp.zeros((o_ref.shape[1], pad), BF16)
        else:
            o_ref[0, FFN_TAIL * LANES:, :] = jnp.zeros((pad, o_ref.shape[2]), BF16)


def ffn_weights_bf16(w_in, w_out):
    nl, D, _ = w_in.shape
    ntile = D_FF_PAD // FFN_BLOCK
    half_pieces = D_FF // LANES

    def col_piece(q):
        last = 2 * half_pieces - 1
        return pl.BlockSpec((1, D, LANES), lambda l, j: (
            l, 0, jnp.minimum((j // ntile) * half_pieces + (j % ntile) * FFN_PIECES + q, last)))

    def row_piece(q):
        return pl.BlockSpec((1, LANES, D), lambda l, k: (l, jnp.minimum(k * FFN_PIECES + q, half_pieces - 1), 0))

    w_in_b = pl.pallas_call(
        functools.partial(_ffn_cast_kernel, axis=1, ntile=ntile),
        grid=(nl, 2 * ntile),
        in_specs=[col_piece(q) for q in range(FFN_PIECES)],
        out_specs=pl.BlockSpec((1, D, FFN_BLOCK), lambda l, j: (l, 0, j)),
        out_shape=jax.ShapeDtypeStruct((nl, D, 2 * D_FF_PAD), BF16),
        compiler_params=_cparams(2),
        name="ffn_w_in_cast",
    )(w_in, w_in, w_in, w_in)
    w_out_b = pl.pallas_call(
        functools.partial(_ffn_cast_kernel, axis=0, ntile=ntile),
        grid=(nl, ntile),
        in_specs=[row_piece(q) for q in range(FFN_PIECES)],
        out_specs=pl.BlockSpec((1, FFN_BLOCK, D), lambda l, k: (l, k, 0)),
        out_shape=jax.ShapeDtypeStruct((nl, D_FF_PAD, D), BF16),
        compiler_params=_cparams(2),
        name="ffn_w_out_cast",
    )(w_out, w_out, w_out, w_out)
    return w_in_b, w_out_b


def _to_time_major(x, B, L):
    return x.reshape(B, L, x.shape[-1]).transpose(1, 0, 2).reshape(L * B, x.shape[-1])


def _to_batch_major(x, B, L):
    return x.reshape(L, B, x.shape[-1]).transpose(1, 0, 2).reshape(B * L, x.shape[-1])


def _lru_kernel(x_ref, cw_ref, cb_ref, wa_ref, ba_ref, wx_ref, bx_ref, sp_ref, h0_ref,
                o_ref, fin_ref, xp_ref, out_ref, a_scr, b_scr, *, B, L, Tc):
    nch = L // Tc
    R = Tc * B
    xp_ref[0:LRU_PAD_L * B, :] = jnp.zeros((LRU_PAD_L * B, LANES), F32)
    xp_ref[LRU_PAD_L * B:(LRU_PAD_L + L) * B, :] = _to_time_major(x_ref[...], B, L)
    tail = LRU_CONV_W - 1 - LRU_PAD_L
    xp_ref[(LRU_PAD_L + L) * B:, :] = jnp.zeros((tail * B, LANES), F32)
    out_ref[...] = jnp.zeros_like(out_ref)

    def chunk(ci, hs):
        bases = (pl.multiple_of(ci * R, R), pl.multiple_of((nch - 1 - ci) * R, R))
        for d in range(2):
            base = bases[d]
            xa = cb_ref[...] + xp_ref[pl.ds(base, R), :] * cw_ref[0:1, :]
            for j in range(1, LRU_CONV_W):
                xa = xa + xp_ref[pl.ds(base + j * B, R), :] * cw_ref[j:j + 1, :]
            r = _sigmoid(_bdot(xa, wa_ref[d]) + ba_ref[d])
            i = _sigmoid(_bdot(xa, wx_ref[d]) + bx_ref[d])
            a = jnp.exp(-LRU_C * r * sp_ref[d])
            a_scr[d] = a
            b_scr[d] = jnp.sqrt(1.0 - a * a) * (i * xa)

        def step(ti, hs):
            new = []
            for d in range(2):
                off = pl.multiple_of((ti if d == 0 else Tc - 1 - ti) * B, B)
                h = a_scr[d, pl.ds(off, B), :] * hs[d] + b_scr[d, pl.ds(off, B), :]
                row = pl.ds(bases[d] + off, B)
                out_ref[row, :] = out_ref[row, :] + h
                new.append(h)
            return tuple(new)

        return lax.fori_loop(0, Tc, step, hs, unroll=8)

    hf, hb = lax.fori_loop(0, nch, chunk, (h0_ref[0], h0_ref[1]))
    fin_ref[0] = hf
    fin_ref[1] = hb
    o_ref[...] = _to_batch_major(out_ref[...], B, L)


def lru_mixer(P, cw, cb, wa, ba, wx, bx, sp, h0, B, L):
    W = BRANCH_W
    Tc = min(128, L // 2)
    kern = functools.partial(_lru_kernel, B=B, L=L, Tc=Tc)
    vec = lambda: pl.BlockSpec((2, 1, LANES), lambda s: (0, 0, s))
    return pl.pallas_call(
        kern,
        grid=(W // LANES,),
        in_specs=[pl.BlockSpec((B * L, LANES), lambda s: (0, COL_AX // LANES + s)),
                  pl.BlockSpec((LRU_CONV_W, LANES), lambda s: (0, s)),
                  pl.BlockSpec((1, LANES), lambda s: (0, s)),
                  pl.BlockSpec((2, LANES, LANES), lambda s: (0, s, s)), vec(),
                  pl.BlockSpec((2, LANES, LANES), lambda s: (0, s, s)), vec(),
                  vec(),
                  pl.BlockSpec((2, B, LANES), lambda s: (0, 0, s))],
        out_specs=[pl.BlockSpec((L * B, LANES), lambda s: (0, s)),
                   pl.BlockSpec((2, B, LANES), lambda s: (0, 0, s))],
        out_shape=[jax.ShapeDtypeStruct((L * B, W), F32), jax.ShapeDtypeStruct((2, B, W), F32)],
        scratch_shapes=[pltpu.VMEM(((L + LRU_CONV_W - 1) * B, LANES), F32), pltpu.VMEM((L * B, LANES), F32),
                        pltpu.VMEM((2, Tc * B, LANES), F32), pltpu.VMEM((2, Tc * B, LANES), F32)],
        compiler_params=_cparams(1),
        name="lru_mixer",
    )(P, cw, cb.reshape(1, W), wa, ba.reshape(2, 1, W), wx, bx.reshape(2, 1, W), sp.reshape(2, 1, W), h0)


S5_SLICES = BRANCH_W // LANES
S5_HALF = (S5_NGROUP // S5_SLICES) * S5_STATE


def _s5_kernel(x_ref, wb_ref, ar_ref, ai_ref, wc_ref, dv_ref, h0_ref, o_ref, fin_ref, u_ref, out_ref, hs_scr,
               *, B, L, Tc):
    nch = L // Tc
    R = Tc * B
    H = S5_HALF
    u_ref[...] = _to_time_major(x_ref[...], B, L)
    out_ref[...] = u_ref[...] * dv_ref[...]
    ab = [(ar_ref[0, d], ai_ref[0, d]) for d in range(2)]

    def chunk(ci, carry):
        bases = (pl.multiple_of(ci * R, R), pl.multiple_of((nch - 1 - ci) * R, R))
        for d in range(2):
            hs_scr[d] = _bdot(u_ref[pl.ds(bases[d], R), :], wb_ref[0, d])

        def step(ti, carry):
            new = []
            for d in range(2):
                hr, hi = carry[d]
                ar, ai = ab[d]
                row = pl.ds(pl.multiple_of((ti if d == 0 else Tc - 1 - ti) * B, B), B)
                nr = ar * hr - ai * hi + hs_scr[d, row, 0:H]
                ni = ar * hi + ai * hr + hs_scr[d, row, H:2 * H]
                hs_scr[d, row, 0:H] = nr
                hs_scr[d, row, H:2 * H] = ni
                new.append((nr, ni))
            return tuple(new)

        carry = lax.fori_loop(0, Tc, step, carry, unroll=2)
        for d in range(2):
            rows = pl.ds(bases[d], R)
            out_ref[rows, :] = out_ref[rows, :] + _bdot(hs_scr[d], wc_ref[0, d])
        return carry

    init = tuple((h0_ref[0, d, :, 0:H], h0_ref[0, d, :, H:2 * H]) for d in range(2))
    fin = lax.fori_loop(0, nch, chunk, init)
    for d in range(2):
        fin_ref[0, d, :, 0:H] = fin[d][0]
        fin_ref[0, d, :, H:2 * H] = fin[d][1]
    o_ref[...] = _to_batch_major(out_ref[...], B, L)


def s5_mixer(P, wb, ar, ai, wc, dvec, h0, B, L):
    W = BRANCH_W
    Tc = min(128, L // 2, 1024 // B)
    H2 = 2 * S5_HALF
    kern = functools.partial(_s5_kernel, B=B, L=L, Tc=Tc)
    return pl.pallas_call(
        kern,
        grid=(S5_SLICES,),
        in_specs=[pl.BlockSpec((B * L, LANES), lambda s: (0, COL_BU // LANES + s)),
                  pl.BlockSpec((1, 2, LANES, H2), lambda s: (s, 0, 0, 0)),
                  pl.BlockSpec((1, 2, 1, S5_HALF), lambda s: (s, 0, 0, 0)),
                  pl.BlockSpec((1, 2, 1, S5_HALF), lambda s: (s, 0, 0, 0)),
                  pl.BlockSpec((1, 2, H2, LANES), lambda s: (s, 0, 0, 0)),
                  pl.BlockSpec((1, LANES), lambda s: (0, s)),
                  pl.BlockSpec((1, 2, B, H2), lambda s: (s, 0, 0, 0))],
        out_specs=[pl.BlockSpec((L * B, LANES), lambda s: (0, s)),
                   pl.BlockSpec((1, 2, B, H2), lambda s: (s, 0, 0, 0))],
        out_shape=[jax.ShapeDtypeStruct((L * B, W), F32), jax.ShapeDtypeStruct((S5_SLICES, 2, B, H2), F32)],
        scratch_shapes=[pltpu.VMEM((L * B, LANES), F32), pltpu.VMEM((L * B, LANES), F32),
                        pltpu.VMEM((2, Tc * B, H2), F32)],
        compiler_params=_cparams(1),
        name="s5_mixer",
    )(P, wb, ar, ai, wc, dvec.reshape(1, W), h0)


def s5_params(a_re, a_im, log_dt, b_re, b_im, c_re, c_im):
    dt = jnp.exp(log_dt)[..., None]
    mag = jnp.exp(a_re * dt)
    abar_r = mag * jnp.cos(a_im * dt)
    abar_i = mag * jnp.sin(a_im * dt)
    den = a_re * a_re + a_im * a_im
    nr = abar_r - 1.0
    fr = (nr * a_re + abar_i * a_im) / den
    fi = (abar_i * a_re - nr * a_im) / den
    bbar_r = fr[..., None] * b_re - fi[..., None] * b_im
    bbar_i = fr[..., None] * b_im + fi[..., None] * b_re
    gs = S5_NGROUP // S5_SLICES
    eye = jnp.eye(gs, dtype=F32)
    bb = jnp.stack([bbar_r, bbar_i], axis=0).reshape(2, 2, S5_SLICES, gs, S5_STATE, S5_GROUP)
    wb = jnp.einsum('rdsgnp,gh->sdgprhn', bb, eye).reshape(S5_SLICES, 2, LANES, 2 * S5_HALF)
    cc = jnp.stack([c_re, -c_im], axis=0).reshape(2, 2, S5_SLICES, gs, S5_GROUP, S5_STATE)
    wc = jnp.einsum('rdsgpn,gh->sdrhngp', cc, eye).reshape(S5_SLICES, 2, 2 * S5_HALF, LANES)
    ar = abar_r.reshape(2, S5_SLICES, 1, S5_HALF).transpose(1, 0, 2, 3)
    ai = abar_i.reshape(2, S5_SLICES, 1, S5_HALF).transpose(1, 0, 2, 3)
    return wb.astype(BF16), ar, ai, wc.astype(BF16)


def _rope(x, cos, sina, sinb):
    return x * cos + pltpu.roll(x, ROPE_NF, 1) * sina + pltpu.roll(x, LANES - ROPE_NF, 1) * sinb


def _attend(qg, segs, sink_col):
    def lane_tiles(a):
        return [a[:, c * LANES:(c + 1) * LANES] for c in range(a.shape[1] // LANES)]

    def tree(op, xs):
        while len(xs) > 1:
            xs = [op(xs[i], xs[i + 1]) if i + 1 < len(xs) else xs[i] for i in range(0, len(xs), 2)]
        return xs[0]

    scores = []
    for k, _, mask in segs:
        s = _bdot_nt(qg, k) * ATTN_SCALE
        if mask is not None:
            s = jnp.where(mask, s, NEG_INF)
        scores.append(s)
    m = jnp.maximum(sink_col, jnp.max(tree(jnp.maximum, [t for s in scores for t in lane_tiles(s)]),
                                      axis=-1, keepdims=True))
    ps = [jnp.exp(s - m) for s in scores]
    den = jnp.exp(sink_col - m) + jnp.sum(tree(jnp.add, [t for p in ps for t in lane_tiles(p)]),
                                          axis=-1, keepdims=True)
    o = tree(jnp.add, [_bdot(p, v) for p, (_, v, _) in zip(ps, segs)])
    return o * (1.0 / den)


def _sink_col(sink_ref, kvh):
    return jnp.concatenate([jnp.full((BLOCK, 1), sink_ref[kvh * Q_PER_KV + g], F32) for g in range(Q_PER_KV)], axis=0)


def _stack_heads(q, kvh):
    return jnp.concatenate([q[:, (kvh * Q_PER_KV + g) * HEAD_DIM:(kvh * Q_PER_KV + g + 1) * HEAD_DIM]
                            for g in range(Q_PER_KV)], axis=0)


def _store_heads(o_ref, og, kvh):
    for g in range(Q_PER_KV):
        h = kvh * Q_PER_KV + g
        o_ref[:, h * HEAD_DIM:(h + 1) * HEAD_DIM] = og[g * BLOCK:(g + 1) * BLOCK]


def _ctx_attn_kernel(sink_ref, q_ref, k_ref, v_ref, o_ref):
    q = q_ref[...]
    k = k_ref[...]
    v = v_ref[...]
    for kvh in range(N_KV_HEADS):
        ks = k[:, kvh * HEAD_DIM:(kvh + 1) * HEAD_DIM]
        vs = v[:, kvh * HEAD_DIM:(kvh + 1) * HEAD_DIM]
        og = _attend(_stack_heads(q, kvh), [(ks, vs, None)], _sink_col(sink_ref, kvh))
        _store_heads(o_ref, og, kvh)


def ctx_attention(P, sink, B, L):
    nb = L // BLOCK
    return pl.pallas_call(
        _ctx_attn_kernel,
        grid=(B, nb),
        in_specs=[pl.BlockSpec(memory_space=pltpu.SMEM),
                  pl.BlockSpec((BLOCK, BRANCH_W), lambda b, j: (b * nb + j, COL_CQ // BRANCH_W)),
                  pl.BlockSpec((L, KV_W), lambda b, j: (b, COL_CK // KV_W)),
                  pl.BlockSpec((L, KV_W), lambda b, j: (b, COL_CV // KV_W))],
        out_specs=pl.BlockSpec((BLOCK, BRANCH_W), lambda b, j: (b * nb + j, 0)),
        out_shape=jax.ShapeDtypeStruct((B * L, BRANCH_W), F32),
        compiler_params=_cparams(2),
        name="ctx_attention",
    )(sink.reshape(N_Q_HEADS), P, P, P)


def _lat_attn_kernel(sink_ref, q_ref, kp_ref, kc_ref, kn_ref, vp_ref, vc_ref, vn_ref, ck_ref, cv_ref,
                     cq_ref, saq_ref, sbq_ref, cp_ref, sap_ref, sbp_ref, cn_ref, san_ref, sbn_ref, o_ref, *, nb):
    j = pl.program_id(1)
    cq, saq, sbq = cq_ref[...], saq_ref[...], sbq_ref[...]
    q = jnp.concatenate([_rope(q_ref[:, c * LANES:(c + 1) * LANES], cq, saq, sbq)
                         for c in range(BRANCH_W // LANES)], axis=1)
    kp = _rope(kp_ref[...], cp_ref[...], sap_ref[...], sbp_ref[...])
    kc = _rope(kc_ref[...], cq, saq, sbq)
    kn = _rope(kn_ref[...], cn_ref[...], san_ref[...], sbn_ref[...])
    R = Q_PER_KV * BLOCK
    r = lax.broadcasted_iota(jnp.int32, (R, BLOCK), 0) % BLOCK
    c = lax.broadcasted_iota(jnp.int32, (R, BLOCK), 1)
    mask_p = (c >= r) & (j > 0)
    mask_n = (c <= r) & (j < nb - 1)
    ck = ck_ref[0, 0]
    cv = cv_ref[0, 0]
    for kvh in range(N_KV_HEADS):
        sl = slice(kvh * HEAD_DIM, (kvh + 1) * HEAD_DIM)
        segs = [(ck[:, sl], cv[:, sl], None),
                (kp[:, sl], vp_ref[:, sl], mask_p),
                (kc[:, sl], vc_ref[:, sl], None),
                (kn[:, sl], vn_ref[:, sl], mask_n)]
        og = _attend(_stack_heads(q, kvh), segs, _sink_col(sink_ref, kvh))
        _store_heads(o_ref, og, kvh)


def lat_attention(P, cache_k, cache_v, layer, sink, rope_tabs, B, L):
    nb = L // BLOCK
    kcol, vcol = COL_CK // KV_W, COL_CV // KV_W
    prev = lambda b, j: b * nb + jnp.maximum(j - 1, 0)
    nxt = lambda b, j: b * nb + jnp.minimum(j + 1, nb - 1)
    kv = lambda rowf, col: pl.BlockSpec((BLOCK, KV_W), lambda b, j: (rowf(b, j), col))
    cur = lambda b, j: b * nb + j
    tab = lambda f: pl.BlockSpec((BLOCK, LANES), f)
    tq = lambda b, j: (j, 0)
    tp = lambda b, j: (jnp.maximum(j - 1, 0), 0)
    tn = lambda b, j: (jnp.minimum(j + 1, nb - 1), 0)
    cos, sina, sinb = rope_tabs
    cache = pl.BlockSpec((1, 1, PAST_LEN, KV_W), lambda b, j: (b, layer, 0, 0))
    return pl.pallas_call(
        functools.partial(_lat_attn_kernel, nb=nb),
        grid=(B, nb),
        in_specs=[pl.BlockSpec(memory_space=pltpu.SMEM),
                  pl.BlockSpec((BLOCK, BRANCH_W), lambda b, j: (b * nb + j, COL_CQ // BRANCH_W)),
                  kv(prev, kcol), kv(cur, kcol), kv(nxt, kcol),
                  kv(prev, vcol), kv(cur, vcol), kv(nxt, vcol),
                  cache, cache,
                  tab(tq), tab(tq), tab(tq), tab(tp), tab(tp), tab(tp), tab(tn), tab(tn), tab(tn)],
        out_specs=pl.BlockSpec((BLOCK, BRANCH_W), lambda b, j: (b * nb + j, 0)),
        out_shape=jax.ShapeDtypeStruct((B * L, BRANCH_W), F32),
        compiler_params=_cparams(2),
        name="lat_attention",
    )(sink.reshape(N_Q_HEADS), P, P, P, P, P, P, P,
      cache_k.reshape(cache_k.shape[0], DEPTH, PAST_LEN, KV_W), cache_v.reshape(cache_v.shape[0], DEPTH, PAST_LEN, KV_W),
      cos, sina, sinb, cos, sina, sinb, cos, sina, sinb)


def rope_tables(L):
    t = jnp.arange(L)
    inv = ROPE_BASE ** (-jnp.arange(ROPE_NF, dtype=F32) / ROPE_NF)
    ang_r = (t // GRID_W).astype(F32)[:, None] * inv
    ang_c = (t % GRID_W).astype(F32)[:, None] * inv
    z = jnp.zeros((L, ROPE_NF), F32)
    cos = jnp.concatenate([jnp.cos(ang_r)] * 2 + [jnp.cos(ang_c)] * 2, axis=1)
    sina = jnp.concatenate([z, jnp.sin(ang_r), z, jnp.sin(ang_c)], axis=1)
    sinb = jnp.concatenate([-jnp.sin(ang_r), z, -jnp.sin(ang_c), z], axis=1)
    return tuple(jnp.tile(a, (1, LANES // HEAD_DIM)) for a in (cos, sina, sinb))


RWKV_CHUNK = 64
RWKV_PREP_CHUNKS = 4


def _split(a):
    hi = a.astype(BF16)
    return hi, (a - hi.astype(F32)).astype(BF16)


def _bdot3(a, b):
    ah, al = _split(a)
    bh, bl = _split(b)
    return jnp.dot(jnp.concatenate([ah, al, ah], axis=1), jnp.concatenate([bh, bh, bl], axis=0),
                   preferred_element_type=F32)


def _head_sum(x, head_ones):
    return _bdot3(x, head_ones)


def _unit_lower_inverse(ns, eye):
    xs = [eye + n for n in ns]
    pws = list(ns)
    for _ in range(5):
        pws = [_bdot(pw, pw) for pw in pws]
        xs = [x + _bdot(x, pw) for x, pw in zip(xs, pws)]
    resids = [eye - (x - _bdot3(n, x)) for n, x in zip(ns, xs)]
    return [x + _bdot(x, resid) for x, resid in zip(xs, resids)]


def _rwkv_kernel(xr_ref, xk_ref, xv_ref, lo_ref, mu_ref, w0_ref, w2_ref, a0_ref, a2_ref, g2_ref,
                 kk_ref, ka_ref, rk_ref, lnw_ref, lnb_ref, t0_ref,
                 y_ref, tfin_ref,
                 r_scr, kh_scr, v_scr, kn_scr, lw_scr, a_scr, kd_scr, yd_scr,
                 phi_scr, psi_scr, pcm_scr, rys_scr, gys_scr, *, L):
    C = RWKV_CHUNK
    nch = L // C
    lane = lax.broadcasted_iota(jnp.int32, (1, LANES), 1)
    m0 = (lane < RWKV_HEAD).astype(F32)
    m1 = 1.0 - m0
    head_ones = (lax.broadcasted_iota(jnp.int32, (LANES, LANES), 0) // RWKV_HEAD
                 == lax.broadcasted_iota(jnp.int32, (LANES, LANES), 1) // RWKV_HEAD).astype(F32)

    def shift(x, mu):
        return x + mu * (0.5 * (_shift_rows(x, L, True) + _shift_rows(x, L, False)) - x)

    r = shift(xr_ref[...], mu_ref[0:1, :])
    kh = shift(xk_ref[...], mu_ref[1:2, :])
    vh = shift(xv_ref[...], mu_ref[2:3, :])
    r_scr[...] = r
    kh_scr[...] = kh
    v_scr[...] = vh
    kk = kh * kk_ref[...]
    kn_scr[...] = kk * lax.rsqrt(_head_sum(kk * kk, head_ones) + 1e-12)
    lo = lo_ref[...]
    tw = jnp.tanh(lo)
    for d in range(2):
        z = w0_ref[d] + _bdot(tw, w2_ref[d])
        lw_scr[d] = -math.exp(-0.5) * _sigmoid(z)
        a = _sigmoid(a0_ref[d] + _bdot(lo, a2_ref[d]))
        a_scr[d] = a
        kd_scr[d] = kh * (1.0 + (a - 1.0) * ka_ref[...])

    ri = lax.broadcasted_iota(jnp.int32, (2 * C, 2 * C), 0)
    ci = lax.broadcasted_iota(jnp.int32, (2 * C, 2 * C), 1)
    same = (ri // C) == (ci // C)
    eye2 = (ri == ci).astype(F32)
    ti = lax.broadcasted_iota(jnp.int32, (C, C), 0)
    si = lax.broadcasted_iota(jnp.int32, (C, C), 1)
    rt, ct = ri % C, ci % C
    strict = (same & (ct < rt), same & (ct > rt))
    incl = (same & (ct <= rt), same & (ct >= rt))
    tri = ((si <= ti).astype(F32), (si >= ti).astype(F32))

    def stack(x):
        return jnp.concatenate([x * m0, x * m1], axis=0)

    def unstack(x):
        return x[0:C] + x[C:2 * C]

    def prepare(jobs):
        J = range(len(jobs))
        ds = [d for _, d in jobs]
        rows = [pl.ds(pl.multiple_of(c * C, C), C) for c, _ in jobs]
        lw = [lw_scr[ds[j], rows[j], :] for j in J]
        cum = [jnp.dot(tri[ds[j]], lw[j], preferred_element_type=F32, precision=HIGHEST) for j in J]
        p_in = [jnp.exp(cum[j]) for j in J]
        p_ex = [jnp.exp(cum[j] - lw[j]) for j in J]
        ip = [jnp.exp(-cum[j]) for j in J]
        pc = [p_in[j][C - 1:C, :] if ds[j] == 0 else p_in[j][0:1, :] for j in J]
        kn = [kn_scr[rows[j], :] for j in J]
        KAP = [stack(kn[j] * p_ex[j]) for j in J]
        RT = [stack(r_scr[rows[j], :] * p_in[j]) for j in J]
        BT = [stack(-(a_scr[ds[j], rows[j], :] * kn[j]) * ip[j]) for j in J]
        KT = [stack(kd_scr[ds[j], rows[j], :] * ip[j]) for j in J]
        V = [stack(v_scr[rows[j], :]) for j in J]
        G = [_bdot_nt(jnp.concatenate([KAP[j], RT[j]], axis=0), jnp.concatenate([BT[j], KT[j]], axis=0))
             for j in J]
        aab = [jnp.where(strict[ds[j]], G[j][0:2 * C, 0:2 * C], 0.0) for j in J]
        aak = [jnp.where(strict[ds[j]], G[j][0:2 * C, 2 * C:4 * C], 0.0) for j in J]
        grb = [jnp.where(incl[ds[j]], G[j][2 * C:4 * C, 0:2 * C], 0.0) for j in J]
        grk = [jnp.where(incl[ds[j]], G[j][2 * C:4 * C, 2 * C:4 * C], 0.0) for j in J]
        AV = [_bdot(aak[j], V[j]) for j in J]
        GV = [_bdot(grk[j], V[j]) for j in J]
        X = _unit_lower_inverse(aab, eye2)
        XK = [_bdot(X[j], KAP[j]) for j in J]
        XAV = [_bdot(X[j], AV[j]) for j in J]
        btp_t = [(BT[j] * pc[j]).T for j in J]
        ktp_t = [(KT[j] * pc[j]).T for j in J]
        phi = [_bdot(btp_t[j], XK[j]) for j in J]
        psi = [_bdot(btp_t[j], XAV[j]) + _bdot(ktp_t[j], V[j]) for j in J]
        rys = [unstack(RT[j] + _bdot(grb[j], XK[j])) for j in J]
        gys = [unstack(_bdot(grb[j], XAV[j]) + GV[j]) for j in J]
        for j, (c, d) in enumerate(jobs):
            phi_scr[d, c] = phi[j].astype(BF16)
            psi_scr[d, c] = psi[j]
            pcm_scr[d, c] = jnp.broadcast_to(pc[j], (LANES, LANES)).T
            rys_scr[d, c] = rys[j].astype(BF16)
            gys_scr[d, c] = gys[j]

    def prep_body(i, carry):
        prepare([(RWKV_PREP_CHUNKS * i + u, d) for u in range(RWKV_PREP_CHUNKS) for d in range(2)])
        return carry

    lax.fori_loop(0, nch // RWKV_PREP_CHUNKS, prep_body, 0)

    def seq_body(i, carry):
        out = []
        for d, T in enumerate(carry):
            c = i if d == 0 else nch - 1 - i
            rows = pl.ds(pl.multiple_of(c * C, C), C)
            tb = T.astype(BF16)
            yd_scr[d, rows, :] = jnp.dot(rys_scr[d, c], tb, preferred_element_type=F32) + gys_scr[d, c]
            out.append(pcm_scr[d, c] * T + jnp.dot(phi_scr[d, c], tb, preferred_element_type=F32) + psi_scr[d, c])
        return tuple(out)

    tf, tb_ = lax.fori_loop(0, nch, seq_body, (t0_ref[0, 0], t0_ref[0, 1]))
    tfin_ref[0, 0] = tf
    tfin_ref[0, 1] = tb_

    y = yd_scr[0] + yd_scr[1]
    inv_n = 1.0 / RWKV_HEAD
    mu = _head_sum(y, head_ones) * inv_n
    yc = y - mu
    var = _head_sum(yc * yc, head_ones) * inv_n
    yn = yc * lax.rsqrt(var + RWKV_GN_EPS) * lnw_ref[...] + lnb_ref[...]
    bonus = _head_sum(r * kh * rk_ref[...], head_ones) * vh
    g = _bdot(_sigmoid(lo), g2_ref[...])
    y_ref[...] = (yn + bonus) * g


def rwkv_mixer(P, prm, t0, B, L):
    HP = BRANCH_W // LANES
    nl = L
    col = lambda c0: (lambda b, h: (b, c0 // LANES + h))
    vecs = lambda n: pl.BlockSpec((n, LANES), lambda b, h: (0, h))
    dvec = pl.BlockSpec((2, 1, LANES), lambda b, h: (0, 0, h))
    lora_w = pl.BlockSpec((2, LORA_W, LANES), lambda b, h: (0, 0, h))
    seq = lambda: pltpu.VMEM((L, LANES), F32)
    seq2 = lambda: pltpu.VMEM((2, L, LANES), F32)
    nch = L // RWKV_CHUNK
    return pl.pallas_call(
        functools.partial(_rwkv_kernel, L=L),
        grid=(B, HP),
        in_specs=[pl.BlockSpec((nl, LANES), col(COL_DR)),
                  pl.BlockSpec((nl, LANES), col(COL_DK)),
                  pl.BlockSpec((nl, LANES), col(COL_DV)),
                  pl.BlockSpec((nl, LORA_W), lambda b, h: (b, COL_LORA // LORA_W)),
                  vecs(3), dvec, lora_w, dvec, lora_w,
                  pl.BlockSpec((LORA_W, LANES), lambda b, h: (0, h)),
                  vecs(1), vecs(1), vecs(1), vecs(1), vecs(1),
                  pl.BlockSpec((1, 2, LANES, LANES), lambda b, h: (b * HP + h, 0, 0, 0))],
        out_specs=[pl.BlockSpec((nl, LANES), lambda b, h: (b, h)),
                   pl.BlockSpec((1, 2, LANES, LANES), lambda b, h: (b * HP + h, 0, 0, 0))],
        out_shape=[jax.ShapeDtypeStruct((B * L, BRANCH_W), F32),
                   jax.ShapeDtypeStruct((B * HP, 2, LANES, LANES), F32)],
        scratch_shapes=[seq(), seq(), seq(), seq(), seq2(), seq2(), seq2(), seq2(),
                        pltpu.VMEM((2, nch, LANES, LANES), BF16), pltpu.VMEM((2, nch, LANES, LANES), F32),
                        pltpu.VMEM((2, nch, LANES, LANES), F32), pltpu.VMEM((2, nch, RWKV_CHUNK, LANES), BF16),
                        pltpu.VMEM((2, nch, RWKV_CHUNK, LANES), F32)],
        compiler_params=_cparams(2),
        name="rwkv_mixer",
    )(P, P, P, P, prm['mu'], prm['w0'], prm['w2'], prm['a0'], prm['a2'], prm['g2'],
      prm['k_k'], prm['k_a'], prm['r_k'], prm['ln_w'], prm['ln_b'], t0)


def _merge_kernel(ag_ref, hs_ref, ys_ref, yc_ref, yd_ref, m0_ref, m1_ref, m2_ref, m3_ref, x_ref, g1_ref,
                  gw_ref, gb_ref, bw_ref, wo_ref, o_ref):
    y_a = _gelu(ag_ref[...]) * hs_ref[...]
    y_s = _gelu(ys_ref[...])
    y_b = y_s * _sigmoid(_bdot(y_s, gw_ref[...]) + gb_ref[...])
    acc = None
    for n, (y, m_ref) in enumerate(((y_a, m0_ref), (y_b, m1_ref), (yc_ref[...], m2_ref), (yd_ref[...], m3_ref))):
        t = _sigmoid(m_ref[...]) * _bdot(y, bw_ref[n])
        acc = t if acc is None else acc + t
    o_ref[...] = x_ref[...] + g1_ref[0] * _bdot(acc, wo_ref[...])


def merge_out(P, hs, ysum, yc, yd, x, g1, glu_w, glu_b, branch_w, w_out, L, bm):
    M, D = x.shape
    W = BRANCH_W
    nb = g1.shape[0]
    assert L % bm == 0
    per = L // bm
    bidx = (lambda i: (i // per, 0, 0)) if nb > 1 else (lambda i: (0, 0, 0))
    row = lambda w, c: pl.BlockSpec((bm, w), lambda i: (i, c))
    const = lambda shape: pl.BlockSpec(shape, lambda i: (0,) * len(shape), pipeline_mode=pl.Buffered(1))
    mcol = COL_MERGE // D
    return pl.pallas_call(
        _merge_kernel,
        grid=(M // bm,),
        in_specs=[row(W, COL_AGATE // W), row(W, 0), row(W, 0), row(W, 0), row(W, 0),
                  row(D, mcol), row(D, mcol + 1), row(D, mcol + 2), row(D, mcol + 3),
                  row(D, 0), pl.BlockSpec((1, 1, D), bidx),
                  const((W, W)), const((1, W)), const((N_BRANCH, W, D)), const((D, D))],
        out_specs=row(D, 0),
        out_shape=jax.ShapeDtypeStruct((M, D), F32),
        compiler_params=_cparams(1),
        name="merge_out",
    )(P, hs, ysum, yc, yd, P, P, P, P, x, g1, glu_w, glu_b.reshape(1, W), branch_w, w_out)


def _shift_rows(g, L, back):
    bm = g.shape[0]
    rolled = pltpu.roll(g, 1 if back else bm - 1, 0)
    r8 = lax.broadcasted_iota(jnp.int32, (SUBLANES, 1), 0)
    pieces = []
    for s in range(0, bm, L):
        if back:
            pieces += [jnp.where(r8 == 0, 0.0, rolled[s:s + SUBLANES]), rolled[s + SUBLANES:s + L]]
        else:
            e = s + L - SUBLANES
            pieces += [rolled[s:e], jnp.where(r8 == SUBLANES - 1, 0.0, rolled[e:s + L])]
    return jnp.concatenate(pieces, axis=0)


def _shifted_chunk(ref, r0, n, L, back):
    r8 = lax.broadcasted_iota(jnp.int32, (SUBLANES, 1), 0)
    if back:
        if r0 % L == 0:
            body = pltpu.roll(ref[r0:r0 + n, :], 1, 0)
            return jnp.concatenate([jnp.where(r8 == 0, 0.0, body[0:SUBLANES]), body[SUBLANES:]], axis=0)
        return pltpu.roll(ref[r0 - SUBLANES:r0 + n, :], 1, 0)[SUBLANES:]
    if (r0 + n) % L == 0:
        body = pltpu.roll(ref[r0:r0 + n, :], n - 1, 0)
        return jnp.concatenate([body[:n - SUBLANES], jnp.where(r8 == SUBLANES - 1, 0.0, body[n - SUBLANES:])], axis=0)
    return pltpu.roll(ref[r0:r0 + n + SUBLANES, :], n + SUBLANES - 1, 0)[:n]


def _ffn_out_kernel(gc_ref, val_ref, cw_ref, cb_ref, w_ref, x_ref, g2_ref, gf_ref, o_ref, *, L, final):
    k = pl.program_id(1)
    last = pl.num_programs(1) - 1

    @pl.when(k == 0)
    def _():
        o_ref[...] = jnp.zeros_like(o_ref)

    n = min(FFN_ROW_CHUNK, L)
    for r0 in range(0, gc_ref.shape[0], n):
        rows = slice(r0, r0 + n)
        gate = (_shifted_chunk(gc_ref, r0, n, L, True) * cw_ref[0:1, :] + gc_ref[rows, :] * cw_ref[1:2, :]
                + _shifted_chunk(gc_ref, r0, n, L, False) * cw_ref[2:3, :] + cb_ref[...])
        half = 0.5 * gate
        act = (half + half * jnp.tanh(half)) * val_ref[rows, :]
        o_ref[rows, :] += jnp.dot(act.astype(BF16), w_ref[0], preferred_element_type=F32)

    @pl.when(k == last)
    def _():
        def rows_epilogue(r, carry):
            rows = pl.ds(pl.multiple_of(r * LANES, LANES), LANES)
            y = x_ref[rows, :] + g2_ref[0] * o_ref[rows, :]
            if final:
                y = y * lax.rsqrt(jnp.mean(y * y, axis=-1, keepdims=True) + EPS) * gf_ref[...]
            o_ref[rows, :] = y
            return carry

        lax.fori_loop(0, o_ref.shape[0] // LANES, rows_epilogue, 0)


def ffn_out(U, x, g2, conv_w, conv_b, w, l, gf, final, L, bm):
    M, D = x.shape
    bk = FFN_BLOCK
    nk = D_FF_PAD // bk
    nb = g2.shape[0]
    assert bm % L == 0 and (nb == 1 or bm == L)
    bidx = (lambda i, k: (i, 0, 0)) if nb > 1 else (lambda i, k: (0, 0, 0))
    return pl.pallas_call(
        functools.partial(_ffn_out_kernel, L=L, final=final),
        grid=(M // bm, nk),
        in_specs=[pl.BlockSpec((bm, bk), lambda i, k: (i, k)),
                  pl.BlockSpec((bm, bk), lambda i, k: (i, nk + k)),
                  pl.BlockSpec((3, bk), lambda i, k: (0, k)),
                  pl.BlockSpec((1, bk), lambda i, k: (0, k)),
                  pl.BlockSpec((1, bk, D), lambda i, k: (l, k, 0)),
                  pl.BlockSpec((bm, D), lambda i, k: (i, 0), pipeline_mode=pl.Buffered(1)),
                  pl.BlockSpec((1, 1, D), bidx),
                  pl.BlockSpec((1, D), lambda i, k: (0, 0))],
        out_specs=pl.BlockSpec((bm, D), lambda i, k: (i, 0)),
        out_shape=jax.ShapeDtypeStruct((M, D), F32),
        compiler_params=_cparams(2),
        name="ffn_out",
    )(U, U, conv_w, conv_b.reshape(1, D_FF_PAD), w, x, g2, gf.reshape(1, D))


def _block_diag(w):
    n, bs, _ = w.shape
    return jnp.einsum('ncd,nm->ncmd', w, jnp.eye(n, dtype=w.dtype)).reshape(n * bs, n * bs)


def _pad_rows(w, start, total):
    k = w.shape[-2]
    pad = [(0, 0)] * (w.ndim - 2) + [(start, total - start - k), (0, 0)]
    return jnp.pad(w, pad)


def prepare_layer(l, w):
    W = BRANCH_W
    p = {}
    p['lru_wa'] = jnp.stack([_block_diag(w['lru_wa'][l, d]) for d in range(2)]).astype(BF16)
    p['lru_wx'] = jnp.stack([_block_diag(w['lru_wx'][l, d]) for d in range(2)]).astype(BF16)
    p['lru_sp'] = jax.nn.softplus(-w['lru_lam'][l])
    p['s5'] = s5_params(w['s5_a_re'][l], w['s5_a_im'][l], w['s5_log_dt'][l], w['s5_b_re'][l], w['s5_b_im'][l],
                        w['s5_c_re'][l], w['s5_c_im'][l])
    p['rwkv'] = dict(
        mu=w['rwkv_mu'][l], w0=w['rwkv_w0'][l].reshape(2, 1, W), a0=w['rwkv_a0'][l].reshape(2, 1, W),
        w2=_pad_rows(w['rwkv_w2'][l], 0, LORA_W).astype(BF16),
        a2=_pad_rows(w['rwkv_a2'][l], W_LORA, LORA_W).astype(BF16),
        g2=_pad_rows(w['rwkv_g2'][l], W_LORA + A_LORA, LORA_W).astype(BF16),
        k_k=w['rwkv_k_k'][l].reshape(1, W), k_a=w['rwkv_k_a'][l].reshape(1, W),
        r_k=w['rwkv_r_k'][l].reshape(1, W), ln_w=w['rwkv_ln_w'][l].reshape(1, W), ln_b=w['rwkv_ln_b'][l].reshape(1, W))
    p['glu_w'] = w['s5_glu_w'][l].astype(BF16)
    p['branch_w'] = w['branch_w'][l].astype(BF16)
    p['w_out'] = w['w_out'][l].astype(BF16)
    fpad = D_FF_PAD - D_FF
    p['ffn_conv_w'] = jnp.pad(w['ffn_conv_w'][l], ((0, 0), (0, fpad)))
    p['ffn_conv_b'] = jnp.pad(w['ffn_conv_b'][l], (0, fpad))
    return p


def layer_step(x, mod, p, w, l, B, L, ctx, rope_tabs, bm):
    W = BRANCH_W
    D = D_MODEL
    lat = ctx is not None
    nb = mod.shape[0]
    sh1, sc1, g1, sh2, sc2, g2 = (mod[:, i * D:(i + 1) * D].reshape(nb, 1, D) for i in range(6))
    P = norm_mod_matmul(x, w['norm1'][l], sc1, sh1, w['w_in'], l, L, bm, 1024)

    lru0 = ctx['lru'].transpose(1, 0, 2) if lat else jnp.zeros((2, B, W), F32)
    hs, lru_fin = lru_mixer(P, w['lru_conv_w'][l], w['lru_conv_b'][l], p['lru_wa'], w['lru_ba'][l],
                            p['lru_wx'], w['lru_bx'][l], p['lru_sp'], lru0, B, L)

    gs = S5_NGROUP // S5_SLICES
    if lat:
        s50 = ctx['s5'].reshape(B, 2, 2, S5_SLICES, gs * S5_STATE).transpose(3, 1, 0, 2, 4).reshape(
            S5_SLICES, 2, B, 2 * S5_HALF)
    else:
        s50 = jnp.zeros((S5_SLICES, 2, B, 2 * S5_HALF), F32)
    wb, ar, ai, wc = p['s5']
    ysum, s5_fin = s5_mixer(P, wb, ar, ai, wc, w['s5_d'][l], s50, B, L)

    sink = w['attn_sink'][l]
    if lat:
        yc = lat_attention(P, ctx['k_all'], ctx['v_all'], l, sink, rope_tabs, B, L)
    else:
        yc = ctx_attention(P, sink, B, L)

    HP = W // LANES
    if lat:
        st = jnp.swapaxes(ctx['wkv'].reshape(B, 2, HP, 2, RWKV_HEAD, RWKV_HEAD), -1, -2)
        z = jnp.zeros_like(st[:, :, :, 0])
        t0 = jnp.concatenate([jnp.concatenate([st[:, :, :, 0], z], axis=-1),
                              jnp.concatenate([z, st[:, :, :, 1]], axis=-1)], axis=-2)
        t0 = t0.transpose(0, 2, 1, 3, 4).reshape(B * HP, 2, LANES, LANES)
    else:
        t0 = jnp.zeros((B * HP, 2, LANES, LANES), F32)
    yd, tfin = rwkv_mixer(P, p['rwkv'], t0, B, L)

    x = merge_out(P, hs, ysum, yc, yd, x, g1, p['glu_w'], w['s5_glu_b'][l], p['branch_w'], p['w_out'], L, 256)
    U = norm_mod_matmul(x, w['norm2'][l], sc2, sh2, w['ffn_w_in_b'], l, L, bm, FFN_BLOCK)
    x = ffn_out(U, x, g2, p['ffn_conv_w'], p['ffn_conv_b'], w['ffn_w_out_b'], l, w['norm_final'], l == DEPTH - 1,
                L, bm)

    new = None
    if not lat:
        k_new = P[:, COL_CK:COL_CK + KV_W].reshape(B, L, N_KV_HEADS, HEAD_DIM)
        v_new = P[:, COL_CV:COL_CV + KV_W].reshape(B, L, N_KV_HEADS, HEAD_DIM)
        lru_new = lru_fin.transpose(1, 0, 2)
        s5_new = s5_fin.reshape(S5_SLICES, 2, B, 2, gs, S5_STATE).transpose(2, 1, 3, 0, 4, 5).reshape(
            B, 2, 2, S5_NGROUP, S5_STATE)
        tf = tfin.reshape(B, HP, 2, LANES, LANES)
        heads = jnp.stack([tf[..., :RWKV_HEAD, :RWKV_HEAD], tf[..., RWKV_HEAD:, RWKV_HEAD:]], axis=3)
        wkv_new = jnp.swapaxes(heads, -1, -2).transpose(0, 2, 1, 3, 4, 5).reshape(
            B, 2, RWKV_NH, RWKV_HEAD, RWKV_HEAD)
        new = (k_new, v_new, lru_new, s5_new, wkv_new)
    return x, new


def kernel(x_prompt, x_sample, cache_k, cache_v, state_lru, state_s5, state_wkv, c, c_ctx, mod_w, mod_b, norm1, norm2, norm_final, w_in, lru_conv_w, lru_conv_b, lru_wa, lru_ba, lru_wx, lru_bx, lru_lam, s5_a_re, s5_a_im, s5_log_dt, s5_b_re, s5_b_im, s5_c_re, s5_c_im, s5_d, s5_glu_w, s5_glu_b, attn_sink, rwkv_mu, rwkv_w0, rwkv_w2, rwkv_a0, rwkv_a2, rwkv_g2, rwkv_k_k, rwkv_k_a, rwkv_r_k, rwkv_ln_w, rwkv_ln_b, branch_w, w_out, ffn_w_in, ffn_conv_w, ffn_conv_b, ffn_w_out):
    w = dict(norm1=norm1, norm2=norm2, norm_final=norm_final, w_in=w_in,
             lru_conv_w=lru_conv_w, lru_conv_b=lru_conv_b, lru_wa=lru_wa, lru_ba=lru_ba,
             lru_wx=lru_wx, lru_bx=lru_bx, lru_lam=lru_lam,
             s5_a_re=s5_a_re, s5_a_im=s5_a_im, s5_log_dt=s5_log_dt, s5_b_re=s5_b_re, s5_b_im=s5_b_im,
             s5_c_re=s5_c_re, s5_c_im=s5_c_im, s5_d=s5_d, s5_glu_w=s5_glu_w, s5_glu_b=s5_glu_b,
             attn_sink=attn_sink,
             rwkv_mu=rwkv_mu, rwkv_w0=rwkv_w0, rwkv_w2=rwkv_w2, rwkv_a0=rwkv_a0, rwkv_a2=rwkv_a2,
             rwkv_g2=rwkv_g2, rwkv_k_k=rwkv_k_k, rwkv_k_a=rwkv_k_a, rwkv_r_k=rwkv_r_k,
             rwkv_ln_w=rwkv_ln_w, rwkv_ln_b=rwkv_ln_b,
             branch_w=branch_w, w_out=w_out, ffn_w_in=ffn_w_in, ffn_conv_w=ffn_conv_w,
             ffn_conv_b=ffn_conv_b, ffn_w_out=ffn_w_out)
    Bc, Lc, D = x_prompt.shape
    Bl, Ll, _ = x_sample.shape
    nrow = 16
    c_all = jnp.concatenate([c_ctx[None, :], c, jnp.zeros((nrow - 1 - Bl, D), F32)], axis=0)
    mods = modulation(c_all, mod_w, mod_b)
    layers = [prepare_layer(l, w) for l in range(DEPTH)]
    w['ffn_w_in_b'], w['ffn_w_out_b'] = ffn_weights_bf16(ffn_w_in, ffn_w_out)
    rope_tabs = rope_tables(Ll)

    xp = x_prompt.reshape(Bc * Lc, D)
    news = []
    for l in range(DEPTH):
        xp, new = layer_step(xp, mods[l, 0:1], layers[l], w, l, Bc, Lc, None, None, 1024)
        news.append(new)
    y_prompt = xp.reshape(Bc, Lc, D)

    xs = x_sample.reshape(Bl * Ll, D)
    for l in range(DEPTH):
        ctx = dict(k_all=cache_k, v_all=cache_v, lru=state_lru[:, l], s5=state_s5[:, l], wkv=state_wkv[:, l])
        xs, _ = layer_step(xs, mods[l, 1:1 + Bl], layers[l], w, l, Bl, Ll, ctx, rope_tabs, min(1024, Ll))
    y_sample = xs.reshape(Bl, Ll, D)

    stack = lambda i: jnp.stack([n[i] for n in news], axis=1)
    return (y_prompt, y_sample, stack(0), stack(1), stack(2), stack(3), stack(4))
```

```python
import functools
import math

import jax
import jax.numpy as jnp
from jax import lax
from jax.experimental import pallas as pl
from jax.experimental.pallas import tpu as pltpu

F32 = jnp.float32
BF16 = jnp.bfloat16
HIGHEST = lax.Precision.HIGHEST

D_MODEL = 2048
DEPTH = 2
PAST_LEN = 256
GRID_W = 64
N_BRANCH = 4
BRANCH_W = 512
EPS = 1e-6
LRU_BLOCKS = 8
LRU_BS = 64
LRU_CONV_W = 4
LRU_PAD_L = 2
LRU_C = 8.0
S5_GROUP = 16
S5_NGROUP = 32
S5_STATE = 64
HEAD_DIM = 64
N_Q_HEADS = 8
N_KV_HEADS = 2
Q_PER_KV = 4
KV_W = 128
WINDOW = 128
BLOCK = 128
ROPE_BASE = 10000.0
ROPE_NF = 16
ATTN_SCALE = 1.0 / math.sqrt(HEAD_DIM)
NEG_INF = -1e30
RWKV_HEAD = 64
RWKV_NH = 8
W_LORA = 64
A_LORA = 64
G_LORA = 128
RWKV_GN_EPS = 64e-5
D_FF = 5504
FFN_BLOCK = 512
D_FF_PAD = 5632
FFN_PIECES = 4
FFN_TAIL = 3
FFN_ROW_CHUNK = 256
NMM_ROW_CHUNK = 256
LANES = 128
SUBLANES = 8
VMEM_LIMIT = 56 * 2 ** 20

COL_AX, COL_AGATE, COL_BU, COL_CQ = 0, 512, 1024, 1536
COL_CK, COL_CV = 2048, 2176
COL_DR, COL_DK, COL_DV, COL_LORA = 2304, 2816, 3328, 3840
COL_MERGE = 4096
D_IN = COL_MERGE + N_BRANCH * D_MODEL
LORA_W = W_LORA + A_LORA + G_LORA


def _cparams(n_grid):
    return pltpu.CompilerParams(dimension_semantics=("arbitrary",) * n_grid, vmem_limit_bytes=VMEM_LIMIT)


def _bdot(a, b):
    return jnp.dot(a.astype(BF16), b.astype(BF16), preferred_element_type=F32)


def _bdot_nt(a, b):
    return lax.dot_general(a.astype(BF16), b.astype(BF16), (((1,), (1,)), ((), ())),
                           preferred_element_type=F32)


def _gelu(x):
    return jax.nn.gelu(x)


def _sigmoid(x):
    return 0.5 * jnp.tanh(0.5 * x) + 0.5


def _mod_kernel(c_ref, w_ref, b_ref, o_ref):
    c = c_ref[...]
    sh, sl = _split(c * _sigmoid(c))
    wh, wl = _split(w_ref[0])
    dot = functools.partial(jnp.dot, preferred_element_type=F32)
    o_ref[0] = dot(sh, wh) + dot(sl, wh) + dot(sh, wl) + b_ref[0]


def modulation(c_all, mod_w, mod_b):
    R, D = c_all.shape
    N = mod_w.shape[2]
    bn = 1024
    return pl.pallas_call(
        _mod_kernel,
        grid=(DEPTH, N // bn),
        in_specs=[pl.BlockSpec((R, D), lambda l, j: (0, 0)),
                  pl.BlockSpec((1, D, bn), lambda l, j: (l, 0, j)),
                  pl.BlockSpec((1, 1, bn), lambda l, j: (l, 0, j))],
        out_specs=pl.BlockSpec((1, R, bn), lambda l, j: (l, 0, j)),
        out_shape=jax.ShapeDtypeStruct((DEPTH, R, N), F32),
        compiler_params=_cparams(2),
        name="modulation",
    )(c_all, mod_w, mod_b.reshape(DEPTH, 1, N))


def _nmm_kernel(x_ref, g_ref, sc_ref, sh_ref, w_ref, o_ref, h_scr):
    j = pl.program_id(1)
    out = o_ref.at[0] if len(o_ref.shape) == 3 else o_ref
    bm = h_scr.shape[0]

    @pl.when(j == 0)
    def _():
        w = w_ref[0].astype(BF16)
        for r0 in range(0, bm, NMM_ROW_CHUNK):
            rows = slice(r0, r0 + NMM_ROW_CHUNK)
            x = x_ref[rows, :]
            y = x * lax.rsqrt(jnp.mean(x * x, axis=-1, keepdims=True) + EPS) * g_ref[...]
            h = (y * (1.0 + sc_ref[0]) + sh_ref[0]).astype(BF16)
            h_scr[rows, :] = h
            out[rows, :] = jnp.dot(h, w, preferred_element_type=F32)

    @pl.when(j > 0)
    def _():
        out[...] = jnp.dot(h_scr[...], w_ref[0].astype(BF16), preferred_element_type=F32)


def norm_mod_matmul(x, g, sc, sh, w, l, L, bm, bn, tiled_out=False):
    M, D = x.shape
    N = w.shape[2]
    nb = sc.shape[0]
    if nb > 1:
        assert L % bm == 0
        per = L // bm
        bidx = lambda i, j: (i // per, 0, 0)
    else:
        bidx = lambda i, j: (0, 0, 0)
    if tiled_out:
        out_spec = pl.BlockSpec((1, bm, bn), lambda i, j: (j, i, 0))
        out_shape = jax.ShapeDtypeStruct((N // bn, M, bn), F32)
    else:
        out_spec = pl.BlockSpec((bm, bn), lambda i, j: (i, j))
        out_shape = jax.ShapeDtypeStruct((M, N), F32)
    return pl.pallas_call(
        _nmm_kernel,
        grid=(M // bm, N // bn),
        in_specs=[pl.BlockSpec((bm, D), lambda i, j: (i, 0)),
                  pl.BlockSpec((1, D), lambda i, j: (0, 0)),
                  pl.BlockSpec((1, 1, D), bidx),
                  pl.BlockSpec((1, 1, D), bidx),
                  pl.BlockSpec((1, D, bn), lambda i, j: (l, 0, j))],
        out_specs=out_spec,
        out_shape=out_shape,
        scratch_shapes=[pltpu.VMEM((bm, D), BF16)],
        compiler_params=_cparams(2),
        name="norm_mod_matmul",
    )(x, g.reshape(1, D), sc, sh, w)


def _ffn_cast_kernel(w0_ref, w1_ref, w2_ref, w3_ref, o_ref, *, axis, ntile):
    t = pl.program_id(1)
    for q, w_ref in enumerate((w0_ref, w1_ref, w2_ref, w3_ref)):
        piece = w_ref[0].astype(BF16)
        if axis == 1:
            o_ref[0, :, q * LANES:(q + 1) * LANES] = piece
        else:
            o_ref[0, q * LANES:(q + 1) * LANES, :] = piece

    @pl.when(t % ntile == ntile - 1)
    def _():
        pad = FFN_BLOCK - FFN_TAIL * LANES
        if axis == 1:
            o_ref[0, :, FFN_TAIL * LANES:] = jnp.zeros((o_ref.shape[1], pad), BF16)
        else:
            o_ref[0, FFN_TAIL * LANES:, :] = jnp.zeros((pad, o_ref.shape[2]), BF16)


def ffn_weights_bf16(w_in, w_out):
    nl, D, _ = w_in.shape
    ntile = D_FF_PAD // FFN_BLOCK
    half_pieces = D_FF // LANES

    def col_piece(q):
        last = 2 * half_pieces - 1
        return pl.BlockSpec((1, D, LANES), lambda l, j: (
            l, 0, jnp.minimum((j // ntile) * half_pieces + (j % ntile) * FFN_PIECES + q, last)))

    def row_piece(q):
        return pl.BlockSpec((1, LANES, D), lambda l, k: (l, jnp.minimum(k * FFN_PIECES + q, half_pieces - 1), 0))

    w_in_b = pl.pallas_call(
        functools.partial(_ffn_cast_kernel, axis=1, ntile=ntile),
        grid=(nl, 2 * ntile),
        in_specs=[col_piece(q) for q in range(FFN_PIECES)],
        out_specs=pl.BlockSpec((1, D, FFN_BLOCK), lambda l, j: (l, 0, j)),
        out_shape=jax.ShapeDtypeStruct((nl, D, 2 * D_FF_PAD), BF16),
        compiler_params=_cparams(2),
        name="ffn_w_in_cast",
    )(w_in, w_in, w_in, w_in)
    w_out_b = pl.pallas_call(
        functools.partial(_ffn_cast_kernel, axis=0, ntile=ntile),
        grid=(nl, ntile),
        in_specs=[row_piece(q) for q in range(FFN_PIECES)],
        out_specs=pl.BlockSpec((1, FFN_BLOCK, D), lambda l, k: (l, k, 0)),
        out_shape=jax.ShapeDtypeStruct((nl, D_FF_PAD, D), BF16),
        compiler_params=_cparams(2),
        name="ffn_w_out_cast",
    )(w_out, w_out, w_out, w_out)
    return w_in_b, w_out_b


def _to_time_major(x, B, L):
    return x.reshape(B, L, x.shape[-1]).transpose(1, 0, 2).reshape(L * B, x.shape[-1])


def _to_batch_major(x, B, L):
    return x.reshape(L, B, x.shape[-1]).transpose(1, 0, 2).reshape(B * L, x.shape[-1])


def _lru_kernel(x_ref, cw_ref, cb_ref, wa_ref, ba_ref, wx_ref, bx_ref, sp_ref, h0_ref,
                o_ref, fin_ref, xp_ref, out_ref, a_scr, b_scr, *, B, L, Tc):
    nch = L // Tc
    R = Tc * B
    xp_ref[0:LRU_PAD_L * B, :] = jnp.zeros((LRU_PAD_L * B, LANES), F32)
    xp_ref[LRU_PAD_L * B:(LRU_PAD_L + L) * B, :] = _to_time_major(x_ref[...], B, L)
    tail = LRU_CONV_W - 1 - LRU_PAD_L
    xp_ref[(LRU_PAD_L + L) * B:, :] = jnp.zeros((tail * B, LANES), F32)
    out_ref[...] = jnp.zeros_like(out_ref)

    def chunk(ci, hs):
        bases = (pl.multiple_of(ci * R, R), pl.multiple_of((nch - 1 - ci) * R, R))
        for d in range(2):
            base = bases[d]
            xa = cb_ref[...] + xp_ref[pl.ds(base, R), :] * cw_ref[0:1, :]
            for j in range(1, LRU_CONV_W):
                xa = xa + xp_ref[pl.ds(base + j * B, R), :] * cw_ref[j:j + 1, :]
            r = _sigmoid(_bdot(xa, wa_ref[d]) + ba_ref[d])
            i = _sigmoid(_bdot(xa, wx_ref[d]) + bx_ref[d])
            a = jnp.exp(-LRU_C * r * sp_ref[d])
            a_scr[d] = a
            b_scr[d] = jnp.sqrt(1.0 - a * a) * (i * xa)

        def step(ti, hs):
            new = []
            for d in range(2):
                off = pl.multiple_of((ti if d == 0 else Tc - 1 - ti) * B, B)
                h = a_scr[d, pl.ds(off, B), :] * hs[d] + b_scr[d, pl.ds(off, B), :]
                row = pl.ds(bases[d] + off, B)
                out_ref[row, :] = out_ref[row, :] + h
                new.append(h)
            return tuple(new)

        return lax.fori_loop(0, Tc, step, hs, unroll=8)

    hf, hb = lax.fori_loop(0, nch, chunk, (h0_ref[0], h0_ref[1]))
    fin_ref[0] = hf
    fin_ref[1] = hb
    o_ref[...] = _to_batch_major(out_ref[...], B, L)


def lru_mixer(P, cw, cb, wa, ba, wx, bx, sp, h0, B, L):
    W = BRANCH_W
    Tc = min(128, L // 2)
    kern = functools.partial(_lru_kernel, B=B, L=L, Tc=Tc)
    vec = lambda: pl.BlockSpec((2, 1, LANES), lambda s: (0, 0, s))
    return pl.pallas_call(
        kern,
        grid=(W // LANES,),
        in_specs=[pl.BlockSpec((B * L, LANES), lambda s: (0, COL_AX // LANES + s)),
                  pl.BlockSpec((LRU_CONV_W, LANES), lambda s: (0, s)),
                  pl.BlockSpec((1, LANES), lambda s: (0, s)),
                  pl.BlockSpec((2, LANES, LANES), lambda s: (0, s, s)), vec(),
                  pl.BlockSpec((2, LANES, LANES), lambda s: (0, s, s)), vec(),
                  vec(),
                  pl.BlockSpec((2, B, LANES), lambda s: (0, 0, s))],
        out_specs=[pl.BlockSpec((L * B, LANES), lambda s: (0, s)),
                   pl.BlockSpec((2, B, LANES), lambda s: (0, 0, s))],
        out_shape=[jax.ShapeDtypeStruct((L * B, W), F32), jax.ShapeDtypeStruct((2, B, W), F32)],
        scratch_shapes=[pltpu.VMEM(((L + LRU_CONV_W - 1) * B, LANES), F32), pltpu.VMEM((L * B, LANES), F32),
                        pltpu.VMEM((2, Tc * B, LANES), F32), pltpu.VMEM((2, Tc * B, LANES), F32)],
        compiler_params=_cparams(1),
        name="lru_mixer",
    )(P, cw, cb.reshape(1, W), wa, ba.reshape(2, 1, W), wx, bx.reshape(2, 1, W), sp.reshape(2, 1, W), h0)


S5_SLICES = BRANCH_W // LANES
S5_HALF = (S5_NGROUP // S5_SLICES) * S5_STATE


def _s5_kernel(x_ref, wb_ref, ar_ref, ai_ref, wc_ref, dv_ref, h0_ref, o_ref, fin_ref, u_ref, out_ref, hs_scr,
               *, B, L, Tc):
    nch = L // Tc
    R = Tc * B
    H = S5_HALF
    u_ref[...] = _to_time_major(x_ref[...], B, L)
    out_ref[...] = u_ref[...] * dv_ref[...]
    ab = [(ar_ref[0, d], ai_ref[0, d]) for d in range(2)]

    def chunk(ci, carry):
        bases = (pl.multiple_of(ci * R, R), pl.multiple_of((nch - 1 - ci) * R, R))
        for d in range(2):
            hs_scr[d] = _bdot(u_ref[pl.ds(bases[d], R), :], wb_ref[0, d])

        def step(ti, carry):
            new = []
            for d in range(2):
                hr, hi = carry[d]
                ar, ai = ab[d]
                row = pl.ds(pl.multiple_of((ti if d == 0 else Tc - 1 - ti) * B, B), B)
                nr = ar * hr - ai * hi + hs_scr[d, row, 0:H]
                ni = ar * hi + ai * hr + hs_scr[d, row, H:2 * H]
                hs_scr[d, row, 0:H] = nr
                hs_scr[d, row, H:2 * H] = ni
                new.append((nr, ni))
            return tuple(new)

        carry = lax.fori_loop(0, Tc, step, carry, unroll=2)
        for d in range(2):
            rows = pl.ds(bases[d], R)
            out_ref[rows, :] = out_ref[rows, :] + _bdot(hs_scr[d], wc_ref[0, d])
        return carry

    init = tuple((h0_ref[0, d, :, 0:H], h0_ref[0, d, :, H:2 * H]) for d in range(2))
    fin = lax.fori_loop(0, nch, chunk, init)
    for d in range(2):
        fin_ref[0, d, :, 0:H] = fin[d][0]
        fin_ref[0, d, :, H:2 * H] = fin[d][1]
    o_ref[...] = _to_batch_major(out_ref[...], B, L)


def s5_mixer(P, wb, ar, ai, wc, dvec, h0, B, L):
    W = BRANCH_W
    Tc = min(128, L // 2, 1024 // B)
    H2 = 2 * S5_HALF
    kern = functools.partial(_s5_kernel, B=B, L=L, Tc=Tc)
    return pl.pallas_call(
        kern,
        grid=(S5_SLICES,),
        in_specs=[pl.BlockSpec((B * L, LANES), lambda s: (0, COL_BU // LANES + s)),
                  pl.BlockSpec((1, 2, LANES, H2), lambda s: (s, 0, 0, 0)),
                  pl.BlockSpec((1, 2, 1, S5_HALF), lambda s: (s, 0, 0, 0)),
                  pl.BlockSpec((1, 2, 1, S5_HALF), lambda s: (s, 0, 0, 0)),
                  pl.BlockSpec((1, 2, H2, LANES), lambda s: (s, 0, 0, 0)),
                  pl.BlockSpec((1, LANES), lambda s: (0, s)),
                  pl.BlockSpec((1, 2, B, H2), lambda s: (s, 0, 0, 0))],
        out_specs=[pl.BlockSpec((L * B, LANES), lambda s: (0, s)),
                   pl.BlockSpec((1, 2, B, H2), lambda s: (s, 0, 0, 0))],
        out_shape=[jax.ShapeDtypeStruct((L * B, W), F32), jax.ShapeDtypeStruct((S5_SLICES, 2, B, H2), F32)],
        scratch_shapes=[pltpu.VMEM((L * B, LANES), F32), pltpu.VMEM((L * B, LANES), F32),
                        pltpu.VMEM((2, Tc * B, H2), F32)],
        compiler_params=_cparams(1),
        name="s5_mixer",
    )(P, wb, ar, ai, wc, dvec.reshape(1, W), h0)


def s5_params(a_re, a_im, log_dt, b_re, b_im, c_re, c_im):
    dt = jnp.exp(log_dt)[..., None]
    mag = jnp.exp(a_re * dt)
    abar_r = mag * jnp.cos(a_im * dt)
    abar_i = mag * jnp.sin(a_im * dt)
    den = a_re * a_re + a_im * a_im
    nr = abar_r - 1.0
    fr = (nr * a_re + abar_i * a_im) / den
    fi = (abar_i * a_re - nr * a_im) / den
    bbar_r = fr[..., None] * b_re - fi[..., None] * b_im
    bbar_i = fr[..., None] * b_im + fi[..., None] * b_re
    gs = S5_NGROUP // S5_SLICES
    eye = jnp.eye(gs, dtype=F32)
    bb = jnp.stack([bbar_r, bbar_i], axis=0).reshape(2, 2, S5_SLICES, gs, S5_STATE, S5_GROUP)
    wb = jnp.einsum('rdsgnp,gh->sdgprhn', bb, eye).reshape(S5_SLICES, 2, LANES, 2 * S5_HALF)
    cc = jnp.stack([c_re, -c_im], axis=0).reshape(2, 2, S5_SLICES, gs, S5_GROUP, S5_STATE)
    wc = jnp.einsum('rdsgpn,gh->sdrhngp', cc, eye).reshape(S5_SLICES, 2, 2 * S5_HALF, LANES)
    ar = abar_r.reshape(2, S5_SLICES, 1, S5_HALF).transpose(1, 0, 2, 3)
    ai = abar_i.reshape(2, S5_SLICES, 1, S5_HALF).transpose(1, 0, 2, 3)
    return wb.astype(BF16), ar, ai, wc.astype(BF16)


def _rope(x, cos, sina, sinb):
    return x * cos + pltpu.roll(x, ROPE_NF, 1) * sina + pltpu.roll(x, LANES - ROPE_NF, 1) * sinb


def _attend(qg, segs, sink_col):
    def lane_tiles(a):
        return [a[:, c * LANES:(c + 1) * LANES] for c in range(a.shape[1] // LANES)]

    def tree(op, xs):
        while len(xs) > 1:
            xs = [op(xs[i], xs[i + 1]) if i + 1 < len(xs) else xs[i] for i in range(0, len(xs), 2)]
        return xs[0]

    scores = []
    for k, _, mask in segs:
        s = _bdot_nt(qg, k) * ATTN_SCALE
        if mask is not None:
            s = jnp.where(mask, s, NEG_INF)
        scores.append(s)
    m = jnp.maximum(sink_col, jnp.max(tree(jnp.maximum, [t for s in scores for t in lane_tiles(s)]),
                                      axis=-1, keepdims=True))
    ps = [jnp.exp(s - m) for s in scores]
    den = jnp.exp(sink_col - m) + jnp.sum(tree(jnp.add, [t for p in ps for t in lane_tiles(p)]),
                                          axis=-1, keepdims=True)
    o = tree(jnp.add, [_bdot(p, v) for p, (_, v, _) in zip(ps, segs)])
    return o * (1.0 / den)


def _sink_col(sink_ref, kvh):
    return jnp.concatenate([jnp.full((BLOCK, 1), sink_ref[kvh * Q_PER_KV + g], F32) for g in range(Q_PER_KV)], axis=0)


def _stack_heads(q, kvh):
    return jnp.concatenate([q[:, (kvh * Q_PER_KV + g) * HEAD_DIM:(kvh * Q_PER_KV + g + 1) * HEAD_DIM]
                            for g in range(Q_PER_KV)], axis=0)


def _store_heads(o_ref, og, kvh):
    for g in range(Q_PER_KV):
        h = kvh * Q_PER_KV + g
        o_ref[:, h * HEAD_DIM:(h + 1) * HEAD_DIM] = og[g * BLOCK:(g + 1) * BLOCK]


def _ctx_attn_kernel(sink_ref, q_ref, k_ref, v_ref, o_ref):
    q = q_ref[...]
    k = k_ref[...]
    v = v_ref[...]
    for kvh in range(N_KV_HEADS):
        ks = k[:, kvh * HEAD_DIM:(kvh + 1) * HEAD_DIM]
        vs = v[:, kvh * HEAD_DIM:(kvh + 1) * HEAD_DIM]
        og = _attend(_stack_heads(q, kvh), [(ks, vs, None)], _sink_col(sink_ref, kvh))
        _store_heads(o_ref, og, kvh)


def ctx_attention(P, sink, B, L):
    nb = L // BLOCK
    return pl.pallas_call(
        _ctx_attn_kernel,
        grid=(B, nb),
        in_specs=[pl.BlockSpec(memory_space=pltpu.SMEM),
                  pl.BlockSpec((BLOCK, BRANCH_W), lambda b, j: (b * nb + j, COL_CQ // BRANCH_W)),
                  pl.BlockSpec((L, KV_W), lambda b, j: (b, COL_CK // KV_W)),
                  pl.BlockSpec((L, KV_W), lambda b, j: (b, COL_CV // KV_W))],
        out_specs=pl.BlockSpec((BLOCK, BRANCH_W), lambda b, j: (b * nb + j, 0)),
        out_shape=jax.ShapeDtypeStruct((B * L, BRANCH_W), F32),
        compiler_params=_cparams(2),
        name="ctx_attention",
    )(sink.reshape(N_Q_HEADS), P, P, P)


def _lat_attn_kernel(sink_ref, q_ref, kp_ref, kc_ref, kn_ref, vp_ref, vc_ref, vn_ref, ck_ref, cv_ref,
                     cq_ref, saq_ref, sbq_ref, cp_ref, sap_ref, sbp_ref, cn_ref, san_ref, sbn_ref, o_ref, *, nb):
    j = pl.program_id(1)
    cq, saq, sbq = cq_ref[...], saq_ref[...], sbq_ref[...]
    q = jnp.concatenate([_rope(q_ref[:, c * LANES:(c + 1) * LANES], cq, saq, sbq)
                         for c in range(BRANCH_W // LANES)], axis=1)
    kp = _rope(kp_ref[...], cp_ref[...], sap_ref[...], sbp_ref[...])
    kc = _rope(kc_ref[...], cq, saq, sbq)
    kn = _rope(kn_ref[...], cn_ref[...], san_ref[...], sbn_ref[...])
    R = Q_PER_KV * BLOCK
    r = lax.broadcasted_iota(jnp.int32, (R, BLOCK), 0) % BLOCK
    c = lax.broadcasted_iota(jnp.int32, (R, BLOCK), 1)
    mask_p = (c >= r) & (j > 0)
    mask_n = (c <= r) & (j < nb - 1)
    ck = ck_ref[0, 0]
    cv = cv_ref[0, 0]
    for kvh in range(N_KV_HEADS):
        sl = slice(kvh * HEAD_DIM, (kvh + 1) * HEAD_DIM)
        segs = [(ck[:, sl], cv[:, sl], None),
                (kp[:, sl], vp_ref[:, sl], mask_p),
                (kc[:, sl], vc_ref[:, sl], None),
                (kn[:, sl], vn_ref[:, sl], mask_n)]
        og = _attend(_stack_heads(q, kvh), segs, _sink_col(sink_ref, kvh))
        _store_heads(o_ref, og, kvh)


def lat_attention(P, cache_k, cache_v, layer, sink, rope_tabs, B, L):
    nb = L // BLOCK
    kcol, vcol = COL_CK // KV_W, COL_CV // KV_W
    prev = lambda b, j: b * nb + jnp.maximum(j - 1, 0)
    nxt = lambda b, j: b * nb + jnp.minimum(j + 1, nb - 1)
    kv = lambda rowf, col: pl.BlockSpec((BLOCK, KV_W), lambda b, j: (rowf(b, j), col))
    cur = lambda b, j: b * nb + j
    tab = lambda f: pl.BlockSpec((BLOCK, LANES), f)
    tq = lambda b, j: (j, 0)
    tp = lambda b, j: (jnp.maximum(j - 1, 0), 0)
    tn = lambda b, j: (jnp.minimum(j + 1, nb - 1), 0)
    cos, sina, sinb = rope_tabs
    cache = pl.BlockSpec((1, 1, PAST_LEN, KV_W), lambda b, j: (b, layer, 0, 0))
    return pl.pallas_call(
        functools.partial(_lat_attn_kernel, nb=nb),
        grid=(B, nb),
        in_specs=[pl.BlockSpec(memory_space=pltpu.SMEM),
                  pl.BlockSpec((BLOCK, BRANCH_W), lambda b, j: (b * nb + j, COL_CQ // BRANCH_W)),
                  kv(prev, kcol), kv(cur, kcol), kv(nxt, kcol),
                  kv(prev, vcol), kv(cur, vcol), kv(nxt, vcol),
                  cache, cache,
                  tab(tq), tab(tq), tab(tq), tab(tp), tab(tp), tab(tp), tab(tn), tab(tn), tab(tn)],
        out_specs=pl.BlockSpec((BLOCK, BRANCH_W), lambda b, j: (b * nb + j, 0)),
        out_shape=jax.ShapeDtypeStruct((B * L, BRANCH_W), F32),
        compiler_params=_cparams(2),
        name="lat_attention",
    )(sink.reshape(N_Q_HEADS), P, P, P, P, P, P, P,
      cache_k.reshape(cache_k.shape[0], DEPTH, PAST_LEN, KV_W), cache_v.reshape(cache_v.shape[0], DEPTH, PAST_LEN, KV_W),
      cos, sina, sinb, cos, sina, sinb, cos, sina, sinb)


def rope_tables(L):
    t = jnp.arange(L)
    inv = ROPE_BASE ** (-jnp.arange(ROPE_NF, dtype=F32) / ROPE_NF)
    ang_r = (t // GRID_W).astype(F32)[:, None] * inv
    ang_c = (t % GRID_W).astype(F32)[:, None] * inv
    z = jnp.zeros((L, ROPE_NF), F32)
    cos = jnp.concatenate([jnp.cos(ang_r)] * 2 + [jnp.cos(ang_c)] * 2, axis=1)
    sina = jnp.concatenate([z, jnp.sin(ang_r), z, jnp.sin(ang_c)], axis=1)
    sinb = jnp.concatenate([-jnp.sin(ang_r), z, -jnp.sin(ang_c), z], axis=1)
    return tuple(jnp.tile(a, (1, LANES // HEAD_DIM)) for a in (cos, sina, sinb))


RWKV_CHUNK = 64
RWKV_PREP_CHUNKS = 4


def _split(a):
    hi = a.astype(BF16)
    return hi, (a - hi.astype(F32)).astype(BF16)


def _bdot3(a, b):
    ah, al = _split(a)
    bh, bl = _split(b)
    return jnp.dot(jnp.concatenate([ah, al, ah], axis=1), jnp.concatenate([bh, bh, bl], axis=0),
                   preferred_element_type=F32)


def _head_sum(x, head_ones):
    return _bdot3(x, head_ones)


def _unit_lower_inverse(ns, eye):
    xs = [eye + n for n in ns]
    pws = list(ns)
    for _ in range(5):
        pws = [_bdot(pw, pw) for pw in pws]
        xs = [x + _bdot(x, pw) for x, pw in zip(xs, pws)]
    resids = [eye - (x - _bdot3(n, x)) for n, x in zip(ns, xs)]
    return [x + _bdot(x, resid) for x, resid in zip(xs, resids)]


def _rwkv_kernel(xr_ref, xk_ref, xv_ref, lo_ref, mu_ref, w0_ref, w2_ref, a0_ref, a2_ref, g2_ref,
                 kk_ref, ka_ref, rk_ref, lnw_ref, lnb_ref, t0_ref,
                 y_ref, tfin_ref,
                 r_scr, kh_scr, v_scr, kn_scr, lw_scr, a_scr, kd_scr, yd_scr,
                 phi_scr, psi_scr, pcm_scr, rys_scr, gys_scr, *, L):
    C = RWKV_CHUNK
    nch = L // C
    lane = lax.broadcasted_iota(jnp.int32, (1, LANES), 1)
    m0 = (lane < RWKV_HEAD).astype(F32)
    m1 = 1.0 - m0
    head_ones = (lax.broadcasted_iota(jnp.int32, (LANES, LANES), 0) // RWKV_HEAD
                 == lax.broadcasted_iota(jnp.int32, (LANES, LANES), 1) // RWKV_HEAD).astype(F32)

    def shift(x, mu):
        return x + mu * (0.5 * (_shift_rows(x, L, True) + _shift_rows(x, L, False)) - x)

    r = shift(xr_ref[...], mu_ref[0:1, :])
    kh = shift(xk_ref[...], mu_ref[1:2, :])
    vh = shift(xv_ref[...], mu_ref[2:3, :])
    r_scr[...] = r
    kh_scr[...] = kh
    v_scr[...] = vh
    kk = kh * kk_ref[...]
    kn_scr[...] = kk * lax.rsqrt(_head_sum(kk * kk, head_ones) + 1e-12)
    lo = lo_ref[...]
    tw = jnp.tanh(lo)
    for d in range(2):
        z = w0_ref[d] + _bdot(tw, w2_ref[d])
        lw_scr[d] = -math.exp(-0.5) * _sigmoid(z)
        a = _sigmoid(a0_ref[d] + _bdot(lo, a2_ref[d]))
        a_scr[d] = a
        kd_scr[d] = kh * (1.0 + (a - 1.0) * ka_ref[...])

    ri = lax.broadcasted_iota(jnp.int32, (2 * C, 2 * C), 0)
    ci = lax.broadcasted_iota(jnp.int32, (2 * C, 2 * C), 1)
    same = (ri // C) == (ci // C)
    eye2 = (ri == ci).astype(F32)
    ti = lax.broadcasted_iota(jnp.int32, (C, C), 0)
    si = lax.broadcasted_iota(jnp.int32, (C, C), 1)
    rt, ct = ri % C, ci % C
    strict = (same & (ct < rt), same & (ct > rt))
    incl = (same & (ct <= rt), same & (ct >= rt))
    tri = ((si <= ti).astype(F32), (si >= ti).astype(F32))

    def stack(x):
        return jnp.concatenate([x * m0, x * m1], axis=0)

    def unstack(x):
        return x[0:C] + x[C:2 * C]

    def prepare(jobs):
        J = range(len(jobs))
        ds = [d for _, d in jobs]
        rows = [pl.ds(pl.multiple_of(c * C, C), C) for c, _ in jobs]
        lw = [lw_scr[ds[j], rows[j], :] for j in J]
        cum = [jnp.dot(tri[ds[j]], lw[j], preferred_element_type=F32, precision=HIGHEST) for j in J]
        p_in = [jnp.exp(cum[j]) for j in J]
        p_ex = [jnp.exp(cum[j] - lw[j]) for j in J]
        ip = [jnp.exp(-cum[j]) for j in J]
        pc = [p_in[j][C - 1:C, :] if ds[j] == 0 else p_in[j][0:1, :] for j in J]
        kn = [kn_scr[rows[j], :] for j in J]
        KAP = [stack(kn[j] * p_ex[j]) for j in J]
        RT = [stack(r_scr[rows[j], :] * p_in[j]) for j in J]
        BT = [stack(-(a_scr[ds[j], rows[j], :] * kn[j]) * ip[j]) for j in J]
        KT = [stack(kd_scr[ds[j], rows[j], :] * ip[j]) for j in J]
        V = [stack(v_scr[rows[j], :]) for j in J]
        G = [_bdot_nt(jnp.concatenate([KAP[j], RT[j]], axis=0), jnp.concatenate([BT[j], KT[j]], axis=0))
             for j in J]
        aab = [jnp.where(strict[ds[j]], G[j][0:2 * C, 0:2 * C], 0.0) for j in J]
        aak = [jnp.where(strict[ds[j]], G[j][0:2 * C, 2 * C:4 * C], 0.0) for j in J]
        grb = [jnp.where(incl[ds[j]], G[j][2 * C:4 * C, 0:2 * C], 0.0) for j in J]
        grk = [jnp.where(incl[ds[j]], G[j][2 * C:4 * C, 2 * C:4 * C], 0.0) for j in J]
        AV = [_bdot(aak[j], V[j]) for j in J]
        GV = [_bdot(grk[j], V[j]) for j in J]
        X = _unit_lower_inverse(aab, eye2)
        XK = [_bdot(X[j], KAP[j]) for j in J]
        XAV = [_bdot(X[j], AV[j]) for j in J]
        btp_t = [(BT[j] * pc[j]).T for j in J]
        ktp_t = [(KT[j] * pc[j]).T for j in J]
        phi = [_bdot(btp_t[j], XK[j]) for j in J]
        psi = [_bdot(btp_t[j], XAV[j]) + _bdot(ktp_t[j], V[j]) for j in J]
        rys = [unstack(RT[j] + _bdot(grb[j], XK[j])) for j in J]
        gys = [unstack(_bdot(grb[j], XAV[j]) + GV[j]) for j in J]
        for j, (c, d) in enumerate(jobs):
            phi_scr[d, c] = phi[j].astype(BF16)
            psi_scr[d, c] = psi[j]
            pcm_scr[d, c] = jnp.broadcast_to(pc[j], (LANES, LANES)).T
            rys_scr[d, c] = rys[j].astype(BF16)
            gys_scr[d, c] = gys[j]

    def prep_body(i, carry):
        prepare([(RWKV_PREP_CHUNKS * i + u, d) for u in range(RWKV_PREP_CHUNKS) for d in range(2)])
        return carry

    lax.fori_loop(0, nch // RWKV_PREP_CHUNKS, prep_body, 0)

    def seq_body(i, carry):
        out = []
        for d, T in enumerate(carry):
            c = i if d == 0 else nch - 1 - i
            rows = pl.ds(pl.multiple_of(c * C, C), C)
            tb = T.astype(BF16)
            yd_scr[d, rows, :] = jnp.dot(rys_scr[d, c], tb, preferred_element_type=F32) + gys_scr[d, c]
            out.append(pcm_scr[d, c] * T + jnp.dot(phi_scr[d, c], tb, preferred_element_type=F32) + psi_scr[d, c])
        return tuple(out)

    tf, tb_ = lax.fori_loop(0, nch, seq_body, (t0_ref[0, 0], t0_ref[0, 1]))
    tfin_ref[0, 0] = tf
    tfin_ref[0, 1] = tb_

    y = yd_scr[0] + yd_scr[1]
    inv_n = 1.0 / RWKV_HEAD
    mu = _head_sum(y, head_ones) * inv_n
    yc = y - mu
    var = _head_sum(yc * yc, head_ones) * inv_n
    yn = yc * lax.rsqrt(var + RWKV_GN_EPS) * lnw_ref[...] + lnb_ref[...]
    bonus = _head_sum(r * kh * rk_ref[...], head_ones) * vh
    g = _bdot(_sigmoid(lo), g2_ref[...])
    y_ref[...] = (yn + bonus) * g


def rwkv_mixer(P, prm, t0, B, L):
    HP = BRANCH_W // LANES
    nl = L
    col = lambda c0: (lambda b, h: (b, c0 // LANES + h))
    vecs = lambda n: pl.BlockSpec((n, LANES), lambda b, h: (0, h))
    dvec = pl.BlockSpec((2, 1, LANES), lambda b, h: (0, 0, h))
    lora_w = pl.BlockSpec((2, LORA_W, LANES), lambda b, h: (0, 0, h))
    seq = lambda: pltpu.VMEM((L, LANES), F32)
    seq2 = lambda: pltpu.VMEM((2, L, LANES), F32)
    nch = L // RWKV_CHUNK
    return pl.pallas_call(
        functools.partial(_rwkv_kernel, L=L),
        grid=(B, HP),
        in_specs=[pl.BlockSpec((nl, LANES), col(COL_DR)),
                  pl.BlockSpec((nl, LANES), col(COL_DK)),
                  pl.BlockSpec((nl, LANES), col(COL_DV)),
                  pl.BlockSpec((nl, LORA_W), lambda b, h: (b, COL_LORA // LORA_W)),
                  vecs(3), dvec, lora_w, dvec, lora_w,
                  pl.BlockSpec((LORA_W, LANES), lambda b, h: (0, h)),
                  vecs(1), vecs(1), vecs(1), vecs(1), vecs(1),
                  pl.BlockSpec((1, 2, LANES, LANES), lambda b, h: (b * HP + h, 0, 0, 0))],
        out_specs=[pl.BlockSpec((nl, LANES), lambda b, h: (b, h)),
                   pl.BlockSpec((1, 2, LANES, LANES), lambda b, h: (b * HP + h, 0, 0, 0))],
        out_shape=[jax.ShapeDtypeStruct((B * L, BRANCH_W), F32),
                   jax.ShapeDtypeStruct((B * HP, 2, LANES, LANES), F32)],
        scratch_shapes=[seq(), seq(), seq(), seq(), seq2(), seq2(), seq2(), seq2(),
                        pltpu.VMEM((2, nch, LANES, LANES), BF16), pltpu.VMEM((2, nch, LANES, LANES), F32),
                        pltpu.VMEM((2, nch, LANES, LANES), F32), pltpu.VMEM((2, nch, RWKV_CHUNK, LANES), BF16),
                        pltpu.VMEM((2, nch, RWKV_CHUNK, LANES), F32)],
        compiler_params=_cparams(2),
        name="rwkv_mixer",
    )(P, P, P, P, prm['mu'], prm['w0'], prm['w2'], prm['a0'], prm['a2'], prm['g2'],
      prm['k_k'], prm['k_a'], prm['r_k'], prm['ln_w'], prm['ln_b'], t0)


def _merge_kernel(ag_ref, hs_ref, ys_ref, yc_ref, yd_ref, m0_ref, m1_ref, m2_ref, m3_ref, x_ref, g1_ref,
                  gw_ref, gb_ref, bw_ref, wo_ref, o_ref):
    y_a = _gelu(ag_ref[...]) * hs_ref[...]
    y_s = _gelu(ys_ref[...])
    y_b = y_s * _sigmoid(_bdot(y_s, gw_ref[...]) + gb_ref[...])
    acc = None
    for n, (y, m_ref) in enumerate(((y_a, m0_ref), (y_b, m1_ref), (yc_ref[...], m2_ref), (yd_ref[...], m3_ref))):
        t = _sigmoid(m_ref[...]) * _bdot(y, bw_ref[n])
        acc = t if acc is None else acc + t
    o_ref[...] = x_ref[...] + g1_ref[0] * _bdot(acc, wo_ref[...])


def merge_out(P, hs, ysum, yc, yd, x, g1, glu_w, glu_b, branch_w, w_out, L, bm):
    M, D = x.shape
    W = BRANCH_W
    nb = g1.shape[0]
    assert L % bm == 0
    per = L // bm
    bidx = (lambda i: (i // per, 0, 0)) if nb > 1 else (lambda i: (0, 0, 0))
    row = lambda w, c: pl.BlockSpec((bm, w), lambda i: (i, c))
    const = lambda shape: pl.BlockSpec(shape, lambda i: (0,) * len(shape), pipeline_mode=pl.Buffered(1))
    mcol = COL_MERGE // D
    return pl.pallas_call(
        _merge_kernel,
        grid=(M // bm,),
        in_specs=[row(W, COL_AGATE // W), row(W, 0), row(W, 0), row(W, 0), row(W, 0),
                  row(D, mcol), row(D, mcol + 1), row(D, mcol + 2), row(D, mcol + 3),
                  row(D, 0), pl.BlockSpec((1, 1, D), bidx),
                  const((W, W)), const((1, W)), const((N_BRANCH, W, D)), const((D, D))],
        out_specs=row(D, 0),
        out_shape=jax.ShapeDtypeStruct((M, D), F32),
        compiler_params=_cparams(1),
        name="merge_out",
    )(P, hs, ysum, yc, yd, P, P, P, P, x, g1, glu_w, glu_b.reshape(1, W), branch_w, w_out)


def _shift_rows(g, L, back):
    bm = g.shape[0]
    rolled = pltpu.roll(g, 1 if back else bm - 1, 0)
    r8 = lax.broadcasted_iota(jnp.int32, (SUBLANES, 1), 0)
    pieces = []
    for s in range(0, bm, L):
        if back:
            pieces += [jnp.where(r8 == 0, 0.0, rolled[s:s + SUBLANES]), rolled[s + SUBLANES:s + L]]
        else:
            e = s + L - SUBLANES
            pieces += [rolled[s:e], jnp.where(r8 == SUBLANES - 1, 0.0, rolled[e:s + L])]
    return jnp.concatenate(pieces, axis=0)


def _shifted_chunk(ref, r0, n, L, back):
    r8 = lax.broadcasted_iota(jnp.int32, (SUBLANES, 1), 0)
    if back:
        if r0 % L == 0:
            body = pltpu.roll(ref[r0:r0 + n, :], 1, 0)
            return jnp.concatenate([jnp.where(r8 == 0, 0.0, body[0:SUBLANES]), body[SUBLANES:]], axis=0)
        return pltpu.roll(ref[r0 - SUBLANES:r0 + n, :], 1, 0)[SUBLANES:]
    if (r0 + n) % L == 0:
        body = pltpu.roll(ref[r0:r0 + n, :], n - 1, 0)
        return jnp.concatenate([body[:n - SUBLANES], jnp.where(r8 == SUBLANES - 1, 0.0, body[n - SUBLANES:])], axis=0)
    return pltpu.roll(ref[r0:r0 + n + SUBLANES, :], n + SUBLANES - 1, 0)[:n]


def _ffn_out_kernel(gc_ref, val_ref, cw_ref, cb_ref, w_ref, x_ref, g2_ref, gf_ref, o_ref, *, L, final):
    k = pl.program_id(1)
    last = pl.num_programs(1) - 1

    @pl.when(k == 0)
    def _():
        o_ref[...] = jnp.zeros_like(o_ref)

    n = min(FFN_ROW_CHUNK, L)
    gate_ref = gc_ref.at[0]
    for r0 in range(0, gate_ref.shape[0], n):
        rows = slice(r0, r0 + n)
        gate = (_shifted_chunk(gate_ref, r0, n, L, True) * cw_ref[0:1, :] + gate_ref[rows, :] * cw_ref[1:2, :]
                + _shifted_chunk(gate_ref, r0, n, L, False) * cw_ref[2:3, :] + cb_ref[...])
        half = 0.5 * gate
        act = (half + half * jnp.tanh(half)) * val_ref[0, rows, :]
        o_ref[rows, :] += jnp.dot(act.astype(BF16), w_ref[0], preferred_element_type=F32)

    @pl.when(k == last)
    def _():
        def rows_epilogue(r, carry):
            rows = pl.ds(pl.multiple_of(r * LANES, LANES), LANES)
            y = x_ref[rows, :] + g2_ref[0] * o_ref[rows, :]
            if final:
                y = y * lax.rsqrt(jnp.mean(y * y, axis=-1, keepdims=True) + EPS) * gf_ref[...]
            o_ref[rows, :] = y
            return carry

        lax.fori_loop(0, o_ref.shape[0] // LANES, rows_epilogue, 0)


def ffn_out(U, x, g2, conv_w, conv_b, w, l, gf, final, L, bm):
    M, D = x.shape
    bk = FFN_BLOCK
    nk = D_FF_PAD // bk
    nb = g2.shape[0]
    assert bm % L == 0 and (nb == 1 or bm == L)
    bidx = (lambda i, k: (i, 0, 0)) if nb > 1 else (lambda i, k: (0, 0, 0))
    return pl.pallas_call(
        functools.partial(_ffn_out_kernel, L=L, final=final),
        grid=(M // bm, nk),
        in_specs=[pl.BlockSpec((1, bm, bk), lambda i, k: (k, i, 0)),
                  pl.BlockSpec((1, bm, bk), lambda i, k: (nk + k, i, 0)),
                  pl.BlockSpec((3, bk), lambda i, k: (0, k)),
                  pl.BlockSpec((1, bk), lambda i, k: (0, k)),
                  pl.BlockSpec((1, bk, D), lambda i, k: (l, k, 0)),
                  pl.BlockSpec((bm, D), lambda i, k: (i, 0)),
                  pl.BlockSpec((1, 1, D), bidx),
                  pl.BlockSpec((1, D), lambda i, k: (0, 0))],
        out_specs=pl.BlockSpec((bm, D), lambda i, k: (i, 0)),
        out_shape=jax.ShapeDtypeStruct((M, D), F32),
        compiler_params=_cparams(2),
        name="ffn_out",
    )(U, U, conv_w, conv_b.reshape(1, D_FF_PAD), w, x, g2, gf.reshape(1, D))


def _block_diag(w):
    n, bs, _ = w.shape
    return jnp.einsum('ncd,nm->ncmd', w, jnp.eye(n, dtype=w.dtype)).reshape(n * bs, n * bs)


def _pad_rows(w, start, total):
    k = w.shape[-2]
    pad = [(0, 0)] * (w.ndim - 2) + [(start, total - start - k), (0, 0)]
    return jnp.pad(w, pad)


def prepare_layer(l, w):
    W = BRANCH_W
    p = {}
    p['lru_wa'] = jnp.stack([_block_diag(w['lru_wa'][l, d]) for d in range(2)]).astype(BF16)
    p['lru_wx'] = jnp.stack([_block_diag(w['lru_wx'][l, d]) for d in range(2)]).astype(BF16)
    p['lru_sp'] = jax.nn.softplus(-w['lru_lam'][l])
    p['s5'] = s5_params(w['s5_a_re'][l], w['s5_a_im'][l], w['s5_log_dt'][l], w['s5_b_re'][l], w['s5_b_im'][l],
                        w['s5_c_re'][l], w['s5_c_im'][l])
    p['rwkv'] = dict(
        mu=w['rwkv_mu'][l], w0=w['rwkv_w0'][l].reshape(2, 1, W), a0=w['rwkv_a0'][l].reshape(2, 1, W),
        w2=_pad_rows(w['rwkv_w2'][l], 0, LORA_W).astype(BF16),
        a2=_pad_rows(w['rwkv_a2'][l], W_LORA, LORA_W).astype(BF16),
        g2=_pad_rows(w['rwkv_g2'][l], W_LORA + A_LORA, LORA_W).astype(BF16),
        k_k=w['rwkv_k_k'][l].reshape(1, W), k_a=w['rwkv_k_a'][l].reshape(1, W),
        r_k=w['rwkv_r_k'][l].reshape(1, W), ln_w=w['rwkv_ln_w'][l].reshape(1, W), ln_b=w['rwkv_ln_b'][l].reshape(1, W))
    p['glu_w'] = w['s5_glu_w'][l].astype(BF16)
    p['branch_w'] = w['branch_w'][l].astype(BF16)
    p['w_out'] = w['w_out'][l].astype(BF16)
    fpad = D_FF_PAD - D_FF
    p['ffn_conv_w'] = jnp.pad(w['ffn_conv_w'][l], ((0, 0), (0, fpad)))
    p['ffn_conv_b'] = jnp.pad(w['ffn_conv_b'][l], (0, fpad))
    return p


def layer_step(x, mod, p, w, l, B, L, ctx, rope_tabs, bm):
    W = BRANCH_W
    D = D_MODEL
    lat = ctx is not None
    nb = mod.shape[0]
    sh1, sc1, g1, sh2, sc2, g2 = (mod[:, i * D:(i + 1) * D].reshape(nb, 1, D) for i in range(6))
    P = norm_mod_matmul(x, w['norm1'][l], sc1, sh1, w['w_in'], l, L, bm, 1024)

    lru0 = ctx['lru'].transpose(1, 0, 2) if lat else jnp.zeros((2, B, W), F32)
    hs, lru_fin = lru_mixer(P, w['lru_conv_w'][l], w['lru_conv_b'][l], p['lru_wa'], w['lru_ba'][l],
                            p['lru_wx'], w['lru_bx'][l], p['lru_sp'], lru0, B, L)

    gs = S5_NGROUP // S5_SLICES
    if lat:
        s50 = ctx['s5'].reshape(B, 2, 2, S5_SLICES, gs * S5_STATE).transpose(3, 1, 0, 2, 4).reshape(
            S5_SLICES, 2, B, 2 * S5_HALF)
    else:
        s50 = jnp.zeros((S5_SLICES, 2, B, 2 * S5_HALF), F32)
    wb, ar, ai, wc = p['s5']
    ysum, s5_fin = s5_mixer(P, wb, ar, ai, wc, w['s5_d'][l], s50, B, L)

    sink = w['attn_sink'][l]
    if lat:
        yc = lat_attention(P, ctx['k_all'], ctx['v_all'], l, sink, rope_tabs, B, L)
    else:
        yc = ctx_attention(P, sink, B, L)

    HP = W // LANES
    if lat:
        st = jnp.swapaxes(ctx['wkv'].reshape(B, 2, HP, 2, RWKV_HEAD, RWKV_HEAD), -1, -2)
        z = jnp.zeros_like(st[:, :, :, 0])
        t0 = jnp.concatenate([jnp.concatenate([st[:, :, :, 0], z], axis=-1),
                              jnp.concatenate([z, st[:, :, :, 1]], axis=-1)], axis=-2)
        t0 = t0.transpose(0, 2, 1, 3, 4).reshape(B * HP, 2, LANES, LANES)
    else:
        t0 = jnp.zeros((B * HP, 2, LANES, LANES), F32)
    yd, tfin = rwkv_mixer(P, p['rwkv'], t0, B, L)

    x = merge_out(P, hs, ysum, yc, yd, x, g1, p['glu_w'], w['s5_glu_b'][l], p['branch_w'], p['w_out'], L, 256)
    U = norm_mod_matmul(x, w['norm2'][l], sc2, sh2, w['ffn_w_in_b'], l, L, bm, FFN_BLOCK, tiled_out=True)
    x = ffn_out(U, x, g2, p['ffn_conv_w'], p['ffn_conv_b'], w['ffn_w_out_b'], l, w['norm_final'], l == DEPTH - 1,
                L, bm)

    new = None
    if not lat:
        k_new = P[:, COL_CK:COL_CK + KV_W].reshape(B, L, N_KV_HEADS, HEAD_DIM)
        v_new = P[:, COL_CV:COL_CV + KV_W].reshape(B, L, N_KV_HEADS, HEAD_DIM)
        lru_new = lru_fin.transpose(1, 0, 2)
        s5_new = s5_fin.reshape(S5_SLICES, 2, B, 2, gs, S5_STATE).transpose(2, 1, 3, 0, 4, 5).reshape(
            B, 2, 2, S5_NGROUP, S5_STATE)
        tf = tfin.reshape(B, HP, 2, LANES, LANES)
        heads = jnp.stack([tf[..., :RWKV_HEAD, :RWKV_HEAD], tf[..., RWKV_HEAD:, RWKV_HEAD:]], axis=3)
        wkv_new = jnp.swapaxes(heads, -1, -2).transpose(0, 2, 1, 3, 4, 5).reshape(
            B, 2, RWKV_NH, RWKV_HEAD, RWKV_HEAD)
        new = (k_new, v_new, lru_new, s5_new, wkv_new)
    return x, new


def kernel(x_prompt, x_sample, cache_k, cache_v, state_lru, state_s5, state_wkv, c, c_ctx, mod_w, mod_b, norm1, norm2, norm_final, w_in, lru_conv_w, lru_conv_b, lru_wa, lru_ba, lru_wx, lru_bx, lru_lam, s5_a_re, s5_a_im, s5_log_dt, s5_b_re, s5_b_im, s5_c_re, s5_c_im, s5_d, s5_glu_w, s5_glu_b, attn_sink, rwkv_mu, rwkv_w0, rwkv_w2, rwkv_a0, rwkv_a2, rwkv_g2, rwkv_k_k, rwkv_k_a, rwkv_r_k, rwkv_ln_w, rwkv_ln_b, branch_w, w_out, ffn_w_in, ffn_conv_w, ffn_conv_b, ffn_w_out):
    w = dict(norm1=norm1, norm2=norm2, norm_final=norm_final, w_in=w_in,
             lru_conv_w=lru_conv_w, lru_conv_b=lru_conv_b, lru_wa=lru_wa, lru_ba=lru_ba,
             lru_wx=lru_wx, lru_bx=lru_bx, lru_lam=lru_lam,
             s5_a_re=s5_a_re, s5_a_im=s5_a_im, s5_log_dt=s5_log_dt, s5_b_re=s5_b_re, s5_b_im=s5_b_im,
             s5_c_re=s5_c_re, s5_c_im=s5_c_im, s5_d=s5_d, s5_glu_w=s5_glu_w, s5_glu_b=s5_glu_b,
             attn_sink=attn_sink,
             rwkv_mu=rwkv_mu, rwkv_w0=rwkv_w0, rwkv_w2=rwkv_w2, rwkv_a0=rwkv_a0, rwkv_a2=rwkv_a2,
             rwkv_g2=rwkv_g2, rwkv_k_k=rwkv_k_k, rwkv_k_a=rwkv_k_a, rwkv_r_k=rwkv_r_k,
             rwkv_ln_w=rwkv_ln_w, rwkv_ln_b=rwkv_ln_b,
             branch_w=branch_w, w_out=w_out, ffn_w_in=ffn_w_in, ffn_conv_w=ffn_conv_w,
             ffn_conv_b=ffn_conv_b, ffn_w_out=ffn_w_out)
    Bc, Lc, D = x_prompt.shape
    Bl, Ll, _ = x_sample.shape
    nrow = 16
    c_all = jnp.concatenate([c_ctx[None, :], c, jnp.zeros((nrow - 1 - Bl, D), F32)], axis=0)
    mods = modulation(c_all, mod_w, mod_b)
    layers = [prepare_layer(l, w) for l in range(DEPTH)]
    w['ffn_w_in_b'], w['ffn_w_out_b'] = ffn_weights_bf16(ffn_w_in, ffn_w_out)
    rope_tabs = rope_tables(Ll)

    xp = x_prompt.reshape(Bc * Lc, D)
    news = []
    for l in range(DEPTH):
        xp, new = layer_step(xp, mods[l, 0:1], layers[l], w, l, Bc, Lc, None, None, 1024)
        news.append(new)
    y_prompt = xp.reshape(Bc, Lc, D)

    xs = x_sample.reshape(Bl * Ll, D)
    for l in range(DEPTH):
        ctx = dict(k_all=cache_k, v_all=cache_v, lru=state_lru[:, l], s5=state_s5[:, l], wkv=state_wkv[:, l])
        xs, _ = layer_step(xs, mods[l, 1:1 + Bl], layers[l], w, l, Bl, Ll, ctx, rope_tabs, min(1024, Ll))
    y_sample = xs.reshape(Bl, Ll, D)

    stack = lambda i: jnp.stack([n[i] for n in news], axis=1)
    return (y_prompt, y_sample, stack(0), stack(1), stack(2), stack(3), stack(4))
```

```python
import functools
import math

import jax
import jax.numpy as jnp
from jax import lax
from jax.experimental import pallas as pl
from jax.experimental.pallas import tpu as pltpu

F32 = jnp.float32
BF16 = jnp.bfloat16
HIGHEST = lax.Precision.HIGHEST

D_MODEL = 2048
DEPTH = 2
PAST_LEN = 256
GRID_W = 64
N_BRANCH = 4
BRANCH_W = 512
EPS = 1e-6
LRU_BLOCKS = 8
LRU_BS = 64
LRU_CONV_W = 4
LRU_PAD_L = 2
LRU_C = 8.0
S5_GROUP = 16
S5_NGROUP = 32
S5_STATE = 64
HEAD_DIM = 64
N_Q_HEADS = 8
N_KV_HEADS = 2
Q_PER_KV = 4
KV_W = 128
WINDOW = 128
BLOCK = 128
ROPE_BASE = 10000.0
ROPE_NF = 16
ATTN_SCALE = 1.0 / math.sqrt(HEAD_DIM)
NEG_INF = -1e30
RWKV_HEAD = 64
RWKV_NH = 8
W_LORA = 64
A_LORA = 64
G_LORA = 128
RWKV_GN_EPS = 64e-5
D_FF = 5504
FFN_BLOCK = 512
D_FF_PAD = 5632
FFN_PIECES = 4
FFN_TAIL = 3
FFN_ROW_CHUNK = 256
NMM_ROW_CHUNK = 256
LANES = 128
SUBLANES = 8
VMEM_LIMIT = 56 * 2 ** 20

COL_AX, COL_AGATE, COL_BU, COL_CQ = 0, 512, 1024, 1536
COL_CK, COL_CV = 2048, 2176
COL_DR, COL_DK, COL_DV, COL_LORA = 2304, 2816, 3328, 3840
COL_MERGE = 4096
D_IN = COL_MERGE + N_BRANCH * D_MODEL
LORA_W = W_LORA + A_LORA + G_LORA


def _cparams(n_grid):
    return pltpu.CompilerParams(dimension_semantics=("arbitrary",) * n_grid, vmem_limit_bytes=VMEM_LIMIT)


def _bdot(a, b):
    return jnp.dot(a.astype(BF16), b.astype(BF16), preferred_element_type=F32)


def _bdot_nt(a, b):
    return lax.dot_general(a.astype(BF16), b.astype(BF16), (((1,), (1,)), ((), ())),
                           preferred_element_type=F32)


def _gelu(x):
    return jax.nn.gelu(x)


def _sigmoid(x):
    return 0.5 * jnp.tanh(0.5 * x) + 0.5


def _mod_kernel(c_ref, w_ref, b_ref, o_ref):
    c = c_ref[...]
    sh, sl = _split(c * _sigmoid(c))
    wh, wl = _split(w_ref[0])
    dot = functools.partial(jnp.dot, preferred_element_type=F32)
    o_ref[0] = dot(sh, wh) + dot(sl, wh) + dot(sh, wl) + b_ref[0]


def modulation(c_all, mod_w, mod_b):
    R, D = c_all.shape
    N = mod_w.shape[2]
    bn = 1024
    return pl.pallas_call(
        _mod_kernel,
        grid=(DEPTH, N // bn),
        in_specs=[pl.BlockSpec((R, D), lambda l, j: (0, 0)),
                  pl.BlockSpec((1, D, bn), lambda l, j: (l, 0, j)),
                  pl.BlockSpec((1, 1, bn), lambda l, j: (l, 0, j))],
        out_specs=pl.BlockSpec((1, R, bn), lambda l, j: (l, 0, j)),
        out_shape=jax.ShapeDtypeStruct((DEPTH, R, N), F32),
        compiler_params=_cparams(2),
        name="modulation",
    )(c_all, mod_w, mod_b.reshape(DEPTH, 1, N))


def _nmm_kernel(x_ref, g_ref, sc_ref, sh_ref, w_ref, o_ref, h_scr):
    j = pl.program_id(1)
    out = o_ref.at[0] if len(o_ref.shape) == 3 else o_ref
    bm = h_scr.shape[0]

    @pl.when(j == 0)
    def _():
        w = w_ref[0].astype(BF16)
        for r0 in range(0, bm, NMM_ROW_CHUNK):
            rows = slice(r0, r0 + NMM_ROW_CHUNK)
            x = x_ref[rows, :]
            y = x * lax.rsqrt(jnp.mean(x * x, axis=-1, keepdims=True) + EPS) * g_ref[...]
            h = (y * (1.0 + sc_ref[0]) + sh_ref[0]).astype(BF16)
            h_scr[rows, :] = h
            out[rows, :] = jnp.dot(h, w, preferred_element_type=F32).astype(out.dtype)

    @pl.when(j > 0)
    def _():
        out[...] = jnp.dot(h_scr[...], w_ref[0].astype(BF16), preferred_element_type=F32).astype(out.dtype)


def norm_mod_matmul(x, g, sc, sh, w, l, L, bm, bn, tiled_out=False, out_dtype=F32):
    M, D = x.shape
    N = w.shape[2]
    nb = sc.shape[0]
    if nb > 1:
        assert L % bm == 0
        per = L // bm
        bidx = lambda i, j: (i // per, 0, 0)
    else:
        bidx = lambda i, j: (0, 0, 0)
    if tiled_out:
        out_spec = pl.BlockSpec((1, bm, bn), lambda i, j: (j, i, 0))
        out_shape = jax.ShapeDtypeStruct((N // bn, M, bn), out_dtype)
    else:
        out_spec = pl.BlockSpec((bm, bn), lambda i, j: (i, j))
        out_shape = jax.ShapeDtypeStruct((M, N), out_dtype)
    return pl.pallas_call(
        _nmm_kernel,
        grid=(M // bm, N // bn),
        in_specs=[pl.BlockSpec((bm, D), lambda i, j: (i, 0)),
                  pl.BlockSpec((1, D), lambda i, j: (0, 0)),
                  pl.BlockSpec((1, 1, D), bidx),
                  pl.BlockSpec((1, 1, D), bidx),
                  pl.BlockSpec((1, D, bn), lambda i, j: (l, 0, j))],
        out_specs=out_spec,
        out_shape=out_shape,
        scratch_shapes=[pltpu.VMEM((bm, D), BF16)],
        compiler_params=_cparams(2),
        name="norm_mod_matmul",
    )(x, g.reshape(1, D), sc, sh, w)


def _ffn_cast_kernel(w0_ref, w1_ref, w2_ref, w3_ref, o_ref, *, axis, ntile):
    t = pl.program_id(1)
    for q, w_ref in enumerate((w0_ref, w1_ref, w2_ref, w3_ref)):
        piece = w_ref[0].astype(BF16)
        if axis == 1:
            o_ref[0, :, q * LANES:(q + 1) * LANES] = piece
        else:
            o_ref[0, q * LANES:(q + 1) * LANES, :] = piece

    @pl.when(t % ntile == ntile - 1)
    def _():
        pad = FFN_BLOCK - FFN_TAIL * LANES
        if axis == 1:
            o_ref[0, :, FFN_TAIL * LANES:] = jnp.zeros((o_ref.shape[1], pad), BF16)
        else:
            o_ref[0, FFN_TAIL * LANES:, :] = jnp.zeros((pad, o_ref.shape[2]), BF16)


def ffn_weights_bf16(w_in, w_out):
    nl, D, _ = w_in.shape
    ntile = D_FF_PAD // FFN_BLOCK
    half_pieces = D_FF // LANES

    def col_piece(q):
        last = 2 * half_pieces - 1
        return pl.BlockSpec((1, D, LANES), lambda l, j: (
            l, 0, jnp.minimum((j // ntile) * half_pieces + (j % ntile) * FFN_PIECES + q, last)))

    def row_piece(q):
        return pl.BlockSpec((1, LANES, D), lambda l, k: (l, jnp.minimum(k * FFN_PIECES + q, half_pieces - 1), 0))

    w_in_b = pl.pallas_call(
        functools.partial(_ffn_cast_kernel, axis=1, ntile=ntile),
        grid=(nl, 2 * ntile),
        in_specs=[col_piece(q) for q in range(FFN_PIECES)],
        out_specs=pl.BlockSpec((1, D, FFN_BLOCK), lambda l, j: (l, 0, j)),
        out_shape=jax.ShapeDtypeStruct((nl, D, 2 * D_FF_PAD), BF16),
        compiler_params=_cparams(2),
        name="ffn_w_in_cast",
    )(w_in, w_in, w_in, w_in)
    w_out_b = pl.pallas_call(
        functools.partial(_ffn_cast_kernel, axis=0, ntile=ntile),
        grid=(nl, ntile),
        in_specs=[row_piece(q) for q in range(FFN_PIECES)],
        out_specs=pl.BlockSpec((1, FFN_BLOCK, D), lambda l, k: (l, k, 0)),
        out_shape=jax.ShapeDtypeStruct((nl, D_FF_PAD, D), BF16),
        compiler_params=_cparams(2),
        name="ffn_w_out_cast",
    )(w_out, w_out, w_out, w_out)
    return w_in_b, w_out_b


def _to_time_major(x, B, L):
    return x.reshape(B, L, x.shape[-1]).transpose(1, 0, 2).reshape(L * B, x.shape[-1])


def _to_batch_major(x, B, L):
    return x.reshape(L, B, x.shape[-1]).transpose(1, 0, 2).reshape(B * L, x.shape[-1])


def _lru_kernel(x_ref, cw_ref, cb_ref, wa_ref, ba_ref, wx_ref, bx_ref, sp_ref, h0_ref,
                o_ref, fin_ref, xp_ref, out_ref, a_scr, b_scr, *, B, L, Tc):
    nch = L // Tc
    R = Tc * B
    xp_ref[0:LRU_PAD_L * B, :] = jnp.zeros((LRU_PAD_L * B, LANES), F32)
    xp_ref[LRU_PAD_L * B:(LRU_PAD_L + L) * B, :] = _to_time_major(x_ref[...], B, L)
    tail = LRU_CONV_W - 1 - LRU_PAD_L
    xp_ref[(LRU_PAD_L + L) * B:, :] = jnp.zeros((tail * B, LANES), F32)
    out_ref[...] = jnp.zeros_like(out_ref)

    def chunk(ci, hs):
        bases = (pl.multiple_of(ci * R, R), pl.multiple_of((nch - 1 - ci) * R, R))
        for d in range(2):
            base = bases[d]
            xa = cb_ref[...] + xp_ref[pl.ds(base, R), :] * cw_ref[0:1, :]
            for j in range(1, LRU_CONV_W):
                xa = xa + xp_ref[pl.ds(base + j * B, R), :] * cw_ref[j:j + 1, :]
            r = _sigmoid(_bdot(xa, wa_ref[d]) + ba_ref[d])
            i = _sigmoid(_bdot(xa, wx_ref[d]) + bx_ref[d])
            a = jnp.exp(-LRU_C * r * sp_ref[d])
            a_scr[d] = a
            b_scr[d] = jnp.sqrt(1.0 - a * a) * (i * xa)

        def step(ti, hs):
            new = []
            for d in range(2):
                off = pl.multiple_of((ti if d == 0 else Tc - 1 - ti) * B, B)
                h = a_scr[d, pl.ds(off, B), :] * hs[d] + b_scr[d, pl.ds(off, B), :]
                row = pl.ds(bases[d] + off, B)
                out_ref[row, :] = out_ref[row, :] + h
                new.append(h)
            return tuple(new)

        return lax.fori_loop(0, Tc, step, hs, unroll=8)

    hf, hb = lax.fori_loop(0, nch, chunk, (h0_ref[0], h0_ref[1]))
    fin_ref[0] = hf
    fin_ref[1] = hb
    o_ref[...] = _to_batch_major(out_ref[...], B, L)


def lru_mixer(P, cw, cb, wa, ba, wx, bx, sp, h0, B, L):
    W = BRANCH_W
    Tc = min(128, L // 2)
    kern = functools.partial(_lru_kernel, B=B, L=L, Tc=Tc)
    vec = lambda: pl.BlockSpec((2, 1, LANES), lambda s: (0, 0, s))
    return pl.pallas_call(
        kern,
        grid=(W // LANES,),
        in_specs=[pl.BlockSpec((B * L, LANES), lambda s: (0, COL_AX // LANES + s)),
                  pl.BlockSpec((LRU_CONV_W, LANES), lambda s: (0, s)),
                  pl.BlockSpec((1, LANES), lambda s: (0, s)),
                  pl.BlockSpec((2, LANES, LANES), lambda s: (0, s, s)), vec(),
                  pl.BlockSpec((2, LANES, LANES), lambda s: (0, s, s)), vec(),
                  vec(),
                  pl.BlockSpec((2, B, LANES), lambda s: (0, 0, s))],
        out_specs=[pl.BlockSpec((L * B, LANES), lambda s: (0, s)),
                   pl.BlockSpec((2, B, LANES), lambda s: (0, 0, s))],
        out_shape=[jax.ShapeDtypeStruct((L * B, W), F32), jax.ShapeDtypeStruct((2, B, W), F32)],
        scratch_shapes=[pltpu.VMEM(((L + LRU_CONV_W - 1) * B, LANES), F32), pltpu.VMEM((L * B, LANES), F32),
                        pltpu.VMEM((2, Tc * B, LANES), F32), pltpu.VMEM((2, Tc * B, LANES), F32)],
        compiler_params=_cparams(1),
        name="lru_mixer",
    )(P, cw, cb.reshape(1, W), wa, ba.reshape(2, 1, W), wx, bx.reshape(2, 1, W), sp.reshape(2, 1, W), h0)


S5_SLICES = BRANCH_W // LANES
S5_HALF = (S5_NGROUP // S5_SLICES) * S5_STATE


def _s5_kernel(x_ref, wb_ref, ar_ref, ai_ref, wc_ref, dv_ref, h0_ref, o_ref, fin_ref, u_ref, out_ref, hs_scr,
               *, B, L, Tc):
    nch = L // Tc
    R = Tc * B
    H = S5_HALF
    u_ref[...] = _to_time_major(x_ref[...], B, L)
    out_ref[...] = u_ref[...] * dv_ref[...]
    ab = [(ar_ref[0, d], ai_ref[0, d]) for d in range(2)]

    def chunk(ci, carry):
        bases = (pl.multiple_of(ci * R, R), pl.multiple_of((nch - 1 - ci) * R, R))
        for d in range(2):
            hs_scr[d] = _bdot(u_ref[pl.ds(bases[d], R), :], wb_ref[0, d])

        def step(ti, carry):
            new = []
            for d in range(2):
                hr, hi = carry[d]
                ar, ai = ab[d]
                row = pl.ds(pl.multiple_of((ti if d == 0 else Tc - 1 - ti) * B, B), B)
                nr = ar * hr - ai * hi + hs_scr[d, row, 0:H]
                ni = ar * hi + ai * hr + hs_scr[d, row, H:2 * H]
                hs_scr[d, row, 0:H] = nr
                hs_scr[d, row, H:2 * H] = ni
                new.append((nr, ni))
            return tuple(new)

        carry = lax.fori_loop(0, Tc, step, carry, unroll=2)
        for d in range(2):
            rows = pl.ds(bases[d], R)
            out_ref[rows, :] = out_ref[rows, :] + _bdot(hs_scr[d], wc_ref[0, d])
        return carry

    init = tuple((h0_ref[0, d, :, 0:H], h0_ref[0, d, :, H:2 * H]) for d in range(2))
    fin = lax.fori_loop(0, nch, chunk, init)
    for d in range(2):
        fin_ref[0, d, :, 0:H] = fin[d][0]
        fin_ref[0, d, :, H:2 * H] = fin[d][1]
    o_ref[...] = _to_batch_major(out_ref[...], B, L)


def s5_mixer(P, wb, ar, ai, wc, dvec, h0, B, L):
    W = BRANCH_W
    Tc = min(128, L // 2, 1024 // B)
    H2 = 2 * S5_HALF
    kern = functools.partial(_s5_kernel, B=B, L=L, Tc=Tc)
    return pl.pallas_call(
        kern,
        grid=(S5_SLICES,),
        in_specs=[pl.BlockSpec((B * L, LANES), lambda s: (0, COL_BU // LANES + s)),
                  pl.BlockSpec((1, 2, LANES, H2), lambda s: (s, 0, 0, 0)),
                  pl.BlockSpec((1, 2, 1, S5_HALF), lambda s: (s, 0, 0, 0)),
                  pl.BlockSpec((1, 2, 1, S5_HALF), lambda s: (s, 0, 0, 0)),
                  pl.BlockSpec((1, 2, H2, LANES), lambda s: (s, 0, 0, 0)),
                  pl.BlockSpec((1, LANES), lambda s: (0, s)),
                  pl.BlockSpec((1, 2, B, H2), lambda s: (s, 0, 0, 0))],
        out_specs=[pl.BlockSpec((L * B, LANES), lambda s: (0, s)),
                   pl.BlockSpec((1, 2, B, H2), lambda s: (s, 0, 0, 0))],
        out_shape=[jax.ShapeDtypeStruct((L * B, W), F32), jax.ShapeDtypeStruct((S5_SLICES, 2, B, H2), F32)],
        scratch_shapes=[pltpu.VMEM((L * B, LANES), F32), pltpu.VMEM((L * B, LANES), F32),
                        pltpu.VMEM((2, Tc * B, H2), F32)],
        compiler_params=_cparams(1),
        name="s5_mixer",
    )(P, wb, ar, ai, wc, dvec.reshape(1, W), h0)


def s5_params(a_re, a_im, log_dt, b_re, b_im, c_re, c_im):
    dt = jnp.exp(log_dt)[..., None]
    mag = jnp.exp(a_re * dt)
    abar_r = mag * jnp.cos(a_im * dt)
    abar_i = mag * jnp.sin(a_im * dt)
    den = a_re * a_re + a_im * a_im
    nr = abar_r - 1.0
    fr = (nr * a_re + abar_i * a_im) / den
    fi = (abar_i * a_re - nr * a_im) / den
    bbar_r = fr[..., None] * b_re - fi[..., None] * b_im
    bbar_i = fr[..., None] * b_im + fi[..., None] * b_re
    gs = S5_NGROUP // S5_SLICES
    eye = jnp.eye(gs, dtype=F32)
    bb = jnp.stack([bbar_r, bbar_i], axis=0).reshape(2, 2, S5_SLICES, gs, S5_STATE, S5_GROUP)
    wb = jnp.einsum('rdsgnp,gh->sdgprhn', bb, eye).reshape(S5_SLICES, 2, LANES, 2 * S5_HALF)
    cc = jnp.stack([c_re, -c_im], axis=0).reshape(2, 2, S5_SLICES, gs, S5_GROUP, S5_STATE)
    wc = jnp.einsum('rdsgpn,gh->sdrhngp', cc, eye).reshape(S5_SLICES, 2, 2 * S5_HALF, LANES)
    ar = abar_r.reshape(2, S5_SLICES, 1, S5_HALF).transpose(1, 0, 2, 3)
    ai = abar_i.reshape(2, S5_SLICES, 1, S5_HALF).transpose(1, 0, 2, 3)
    return wb.astype(BF16), ar, ai, wc.astype(BF16)


def _rope(x, cos, sina, sinb):
    return x * cos + pltpu.roll(x, ROPE_NF, 1) * sina + pltpu.roll(x, LANES - ROPE_NF, 1) * sinb


def _attend(qg, segs, sink_col):
    def lane_tiles(a):
        return [a[:, c * LANES:(c + 1) * LANES] for c in range(a.shape[1] // LANES)]

    def tree(op, xs):
        while len(xs) > 1:
            xs = [op(xs[i], xs[i + 1]) if i + 1 < len(xs) else xs[i] for i in range(0, len(xs), 2)]
        return xs[0]

    scores = []
    for k, _, mask in segs:
        s = _bdot_nt(qg, k) * ATTN_SCALE
        if mask is not None:
            s = jnp.where(mask, s, NEG_INF)
        scores.append(s)
    m = jnp.maximum(sink_col, jnp.max(tree(jnp.maximum, [t for s in scores for t in lane_tiles(s)]),
                                      axis=-1, keepdims=True))
    ps = [jnp.exp(s - m) for s in scores]
    den = jnp.exp(sink_col - m) + jnp.sum(tree(jnp.add, [t for p in ps for t in lane_tiles(p)]),
                                          axis=-1, keepdims=True)
    o = tree(jnp.add, [_bdot(p, v) for p, (_, v, _) in zip(ps, segs)])
    return o * (1.0 / den)


def _sink_col(sink_ref, kvh):
    return jnp.concatenate([jnp.full((BLOCK, 1), sink_ref[kvh * Q_PER_KV + g], F32) for g in range(Q_PER_KV)], axis=0)


def _stack_heads(q, kvh):
    return jnp.concatenate([q[:, (kvh * Q_PER_KV + g) * HEAD_DIM:(kvh * Q_PER_KV + g + 1) * HEAD_DIM]
                            for g in range(Q_PER_KV)], axis=0)


def _store_heads(o_ref, og, kvh):
    for g in range(Q_PER_KV):
        h = kvh * Q_PER_KV + g
        o_ref[:, h * HEAD_DIM:(h + 1) * HEAD_DIM] = og[g * BLOCK:(g + 1) * BLOCK]


def _ctx_attn_kernel(sink_ref, q_ref, k_ref, v_ref, o_ref):
    q = q_ref[...]
    k = k_ref[...]
    v = v_ref[...]
    for kvh in range(N_KV_HEADS):
        ks = k[:, kvh * HEAD_DIM:(kvh + 1) * HEAD_DIM]
        vs = v[:, kvh * HEAD_DIM:(kvh + 1) * HEAD_DIM]
        og = _attend(_stack_heads(q, kvh), [(ks, vs, None)], _sink_col(sink_ref, kvh))
        _store_heads(o_ref, og, kvh)


def ctx_attention(P, sink, B, L):
    nb = L // BLOCK
    return pl.pallas_call(
        _ctx_attn_kernel,
        grid=(B, nb),
        in_specs=[pl.BlockSpec(memory_space=pltpu.SMEM),
                  pl.BlockSpec((BLOCK, BRANCH_W), lambda b, j: (b * nb + j, COL_CQ // BRANCH_W)),
                  pl.BlockSpec((L, KV_W), lambda b, j: (b, COL_CK // KV_W)),
                  pl.BlockSpec((L, KV_W), lambda b, j: (b, COL_CV // KV_W))],
        out_specs=pl.BlockSpec((BLOCK, BRANCH_W), lambda b, j: (b * nb + j, 0)),
        out_shape=jax.ShapeDtypeStruct((B * L, BRANCH_W), F32),
        compiler_params=_cparams(2),
        name="ctx_attention",
    )(sink.reshape(N_Q_HEADS), P, P, P)


def _lat_attn_kernel(sink_ref, q_ref, kp_ref, kc_ref, kn_ref, vp_ref, vc_ref, vn_ref, ck_ref, cv_ref,
                     cq_ref, saq_ref, sbq_ref, cp_ref, sap_ref, sbp_ref, cn_ref, san_ref, sbn_ref, o_ref, *, nb):
    j = pl.program_id(1)
    cq, saq, sbq = cq_ref[...], saq_ref[...], sbq_ref[...]
    q = jnp.concatenate([_rope(q_ref[:, c * LANES:(c + 1) * LANES], cq, saq, sbq)
                         for c in range(BRANCH_W // LANES)], axis=1)
    kp = _rope(kp_ref[...], cp_ref[...], sap_ref[...], sbp_ref[...])
    kc = _rope(kc_ref[...], cq, saq, sbq)
    kn = _rope(kn_ref[...], cn_ref[...], san_ref[...], sbn_ref[...])
    R = Q_PER_KV * BLOCK
    r = lax.broadcasted_iota(jnp.int32, (R, BLOCK), 0) % BLOCK
    c = lax.broadcasted_iota(jnp.int32, (R, BLOCK), 1)
    mask_p = (c >= r) & (j > 0)
    mask_n = (c <= r) & (j < nb - 1)
    ck = ck_ref[0, 0]
    cv = cv_ref[0, 0]
    for kvh in range(N_KV_HEADS):
        sl = slice(kvh * HEAD_DIM, (kvh + 1) * HEAD_DIM)
        segs = [(ck[:, sl], cv[:, sl], None),
                (kp[:, sl], vp_ref[:, sl], mask_p),
                (kc[:, sl], vc_ref[:, sl], None),
                (kn[:, sl], vn_ref[:, sl], mask_n)]
        og = _attend(_stack_heads(q, kvh), segs, _sink_col(sink_ref, kvh))
        _store_heads(o_ref, og, kvh)


def lat_attention(P, cache_k, cache_v, layer, sink, rope_tabs, B, L):
    nb = L // BLOCK
    kcol, vcol = COL_CK // KV_W, COL_CV // KV_W
    prev = lambda b, j: b * nb + jnp.maximum(j - 1, 0)
    nxt = lambda b, j: b * nb + jnp.minimum(j + 1, nb - 1)
    kv = lambda rowf, col: pl.BlockSpec((BLOCK, KV_W), lambda b, j: (rowf(b, j), col))
    cur = lambda b, j: b * nb + j
    tab = lambda f: pl.BlockSpec((BLOCK, LANES), f)
    tq = lambda b, j: (j, 0)
    tp = lambda b, j: (jnp.maximum(j - 1, 0), 0)
    tn = lambda b, j: (jnp.minimum(j + 1, nb - 1), 0)
    cos, sina, sinb = rope_tabs
    cache = pl.BlockSpec((1, 1, PAST_LEN, KV_W), lambda b, j: (b, layer, 0, 0))
    return pl.pallas_call(
        functools.partial(_lat_attn_kernel, nb=nb),
        grid=(B, nb),
        in_specs=[pl.BlockSpec(memory_space=pltpu.SMEM),
                  pl.BlockSpec((BLOCK, BRANCH_W), lambda b, j: (b * nb + j, COL_CQ // BRANCH_W)),
                  kv(prev, kcol), kv(cur, kcol), kv(nxt, kcol),
                  kv(prev, vcol), kv(cur, vcol), kv(nxt, vcol),
                  cache, cache,
                  tab(tq), tab(tq), tab(tq), tab(tp), tab(tp), tab(tp), tab(tn), tab(tn), tab(tn)],
        out_specs=pl.BlockSpec((BLOCK, BRANCH_W), lambda b, j: (b * nb + j, 0)),
        out_shape=jax.ShapeDtypeStruct((B * L, BRANCH_W), F32),
        compiler_params=_cparams(2),
        name="lat_attention",
    )(sink.reshape(N_Q_HEADS), P, P, P, P, P, P, P,
      cache_k.reshape(cache_k.shape[0], DEPTH, PAST_LEN, KV_W), cache_v.reshape(cache_v.shape[0], DEPTH, PAST_LEN, KV_W),
      cos, sina, sinb, cos, sina, sinb, cos, sina, sinb)


def rope_tables(L):
    t = jnp.arange(L)
    inv = ROPE_BASE ** (-jnp.arange(ROPE_NF, dtype=F32) / ROPE_NF)
    ang_r = (t // GRID_W).astype(F32)[:, None] * inv
    ang_c = (t % GRID_W).astype(F32)[:, None] * inv
    z = jnp.zeros((L, ROPE_NF), F32)
    cos = jnp.concatenate([jnp.cos(ang_r)] * 2 + [jnp.cos(ang_c)] * 2, axis=1)
    sina = jnp.concatenate([z, jnp.sin(ang_r), z, jnp.sin(ang_c)], axis=1)
    sinb = jnp.concatenate([-jnp.sin(ang_r), z, -jnp.sin(ang_c), z], axis=1)
    return tuple(jnp.tile(a, (1, LANES // HEAD_DIM)) for a in (cos, sina, sinb))


RWKV_CHUNK = 64
RWKV_PREP_CHUNKS = 4


def _split(a):
    hi = a.astype(BF16)
    return hi, (a - hi.astype(F32)).astype(BF16)


def _bdot3(a, b):
    ah, al = _split(a)
    bh, bl = _split(b)
    return jnp.dot(jnp.concatenate([ah, al, ah], axis=1), jnp.concatenate([bh, bh, bl], axis=0),
                   preferred_element_type=F32)


def _head_sum(x, head_ones):
    return _bdot3(x, head_ones)


def _unit_lower_inverse(ns, eye):
    xs = [eye + n for n in ns]
    pws = list(ns)
    for _ in range(5):
        pws = [_bdot(pw, pw) for pw in pws]
        xs = [x + _bdot(x, pw) for x, pw in zip(xs, pws)]
    resids = [eye - (x - _bdot3(n, x)) for n, x in zip(ns, xs)]
    return [x + _bdot(x, resid) for x, resid in zip(xs, resids)]


def _rwkv_kernel(xr_ref, xk_ref, xv_ref, lo_ref, mu_ref, w0_ref, w2_ref, a0_ref, a2_ref, g2_ref,
                 kk_ref, ka_ref, rk_ref, lnw_ref, lnb_ref, t0_ref,
                 y_ref, tfin_ref,
                 r_scr, kh_scr, v_scr, kn_scr, lw_scr, a_scr, kd_scr, yd_scr,
                 phi_scr, psi_scr, pcm_scr, rys_scr, gys_scr, *, L):
    C = RWKV_CHUNK
    nch = L // C
    lane = lax.broadcasted_iota(jnp.int32, (1, LANES), 1)
    m0 = (lane < RWKV_HEAD).astype(F32)
    m1 = 1.0 - m0
    head_ones = (lax.broadcasted_iota(jnp.int32, (LANES, LANES), 0) // RWKV_HEAD
                 == lax.broadcasted_iota(jnp.int32, (LANES, LANES), 1) // RWKV_HEAD).astype(F32)

    def shift(x, mu):
        return x + mu * (0.5 * (_shift_rows(x, L, True) + _shift_rows(x, L, False)) - x)

    r = shift(xr_ref[...], mu_ref[0:1, :])
    kh = shift(xk_ref[...], mu_ref[1:2, :])
    vh = shift(xv_ref[...], mu_ref[2:3, :])
    r_scr[...] = r
    kh_scr[...] = kh
    v_scr[...] = vh
    kk = kh * kk_ref[...]
    kn_scr[...] = kk * lax.rsqrt(_head_sum(kk * kk, head_ones) + 1e-12)
    lo = lo_ref[...]
    tw = jnp.tanh(lo)
    for d in range(2):
        z = w0_ref[d] + _bdot(tw, w2_ref[d])
        lw_scr[d] = -math.exp(-0.5) * _sigmoid(z)
        a = _sigmoid(a0_ref[d] + _bdot(lo, a2_ref[d]))
        a_scr[d] = a
        kd_scr[d] = kh * (1.0 + (a - 1.0) * ka_ref[...])

    ri = lax.broadcasted_iota(jnp.int32, (2 * C, 2 * C), 0)
    ci = lax.broadcasted_iota(jnp.int32, (2 * C, 2 * C), 1)
    same = (ri // C) == (ci // C)
    eye2 = (ri == ci).astype(F32)
    ti = lax.broadcasted_iota(jnp.int32, (C, C), 0)
    si = lax.broadcasted_iota(jnp.int32, (C, C), 1)
    rt, ct = ri % C, ci % C
    strict = (same & (ct < rt), same & (ct > rt))
    incl = (same & (ct <= rt), same & (ct >= rt))
    tri = ((si <= ti).astype(F32), (si >= ti).astype(F32))

    def stack(x):
        return jnp.concatenate([x * m0, x * m1], axis=0)

    def unstack(x):
        return x[0:C] + x[C:2 * C]

    def prepare(jobs):
        J = range(len(jobs))
        ds = [d for _, d in jobs]
        rows = [pl.ds(pl.multiple_of(c * C, C), C) for c, _ in jobs]
        lw = [lw_scr[ds[j], rows[j], :] for j in J]
        cum = [jnp.dot(tri[ds[j]], lw[j], preferred_element_type=F32, precision=HIGHEST) for j in J]
        p_in = [jnp.exp(cum[j]) for j in J]
        p_ex = [jnp.exp(cum[j] - lw[j]) for j in J]
        ip = [jnp.exp(-cum[j]) for j in J]
        pc = [p_in[j][C - 1:C, :] if ds[j] == 0 else p_in[j][0:1, :] for j in J]
        kn = [kn_scr[rows[j], :] for j in J]
        KAP = [stack(kn[j] * p_ex[j]) for j in J]
        RT = [stack(r_scr[rows[j], :] * p_in[j]) for j in J]
        BT = [stack(-(a_scr[ds[j], rows[j], :] * kn[j]) * ip[j]) for j in J]
        KT = [stack(kd_scr[ds[j], rows[j], :] * ip[j]) for j in J]
        V = [stack(v_scr[rows[j], :]) for j in J]
        G = [_bdot_nt(jnp.concatenate([KAP[j], RT[j]], axis=0), jnp.concatenate([BT[j], KT[j]], axis=0))
             for j in J]
        aab = [jnp.where(strict[ds[j]], G[j][0:2 * C, 0:2 * C], 0.0) for j in J]
        aak = [jnp.where(strict[ds[j]], G[j][0:2 * C, 2 * C:4 * C], 0.0) for j in J]
        grb = [jnp.where(incl[ds[j]], G[j][2 * C:4 * C, 0:2 * C], 0.0) for j in J]
        grk = [jnp.where(incl[ds[j]], G[j][2 * C:4 * C, 2 * C:4 * C], 0.0) for j in J]
        AV = [_bdot(aak[j], V[j]) for j in J]
        GV = [_bdot(grk[j], V[j]) for j in J]
        X = _unit_lower_inverse(aab, eye2)
        XK = [_bdot(X[j], KAP[j]) for j in J]
        XAV = [_bdot(X[j], AV[j]) for j in J]
        btp_t = [(BT[j] * pc[j]).T for j in J]
        ktp_t = [(KT[j] * pc[j]).T for j in J]
        phi = [_bdot(btp_t[j], XK[j]) for j in J]
        psi = [_bdot(btp_t[j], XAV[j]) + _bdot(ktp_t[j], V[j]) for j in J]
        rys = [unstack(RT[j] + _bdot(grb[j], XK[j])) for j in J]
        gys = [unstack(_bdot(grb[j], XAV[j]) + GV[j]) for j in J]
        for j, (c, d) in enumerate(jobs):
            phi_scr[d, c] = phi[j].astype(BF16)
            psi_scr[d, c] = psi[j]
            pcm_scr[d, c] = jnp.broadcast_to(pc[j], (LANES, LANES)).T
            rys_scr[d, c] = rys[j].astype(BF16)
            gys_scr[d, c] = gys[j]

    def prep_body(i, carry):
        prepare([(RWKV_PREP_CHUNKS * i + u, d) for u in range(RWKV_PREP_CHUNKS) for d in range(2)])
        return carry

    lax.fori_loop(0, nch // RWKV_PREP_CHUNKS, prep_body, 0)

    def seq_body(i, carry):
        out = []
        for d, T in enumerate(carry):
            c = i if d == 0 else nch - 1 - i
            rows = pl.ds(pl.multiple_of(c * C, C), C)
            tb = T.astype(BF16)
            yd_scr[d, rows, :] = jnp.dot(rys_scr[d, c], tb, preferred_element_type=F32) + gys_scr[d, c]
            out.append(pcm_scr[d, c] * T + jnp.dot(phi_scr[d, c], tb, preferred_element_type=F32) + psi_scr[d, c])
        return tuple(out)

    tf, tb_ = lax.fori_loop(0, nch, seq_body, (t0_ref[0, 0], t0_ref[0, 1]))
    tfin_ref[0, 0] = tf
    tfin_ref[0, 1] = tb_

    y = yd_scr[0] + yd_scr[1]
    inv_n = 1.0 / RWKV_HEAD
    mu = _head_sum(y, head_ones) * inv_n
    yc = y - mu
    var = _head_sum(yc * yc, head_ones) * inv_n
    yn = yc * lax.rsqrt(var + RWKV_GN_EPS) * lnw_ref[...] + lnb_ref[...]
    bonus = _head_sum(r * kh * rk_ref[...], head_ones) * vh
    g = _bdot(_sigmoid(lo), g2_ref[...])
    y_ref[...] = (yn + bonus) * g


def rwkv_mixer(P, prm, t0, B, L):
    HP = BRANCH_W // LANES
    nl = L
    col = lambda c0: (lambda b, h: (b, c0 // LANES + h))
    vecs = lambda n: pl.BlockSpec((n, LANES), lambda b, h: (0, h))
    dvec = pl.BlockSpec((2, 1, LANES), lambda b, h: (0, 0, h))
    lora_w = pl.BlockSpec((2, LORA_W, LANES), lambda b, h: (0, 0, h))
    seq = lambda: pltpu.VMEM((L, LANES), F32)
    seq2 = lambda: pltpu.VMEM((2, L, LANES), F32)
    nch = L // RWKV_CHUNK
    return pl.pallas_call(
        functools.partial(_rwkv_kernel, L=L),
        grid=(B, HP),
        in_specs=[pl.BlockSpec((nl, LANES), col(COL_DR)),
                  pl.BlockSpec((nl, LANES), col(COL_DK)),
                  pl.BlockSpec((nl, LANES), col(COL_DV)),
                  pl.BlockSpec((nl, LORA_W), lambda b, h: (b, COL_LORA // LORA_W)),
                  vecs(3), dvec, lora_w, dvec, lora_w,
                  pl.BlockSpec((LORA_W, LANES), lambda b, h: (0, h)),
                  vecs(1), vecs(1), vecs(1), vecs(1), vecs(1),
                  pl.BlockSpec((1, 2, LANES, LANES), lambda b, h: (b * HP + h, 0, 0, 0))],
        out_specs=[pl.BlockSpec((nl, LANES), lambda b, h: (b, h)),
                   pl.BlockSpec((1, 2, LANES, LANES), lambda b, h: (b * HP + h, 0, 0, 0))],
        out_shape=[jax.ShapeDtypeStruct((B * L, BRANCH_W), F32),
                   jax.ShapeDtypeStruct((B * HP, 2, LANES, LANES), F32)],
        scratch_shapes=[seq(), seq(), seq(), seq(), seq2(), seq2(), seq2(), seq2(),
                        pltpu.VMEM((2, nch, LANES, LANES), BF16), pltpu.VMEM((2, nch, LANES, LANES), F32),
                        pltpu.VMEM((2, nch, LANES, LANES), F32), pltpu.VMEM((2, nch, RWKV_CHUNK, LANES), BF16),
                        pltpu.VMEM((2, nch, RWKV_CHUNK, LANES), F32)],
        compiler_params=_cparams(2),
        name="rwkv_mixer",
    )(P, P, P, P, prm['mu'], prm['w0'], prm['w2'], prm['a0'], prm['a2'], prm['g2'],
      prm['k_k'], prm['k_a'], prm['r_k'], prm['ln_w'], prm['ln_b'], t0)


def _merge_kernel(ag_ref, hs_ref, ys_ref, yc_ref, yd_ref, m0_ref, m1_ref, m2_ref, m3_ref, x_ref, g1_ref,
                  gw_ref, gb_ref, bw_ref, wo_ref, o_ref):
    y_a = _gelu(ag_ref[...]) * hs_ref[...]
    y_s = _gelu(ys_ref[...])
    y_b = y_s * _sigmoid(_bdot(y_s, gw_ref[...]) + gb_ref[...])
    acc = None
    for n, (y, m_ref) in enumerate(((y_a, m0_ref), (y_b, m1_ref), (yc_ref[...], m2_ref), (yd_ref[...], m3_ref))):
        t = _sigmoid(m_ref[...]) * _bdot(y, bw_ref[n])
        acc = t if acc is None else acc + t
    o_ref[...] = x_ref[...] + g1_ref[0] * _bdot(acc, wo_ref[...])


def merge_out(P, hs, ysum, yc, yd, x, g1, glu_w, glu_b, branch_w, w_out, L, bm):
    M, D = x.shape
    W = BRANCH_W
    nb = g1.shape[0]
    assert L % bm == 0
    per = L // bm
    bidx = (lambda i: (i // per, 0, 0)) if nb > 1 else (lambda i: (0, 0, 0))
    row = lambda w, c: pl.BlockSpec((bm, w), lambda i: (i, c))
    const = lambda shape: pl.BlockSpec(shape, lambda i: (0,) * len(shape), pipeline_mode=pl.Buffered(1))
    mcol = COL_MERGE // D
    return pl.pallas_call(
        _merge_kernel,
        grid=(M // bm,),
        in_specs=[row(W, COL_AGATE // W), row(W, 0), row(W, 0), row(W, 0), row(W, 0),
                  row(D, mcol), row(D, mcol + 1), row(D, mcol + 2), row(D, mcol + 3),
                  row(D, 0), pl.BlockSpec((1, 1, D), bidx),
                  const((W, W)), const((1, W)), const((N_BRANCH, W, D)), const((D, D))],
        out_specs=row(D, 0),
        out_shape=jax.ShapeDtypeStruct((M, D), F32),
        compiler_params=_cparams(1),
        name="merge_out",
    )(P, hs, ysum, yc, yd, P, P, P, P, x, g1, glu_w, glu_b.reshape(1, W), branch_w, w_out)


def _shift_rows(g, L, back):
    bm = g.shape[0]
    rolled = pltpu.roll(g, 1 if back else bm - 1, 0)
    r8 = lax.broadcasted_iota(jnp.int32, (SUBLANES, 1), 0)
    pieces = []
    for s in range(0, bm, L):
        if back:
            pieces += [jnp.where(r8 == 0, 0.0, rolled[s:s + SUBLANES]), rolled[s + SUBLANES:s + L]]
        else:
            e = s + L - SUBLANES
            pieces += [rolled[s:e], jnp.where(r8 == SUBLANES - 1, 0.0, rolled[e:s + L])]
    return jnp.concatenate(pieces, axis=0)


def _shifted_chunk(ref, r0, n, L, back):
    r8 = lax.broadcasted_iota(jnp.int32, (SUBLANES, 1), 0)
    halo = SUBLANES * (4 // ref.dtype.itemsize)
    load = lambda a, b: ref[a:b, :].astype(F32)
    if back:
        if r0 % L == 0:
            body = pltpu.roll(load(r0, r0 + n), 1, 0)
            return jnp.concatenate([jnp.where(r8 == 0, 0.0, body[0:SUBLANES]), body[SUBLANES:]], axis=0)
        return pltpu.roll(load(r0 - halo, r0 + n), 1, 0)[halo:]
    if (r0 + n) % L == 0:
        body = pltpu.roll(load(r0, r0 + n), n - 1, 0)
        return jnp.concatenate([body[:n - SUBLANES], jnp.where(r8 == SUBLANES - 1, 0.0, body[n - SUBLANES:])], axis=0)
    return pltpu.roll(load(r0, r0 + n + halo), n + halo - 1, 0)[:n]


def _ffn_out_kernel(gc_ref, val_ref, cw_ref, cb_ref, w_ref, x_ref, g2_ref, gf_ref, o_ref, *, L, final):
    k = pl.program_id(1)
    last = pl.num_programs(1) - 1

    @pl.when(k == 0)
    def _():
        o_ref[...] = jnp.zeros_like(o_ref)

    n = min(FFN_ROW_CHUNK, L)
    gate_ref = gc_ref.at[0]
    for r0 in range(0, gate_ref.shape[0], n):
        rows = slice(r0, r0 + n)
        gate = (_shifted_chunk(gate_ref, r0, n, L, True) * cw_ref[0:1, :]
                + gate_ref[rows, :].astype(F32) * cw_ref[1:2, :]
                + _shifted_chunk(gate_ref, r0, n, L, False) * cw_ref[2:3, :] + cb_ref[...])
        half = 0.5 * gate
        act = (half + half * jnp.tanh(half)) * val_ref[0, rows, :].astype(F32)
        o_ref[rows, :] += jnp.dot(act.astype(BF16), w_ref[0], preferred_element_type=F32)

    @pl.when(k == last)
    def _():
        def rows_epilogue(r, carry):
            rows = pl.ds(pl.multiple_of(r * LANES, LANES), LANES)
            y = x_ref[rows, :] + g2_ref[0] * o_ref[rows, :]
            if final:
                y = y * lax.rsqrt(jnp.mean(y * y, axis=-1, keepdims=True) + EPS) * gf_ref[...]
            o_ref[rows, :] = y
            return carry

        lax.fori_loop(0, o_ref.shape[0] // LANES, rows_epilogue, 0)


def ffn_out(U, x, g2, conv_w, conv_b, w, l, gf, final, L, bm):
    M, D = x.shape
    bk = FFN_BLOCK
    nk = D_FF_PAD // bk
    nb = g2.shape[0]
    assert bm % L == 0 and (nb == 1 or bm == L)
    bidx = (lambda i, k: (i, 0, 0)) if nb > 1 else (lambda i, k: (0, 0, 0))
    return pl.pallas_call(
        functools.partial(_ffn_out_kernel, L=L, final=final),
        grid=(M // bm, nk),
        in_specs=[pl.BlockSpec((1, bm, bk), lambda i, k: (k, i, 0)),
                  pl.BlockSpec((1, bm, bk), lambda i, k: (nk + k, i, 0)),
                  pl.BlockSpec((3, bk), lambda i, k: (0, k)),
                  pl.BlockSpec((1, bk), lambda i, k: (0, k)),
                  pl.BlockSpec((1, bk, D), lambda i, k: (l, k, 0)),
                  pl.BlockSpec((bm, D), lambda i, k: (i, 0)),
                  pl.BlockSpec((1, 1, D), bidx),
                  pl.BlockSpec((1, D), lambda i, k: (0, 0))],
        out_specs=pl.BlockSpec((bm, D), lambda i, k: (i, 0)),
        out_shape=jax.ShapeDtypeStruct((M, D), F32),
        compiler_params=_cparams(2),
        name="ffn_out",
    )(U, U, conv_w, conv_b.reshape(1, D_FF_PAD), w, x, g2, gf.reshape(1, D))


def _block_diag(w):
    n, bs, _ = w.shape
    return jnp.einsum('ncd,nm->ncmd', w, jnp.eye(n, dtype=w.dtype)).reshape(n * bs, n * bs)


def _pad_rows(w, start, total):
    k = w.shape[-2]
    pad = [(0, 0)] * (w.ndim - 2) + [(start, total - start - k), (0, 0)]
    return jnp.pad(w, pad)


def prepare_layer(l, w):
    W = BRANCH_W
    p = {}
    p['lru_wa'] = jnp.stack([_block_diag(w['lru_wa'][l, d]) for d in range(2)]).astype(BF16)
    p['lru_wx'] = jnp.stack([_block_diag(w['lru_wx'][l, d]) for d in range(2)]).astype(BF16)
    p['lru_sp'] = jax.nn.softplus(-w['lru_lam'][l])
    p['s5'] = s5_params(w['s5_a_re'][l], w['s5_a_im'][l], w['s5_log_dt'][l], w['s5_b_re'][l], w['s5_b_im'][l],
                        w['s5_c_re'][l], w['s5_c_im'][l])
    p['rwkv'] = dict(
        mu=w['rwkv_mu'][l], w0=w['rwkv_w0'][l].reshape(2, 1, W), a0=w['rwkv_a0'][l].reshape(2, 1, W),
        w2=_pad_rows(w['rwkv_w2'][l], 0, LORA_W).astype(BF16),
        a2=_pad_rows(w['rwkv_a2'][l], W_LORA, LORA_W).astype(BF16),
        g2=_pad_rows(w['rwkv_g2'][l], W_LORA + A_LORA, LORA_W).astype(BF16),
        k_k=w['rwkv_k_k'][l].reshape(1, W), k_a=w['rwkv_k_a'][l].reshape(1, W),
        r_k=w['rwkv_r_k'][l].reshape(1, W), ln_w=w['rwkv_ln_w'][l].reshape(1, W), ln_b=w['rwkv_ln_b'][l].reshape(1, W))
    p['glu_w'] = w['s5_glu_w'][l].astype(BF16)
    p['branch_w'] = w['branch_w'][l].astype(BF16)
    p['w_out'] = w['w_out'][l].astype(BF16)
    fpad = D_FF_PAD - D_FF
    p['ffn_conv_w'] = jnp.pad(w['ffn_conv_w'][l], ((0, 0), (0, fpad)))
    p['ffn_conv_b'] = jnp.pad(w['ffn_conv_b'][l], (0, fpad))
    return p


def layer_step(x, mod, p, w, l, B, L, ctx, rope_tabs, bm):
    W = BRANCH_W
    D = D_MODEL
    lat = ctx is not None
    nb = mod.shape[0]
    sh1, sc1, g1, sh2, sc2, g2 = (mod[:, i * D:(i + 1) * D].reshape(nb, 1, D) for i in range(6))
    P = norm_mod_matmul(x, w['norm1'][l], sc1, sh1, w['w_in'], l, L, bm, 1024)

    lru0 = ctx['lru'].transpose(1, 0, 2) if lat else jnp.zeros((2, B, W), F32)
    hs, lru_fin = lru_mixer(P, w['lru_conv_w'][l], w['lru_conv_b'][l], p['lru_wa'], w['lru_ba'][l],
                            p['lru_wx'], w['lru_bx'][l], p['lru_sp'], lru0, B, L)

    gs = S5_NGROUP // S5_SLICES
    if lat:
        s50 = ctx['s5'].reshape(B, 2, 2, S5_SLICES, gs * S5_STATE).transpose(3, 1, 0, 2, 4).reshape(
            S5_SLICES, 2, B, 2 * S5_HALF)
    else:
        s50 = jnp.zeros((S5_SLICES, 2, B, 2 * S5_HALF), F32)
    wb, ar, ai, wc = p['s5']
    ysum, s5_fin = s5_mixer(P, wb, ar, ai, wc, w['s5_d'][l], s50, B, L)

    sink = w['attn_sink'][l]
    if lat:
        yc = lat_attention(P, ctx['k_all'], ctx['v_all'], l, sink, rope_tabs, B, L)
    else:
        yc = ctx_attention(P, sink, B, L)

    HP = W // LANES
    if lat:
        st = jnp.swapaxes(ctx['wkv'].reshape(B, 2, HP, 2, RWKV_HEAD, RWKV_HEAD), -1, -2)
        z = jnp.zeros_like(st[:, :, :, 0])
        t0 = jnp.concatenate([jnp.concatenate([st[:, :, :, 0], z], axis=-1),
                              jnp.concatenate([z, st[:, :, :, 1]], axis=-1)], axis=-2)
        t0 = t0.transpose(0, 2, 1, 3, 4).reshape(B * HP, 2, LANES, LANES)
    else:
        t0 = jnp.zeros((B * HP, 2, LANES, LANES), F32)
    yd, tfin = rwkv_mixer(P, p['rwkv'], t0, B, L)

    x = merge_out(P, hs, ysum, yc, yd, x, g1, p['glu_w'], w['s5_glu_b'][l], p['branch_w'], p['w_out'], L, 256)
    U = norm_mod_matmul(x, w['norm2'][l], sc2, sh2, w['ffn_w_in_b'], l, L, bm, FFN_BLOCK, tiled_out=True,
                        out_dtype=BF16)
    x = ffn_out(U, x, g2, p['ffn_conv_w'], p['ffn_conv_b'], w['ffn_w_out_b'], l, w['norm_final'], l == DEPTH - 1,
                L, bm)

    new = None
    if not lat:
        k_new = P[:, COL_CK:COL_CK + KV_W].reshape(B, L, N_KV_HEADS, HEAD_DIM)
        v_new = P[:, COL_CV:COL_CV + KV_W].reshape(B, L, N_KV_HEADS, HEAD_DIM)
        lru_new = lru_fin.transpose(1, 0, 2)
        s5_new = s5_fin.reshape(S5_SLICES, 2, B, 2, gs, S5_STATE).transpose(2, 1, 3, 0, 4, 5).reshape(
            B, 2, 2, S5_NGROUP, S5_STATE)
        tf = tfin.reshape(B, HP, 2, LANES, LANES)
        heads = jnp.stack([tf[..., :RWKV_HEAD, :RWKV_HEAD], tf[..., RWKV_HEAD:, RWKV_HEAD:]], axis=3)
        wkv_new = jnp.swapaxes(heads, -1, -2).transpose(0, 2, 1, 3, 4, 5).reshape(
            B, 2, RWKV_NH, RWKV_HEAD, RWKV_HEAD)
        new = (k_new, v_new, lru_new, s5_new, wkv_new)
    return x, new


def kernel(x_prompt, x_sample, cache_k, cache_v, state_lru, state_s5, state_wkv, c, c_ctx, mod_w, mod_b, norm1, norm2, norm_final, w_in, lru_conv_w, lru_conv_b, lru_wa, lru_ba, lru_wx, lru_bx, lru_lam, s5_a_re, s5_a_im, s5_log_dt, s5_b_re, s5_b_im, s5_c_re, s5_c_im, s5_d, s5_glu_w, s5_glu_b, attn_sink, rwkv_mu, rwkv_w0, rwkv_w2, rwkv_a0, rwkv_a2, rwkv_g2, rwkv_k_k, rwkv_k_a, rwkv_r_k, rwkv_ln_w, rwkv_ln_b, branch_w, w_out, ffn_w_in, ffn_conv_w, ffn_conv_b, ffn_w_out):
    w = dict(norm1=norm1, norm2=norm2, norm_final=norm_final, w_in=w_in,
             lru_conv_w=lru_conv_w, lru_conv_b=lru_conv_b, lru_wa=lru_wa, lru_ba=lru_ba,
             lru_wx=lru_wx, lru_bx=lru_bx, lru_lam=lru_lam,
             s5_a_re=s5_a_re, s5_a_im=s5_a_im, s5_log_dt=s5_log_dt, s5_b_re=s5_b_re, s5_b_im=s5_b_im,
             s5_c_re=s5_c_re, s5_c_im=s5_c_im, s5_d=s5_d, s5_glu_w=s5_glu_w, s5_glu_b=s5_glu_b,
             attn_sink=attn_sink,
             rwkv_mu=rwkv_mu, rwkv_w0=rwkv_w0, rwkv_w2=rwkv_w2, rwkv_a0=rwkv_a0, rwkv_a2=rwkv_a2,
             rwkv_g2=rwkv_g2, rwkv_k_k=rwkv_k_k, rwkv_k_a=rwkv_k_a, rwkv_r_k=rwkv_r_k,
             rwkv_ln_w=rwkv_ln_w, rwkv_ln_b=rwkv_ln_b,
             branch_w=branch_w, w_out=w_out, ffn_w_in=ffn_w_in, ffn_conv_w=ffn_conv_w,
             ffn_conv_b=ffn_conv_b, ffn_w_out=ffn_w_out)
    Bc, Lc, D = x_prompt.shape
    Bl, Ll, _ = x_sample.shape
    nrow = 16
    c_all = jnp.concatenate([c_ctx[None, :], c, jnp.zeros((nrow - 1 - Bl, D), F32)], axis=0)
    mods = modulation(c_all, mod_w, mod_b)
    layers = [prepare_layer(l, w) for l in range(DEPTH)]
    w['ffn_w_in_b'], w['ffn_w_out_b'] = ffn_weights_bf16(ffn_w_in, ffn_w_out)
    rope_tabs = rope_tables(Ll)

    xp = x_prompt.reshape(Bc * Lc, D)
    news = []
    for l in range(DEPTH):
        xp, new = layer_step(xp, mods[l, 0:1], layers[l], w, l, Bc, Lc, None, None, 1024)
        news.append(new)
    y_prompt = xp.reshape(Bc, Lc, D)

    xs = x_sample.reshape(Bl * Ll, D)
    for l in range(DEPTH):
        ctx = dict(k_all=cache_k, v_all=cache_v, lru=state_lru[:, l], s5=state_s5[:, l], wkv=state_wkv[:, l])
        xs, _ = layer_step(xs, mods[l, 1:1 + Bl], layers[l], w, l, Bl, Ll, ctx, rope_tabs, min(1024, Ll))
    y_sample = xs.reshape(Bl, Ll, D)

    stack = lambda i: jnp.stack([n[i] for n in news], axis=1)
    return (y_prompt, y_sample, stack(0), stack(1), stack(2), stack(3), stack(4))
```

```python
import functools
import math

import jax
import jax.numpy as jnp
from jax import lax
from jax.experimental import pallas as pl
from jax.experimental.pallas import tpu as pltpu

F32 = jnp.float32
BF16 = jnp.bfloat16
HIGHEST = lax.Precision.HIGHEST

D_MODEL = 2048
DEPTH = 2
PAST_LEN = 256
GRID_W = 64
N_BRANCH = 4
BRANCH_W = 512
EPS = 1e-6
LRU_BLOCKS = 8
LRU_BS = 64
LRU_CONV_W = 4
LRU_PAD_L = 2
LRU_C = 8.0
S5_GROUP = 16
S5_NGROUP = 32
S5_STATE = 64
HEAD_DIM = 64
N_Q_HEADS = 8
N_KV_HEADS = 2
Q_PER_KV = 4
KV_W = 128
WINDOW = 128
BLOCK = 128
ROPE_BASE = 10000.0
ROPE_NF = 16
ATTN_SCALE = 1.0 / math.sqrt(HEAD_DIM)
NEG_INF = -1e30
RWKV_HEAD = 64
RWKV_NH = 8
W_LORA = 64
A_LORA = 64
G_LORA = 128
RWKV_GN_EPS = 64e-5
D_FF = 5504
FFN_BLOCK = 512
D_FF_PAD = 5632
FFN_PIECES = 4
FFN_TAIL = 3
FFN_ROW_CHUNK = 256
NMM_ROW_CHUNK = 256
LANES = 128
SUBLANES = 8
VMEM_LIMIT = 56 * 2 ** 20

COL_AX, COL_AGATE, COL_BU, COL_CQ = 0, 512, 1024, 1536
COL_CK, COL_CV = 2048, 2176
COL_DR, COL_DK, COL_DV, COL_LORA = 2304, 2816, 3328, 3840
COL_MERGE = 4096
D_IN = COL_MERGE + N_BRANCH * D_MODEL
LORA_W = W_LORA + A_LORA + G_LORA


def _cparams(n_grid):
    return pltpu.CompilerParams(dimension_semantics=("arbitrary",) * n_grid, vmem_limit_bytes=VMEM_LIMIT)


def _bdot(a, b):
    return jnp.dot(a.astype(BF16), b.astype(BF16), preferred_element_type=F32)


def _bdot_nt(a, b):
    return lax.dot_general(a.astype(BF16), b.astype(BF16), (((1,), (1,)), ((), ())),
                           preferred_element_type=F32)


def _gelu(x):
    return jax.nn.gelu(x)


def _sigmoid(x):
    return 0.5 * jnp.tanh(0.5 * x) + 0.5


def _mod_kernel(c_ref, w_ref, b_ref, o_ref):
    c = c_ref[...]
    sh, sl = _split(c * _sigmoid(c))
    wh, wl = _split(w_ref[0])
    dot = functools.partial(jnp.dot, preferred_element_type=F32)
    o_ref[0] = dot(sh, wh) + dot(sl, wh) + dot(sh, wl) + b_ref[0]


def modulation(c_all, mod_w, mod_b):
    R, D = c_all.shape
    N = mod_w.shape[2]
    bn = 1024
    return pl.pallas_call(
        _mod_kernel,
        grid=(DEPTH, N // bn),
        in_specs=[pl.BlockSpec((R, D), lambda l, j: (0, 0)),
                  pl.BlockSpec((1, D, bn), lambda l, j: (l, 0, j)),
                  pl.BlockSpec((1, 1, bn), lambda l, j: (l, 0, j))],
        out_specs=pl.BlockSpec((1, R, bn), lambda l, j: (l, 0, j)),
        out_shape=jax.ShapeDtypeStruct((DEPTH, R, N), F32),
        compiler_params=_cparams(2),
        name="modulation",
    )(c_all, mod_w, mod_b.reshape(DEPTH, 1, N))


def _nmm_kernel(x_ref, g_ref, sc_ref, sh_ref, w_ref, o_ref, h_scr):
    j = pl.program_id(1)
    out = o_ref.at[0] if len(o_ref.shape) == 3 else o_ref
    bm = h_scr.shape[0]

    @pl.when(j == 0)
    def _():
        w = w_ref[0].astype(BF16)
        for r0 in range(0, bm, NMM_ROW_CHUNK):
            rows = slice(r0, r0 + NMM_ROW_CHUNK)
            x = x_ref[rows, :]
            y = x * lax.rsqrt(jnp.mean(x * x, axis=-1, keepdims=True) + EPS) * g_ref[...]
            h = (y * (1.0 + sc_ref[0]) + sh_ref[0]).astype(BF16)
            h_scr[rows, :] = h
            out[rows, :] = jnp.dot(h, w, preferred_element_type=F32).astype(out.dtype)

    @pl.when(j > 0)
    def _():
        out[...] = jnp.dot(h_scr[...], w_ref[0].astype(BF16), preferred_element_type=F32).astype(out.dtype)


def norm_mod_matmul(x, g, sc, sh, w, l, L, bm, bn, tiled_out=False, out_dtype=F32):
    M, D = x.shape
    N = w.shape[2]
    nb = sc.shape[0]
    if nb > 1:
        assert L % bm == 0
        per = L // bm
        bidx = lambda i, j: (i // per, 0, 0)
    else:
        bidx = lambda i, j: (0, 0, 0)
    if tiled_out:
        out_spec = pl.BlockSpec((1, bm, bn), lambda i, j: (j, i, 0))
        out_shape = jax.ShapeDtypeStruct((N // bn, M, bn), out_dtype)
    else:
        out_spec = pl.BlockSpec((bm, bn), lambda i, j: (i, j))
        out_shape = jax.ShapeDtypeStruct((M, N), out_dtype)
    return pl.pallas_call(
        _nmm_kernel,
        grid=(M // bm, N // bn),
        in_specs=[pl.BlockSpec((bm, D), lambda i, j: (i, 0)),
                  pl.BlockSpec((1, D), lambda i, j: (0, 0)),
                  pl.BlockSpec((1, 1, D), bidx),
                  pl.BlockSpec((1, 1, D), bidx),
                  pl.BlockSpec((1, D, bn), lambda i, j: (l, 0, j))],
        out_specs=out_spec,
        out_shape=out_shape,
        scratch_shapes=[pltpu.VMEM((bm, D), BF16)],
        compiler_params=_cparams(2),
        name="norm_mod_matmul",
    )(x, g.reshape(1, D), sc, sh, w)


def _ffn_cast_kernel(w0_ref, w1_ref, w2_ref, w3_ref, o_ref, *, axis, ntile):
    t = pl.program_id(1)
    for q, w_ref in enumerate((w0_ref, w1_ref, w2_ref, w3_ref)):
        piece = w_ref[0].astype(BF16)
        if axis == 1:
            o_ref[0, :, q * LANES:(q + 1) * LANES] = piece
        else:
            o_ref[0, q * LANES:(q + 1) * LANES, :] = piece

    @pl.when(t % ntile == ntile - 1)
    def _():
        pad = FFN_BLOCK - FFN_TAIL * LANES
        if axis == 1:
            o_ref[0, :, FFN_TAIL * LANES:] = jnp.zeros((o_ref.shape[1], pad), BF16)
        else:
            o_ref[0, FFN_TAIL * LANES:, :] = jnp.zeros((pad, o_ref.shape[2]), BF16)


def ffn_weights_bf16(w_in, w_out):
    nl, D, _ = w_in.shape
    ntile = D_FF_PAD // FFN_BLOCK
    half_pieces = D_FF // LANES

    def col_piece(q):
        last = 2 * half_pieces - 1
        return pl.BlockSpec((1, D, LANES), lambda l, j: (
            l, 0, jnp.minimum((j // ntile) * half_pieces + (j % ntile) * FFN_PIECES + q, last)))

    def row_piece(q):
        return pl.BlockSpec((1, LANES, D), lambda l, k: (l, jnp.minimum(k * FFN_PIECES + q, half_pieces - 1), 0))

    w_in_b = pl.pallas_call(
        functools.partial(_ffn_cast_kernel, axis=1, ntile=ntile),
        grid=(nl, 2 * ntile),
        in_specs=[col_piece(q) for q in range(FFN_PIECES)],
        out_specs=pl.BlockSpec((1, D, FFN_BLOCK), lambda l, j: (l, 0, j)),
        out_shape=jax.ShapeDtypeStruct((nl, D, 2 * D_FF_PAD), BF16),
        compiler_params=_cparams(2),
        name="ffn_w_in_cast",
    )(w_in, w_in, w_in, w_in)
    w_out_b = pl.pallas_call(
        functools.partial(_ffn_cast_kernel, axis=0, ntile=ntile),
        grid=(nl, ntile),
        in_specs=[row_piece(q) for q in range(FFN_PIECES)],
        out_specs=pl.BlockSpec((1, FFN_BLOCK, D), lambda l, k: (l, k, 0)),
        out_shape=jax.ShapeDtypeStruct((nl, D_FF_PAD, D), BF16),
        compiler_params=_cparams(2),
        name="ffn_w_out_cast",
    )(w_out, w_out, w_out, w_out)
    return w_in_b, w_out_b


def _to_time_major(x, B, L):
    return x.reshape(B, L, x.shape[-1]).transpose(1, 0, 2).reshape(L * B, x.shape[-1])


def _to_batch_major(x, B, L):
    return x.reshape(L, B, x.shape[-1]).transpose(1, 0, 2).reshape(B * L, x.shape[-1])


def _lru_kernel(x_ref, cw_ref, cb_ref, wa_ref, ba_ref, wx_ref, bx_ref, sp_ref, h0_ref,
                o_ref, fin_ref, xp_ref, out_ref, a_scr, b_scr, *, B, L, Tc):
    nch = L // Tc
    R = Tc * B
    xp_ref[0:LRU_PAD_L * B, :] = jnp.zeros((LRU_PAD_L * B, LANES), F32)
    xp_ref[LRU_PAD_L * B:(LRU_PAD_L + L) * B, :] = _to_time_major(x_ref[...], B, L)
    tail = LRU_CONV_W - 1 - LRU_PAD_L
    xp_ref[(LRU_PAD_L + L) * B:, :] = jnp.zeros((tail * B, LANES), F32)
    out_ref[...] = jnp.zeros_like(out_ref)

    def chunk(ci, hs):
        bases = (pl.multiple_of(ci * R, R), pl.multiple_of((nch - 1 - ci) * R, R))
        for d in range(2):
            base = bases[d]
            xa = cb_ref[...] + xp_ref[pl.ds(base, R), :] * cw_ref[0:1, :]
            for j in range(1, LRU_CONV_W):
                xa = xa + xp_ref[pl.ds(base + j * B, R), :] * cw_ref[j:j + 1, :]
            r = _sigmoid(_bdot(xa, wa_ref[d]) + ba_ref[d])
            i = _sigmoid(_bdot(xa, wx_ref[d]) + bx_ref[d])
            a = jnp.exp(-LRU_C * r * sp_ref[d])
            a_scr[d] = a
            b_scr[d] = jnp.sqrt(1.0 - a * a) * (i * xa)

        def step(ti, hs):
            new = []
            for d in range(2):
                off = pl.multiple_of((ti if d == 0 else Tc - 1 - ti) * B, B)
                h = a_scr[d, pl.ds(off, B), :] * hs[d] + b_scr[d, pl.ds(off, B), :]
                row = pl.ds(bases[d] + off, B)
                out_ref[row, :] = out_ref[row, :] + h
                new.append(h)
            return tuple(new)

        return lax.fori_loop(0, Tc, step, hs, unroll=8)

    hf, hb = lax.fori_loop(0, nch, chunk, (h0_ref[0], h0_ref[1]))
    fin_ref[0] = hf
    fin_ref[1] = hb
    o_ref[...] = _to_batch_major(out_ref[...], B, L)


def lru_mixer(P, cw, cb, wa, ba, wx, bx, sp, h0, B, L):
    W = BRANCH_W
    Tc = min(128, L // 2)
    kern = functools.partial(_lru_kernel, B=B, L=L, Tc=Tc)
    vec = lambda: pl.BlockSpec((2, 1, LANES), lambda s: (0, 0, s))
    return pl.pallas_call(
        kern,
        grid=(W // LANES,),
        in_specs=[pl.BlockSpec((B * L, LANES), lambda s: (0, COL_AX // LANES + s)),
                  pl.BlockSpec((LRU_CONV_W, LANES), lambda s: (0, s)),
                  pl.BlockSpec((1, LANES), lambda s: (0, s)),
                  pl.BlockSpec((2, LANES, LANES), lambda s: (0, s, s)), vec(),
                  pl.BlockSpec((2, LANES, LANES), lambda s: (0, s, s)), vec(),
                  vec(),
                  pl.BlockSpec((2, B, LANES), lambda s: (0, 0, s))],
        out_specs=[pl.BlockSpec((L * B, LANES), lambda s: (0, s)),
                   pl.BlockSpec((2, B, LANES), lambda s: (0, 0, s))],
        out_shape=[jax.ShapeDtypeStruct((L * B, W), F32), jax.ShapeDtypeStruct((2, B, W), F32)],
        scratch_shapes=[pltpu.VMEM(((L + LRU_CONV_W - 1) * B, LANES), F32), pltpu.VMEM((L * B, LANES), F32),
                        pltpu.VMEM((2, Tc * B, LANES), F32), pltpu.VMEM((2, Tc * B, LANES), F32)],
        compiler_params=_cparams(1),
        name="lru_mixer",
    )(P, cw, cb.reshape(1, W), wa, ba.reshape(2, 1, W), wx, bx.reshape(2, 1, W), sp.reshape(2, 1, W), h0)


S5_SLICES = BRANCH_W // LANES
S5_HALF = (S5_NGROUP // S5_SLICES) * S5_STATE


def _s5_kernel(x_ref, wb_ref, ar_ref, ai_ref, wc_ref, dv_ref, h0_ref, o_ref, fin_ref, u_ref, out_ref, hs_scr,
               *, B, L, Tc):
    nch = L // Tc
    R = Tc * B
    H = S5_HALF
    u_ref[...] = _to_time_major(x_ref[...], B, L)
    out_ref[...] = u_ref[...] * dv_ref[...]
    ab = [(ar_ref[0, d], ai_ref[0, d]) for d in range(2)]

    def chunk(ci, carry):
        bases = (pl.multiple_of(ci * R, R), pl.multiple_of((nch - 1 - ci) * R, R))
        for d in range(2):
            hs_scr[d] = _bdot(u_ref[pl.ds(bases[d], R), :], wb_ref[0, d])

        def step(ti, carry):
            new = []
            for d in range(2):
                hr, hi = carry[d]
                ar, ai = ab[d]
                row = pl.ds(pl.multiple_of((ti if d == 0 else Tc - 1 - ti) * B, B), B)
                nr = ar * hr - ai * hi + hs_scr[d, row, 0:H]
                ni = ar * hi + ai * hr + hs_scr[d, row, H:2 * H]
                hs_scr[d, row, 0:H] = nr
                hs_scr[d, row, H:2 * H] = ni
                new.append((nr, ni))
            return tuple(new)

        carry = lax.fori_loop(0, Tc, step, carry, unroll=2)
        for d in range(2):
            rows = pl.ds(bases[d], R)
            out_ref[rows, :] = out_ref[rows, :] + _bdot(hs_scr[d], wc_ref[0, d])
        return carry

    init = tuple((h0_ref[0, d, :, 0:H], h0_ref[0, d, :, H:2 * H]) for d in range(2))
    fin = lax.fori_loop(0, nch, chunk, init)
    for d in range(2):
        fin_ref[0, d, :, 0:H] = fin[d][0]
        fin_ref[0, d, :, H:2 * H] = fin[d][1]
    o_ref[...] = _to_batch_major(out_ref[...], B, L)


def s5_mixer(P, wb, ar, ai, wc, dvec, h0, B, L):
    W = BRANCH_W
    Tc = min(128, L // 2, 1024 // B)
    H2 = 2 * S5_HALF
    kern = functools.partial(_s5_kernel, B=B, L=L, Tc=Tc)
    return pl.pallas_call(
        kern,
        grid=(S5_SLICES,),
        in_specs=[pl.BlockSpec((B * L, LANES), lambda s: (0, COL_BU // LANES + s)),
                  pl.BlockSpec((1, 2, LANES, H2), lambda s: (s, 0, 0, 0)),
                  pl.BlockSpec((1, 2, 1, S5_HALF), lambda s: (s, 0, 0, 0)),
                  pl.BlockSpec((1, 2, 1, S5_HALF), lambda s: (s, 0, 0, 0)),
                  pl.BlockSpec((1, 2, H2, LANES), lambda s: (s, 0, 0, 0)),
                  pl.BlockSpec((1, LANES), lambda s: (0, s)),
                  pl.BlockSpec((1, 2, B, H2), lambda s: (s, 0, 0, 0))],
        out_specs=[pl.BlockSpec((L * B, LANES), lambda s: (0, s)),
                   pl.BlockSpec((1, 2, B, H2), lambda s: (s, 0, 0, 0))],
        out_shape=[jax.ShapeDtypeStruct((L * B, W), F32), jax.ShapeDtypeStruct((S5_SLICES, 2, B, H2), F32)],
        scratch_shapes=[pltpu.VMEM((L * B, LANES), F32), pltpu.VMEM((L * B, LANES), F32),
                        pltpu.VMEM((2, Tc * B, H2), F32)],
        compiler_params=_cparams(1),
        name="s5_mixer",
    )(P, wb, ar, ai, wc, dvec.reshape(1, W), h0)


def s5_params(a_re, a_im, log_dt, b_re, b_im, c_re, c_im):
    dt = jnp.exp(log_dt)[..., None]
    mag = jnp.exp(a_re * dt)
    abar_r = mag * jnp.cos(a_im * dt)
    abar_i = mag * jnp.sin(a_im * dt)
    den = a_re * a_re + a_im * a_im
    nr = abar_r - 1.0
    fr = (nr * a_re + abar_i * a_im) / den
    fi = (abar_i * a_re - nr * a_im) / den
    bbar_r = fr[..., None] * b_re - fi[..., None] * b_im
    bbar_i = fr[..., None] * b_im + fi[..., None] * b_re
    gs = S5_NGROUP // S5_SLICES
    eye = jnp.eye(gs, dtype=F32)
    bb = jnp.stack([bbar_r, bbar_i], axis=0).reshape(2, 2, S5_SLICES, gs, S5_STATE, S5_GROUP)
    wb = jnp.einsum('rdsgnp,gh->sdgprhn', bb, eye).reshape(S5_SLICES, 2, LANES, 2 * S5_HALF)
    cc = jnp.stack([c_re, -c_im], axis=0).reshape(2, 2, S5_SLICES, gs, S5_GROUP, S5_STATE)
    wc = jnp.einsum('rdsgpn,gh->sdrhngp', cc, eye).reshape(S5_SLICES, 2, 2 * S5_HALF, LANES)
    ar = abar_r.reshape(2, S5_SLICES, 1, S5_HALF).transpose(1, 0, 2, 3)
    ai = abar_i.reshape(2, S5_SLICES, 1, S5_HALF).transpose(1, 0, 2, 3)
    return wb.astype(BF16), ar, ai, wc.astype(BF16)


def _rope(x, cos, sina, sinb):
    return x * cos + pltpu.roll(x, ROPE_NF, 1) * sina + pltpu.roll(x, LANES - ROPE_NF, 1) * sinb


def _attend(qg, segs, sink_col):
    def lane_tiles(a):
        return [a[:, c * LANES:(c + 1) * LANES] for c in range(a.shape[1] // LANES)]

    def tree(op, xs):
        while len(xs) > 1:
            xs = [op(xs[i], xs[i + 1]) if i + 1 < len(xs) else xs[i] for i in range(0, len(xs), 2)]
        return xs[0]

    scores = []
    for k, _, mask in segs:
        s = _bdot_nt(qg, k) * ATTN_SCALE
        if mask is not None:
            s = jnp.where(mask, s, NEG_INF)
        scores.append(s)
    m = jnp.maximum(sink_col, jnp.max(tree(jnp.maximum, [t for s in scores for t in lane_tiles(s)]),
                                      axis=-1, keepdims=True))
    ps = [jnp.exp(s - m) for s in scores]
    den = jnp.exp(sink_col - m) + jnp.sum(tree(jnp.add, [t for p in ps for t in lane_tiles(p)]),
                                          axis=-1, keepdims=True)
    o = tree(jnp.add, [_bdot(p, v) for p, (_, v, _) in zip(ps, segs)])
    return o * (1.0 / den)


def _sink_col(sink_ref, kvh):
    return jnp.concatenate([jnp.full((BLOCK, 1), sink_ref[kvh * Q_PER_KV + g], F32) for g in range(Q_PER_KV)], axis=0)


def _stack_heads(q, kvh):
    return jnp.concatenate([q[:, (kvh * Q_PER_KV + g) * HEAD_DIM:(kvh * Q_PER_KV + g + 1) * HEAD_DIM]
                            for g in range(Q_PER_KV)], axis=0)


def _store_heads(o_ref, og, kvh):
    for g in range(Q_PER_KV):
        h = kvh * Q_PER_KV + g
        o_ref[:, h * HEAD_DIM:(h + 1) * HEAD_DIM] = og[g * BLOCK:(g + 1) * BLOCK]


def _ctx_attn_kernel(sink_ref, q_ref, k_ref, v_ref, o_ref):
    q = q_ref[...]
    k = k_ref[...]
    v = v_ref[...]
    for kvh in range(N_KV_HEADS):
        ks = k[:, kvh * HEAD_DIM:(kvh + 1) * HEAD_DIM]
        vs = v[:, kvh * HEAD_DIM:(kvh + 1) * HEAD_DIM]
        og = _attend(_stack_heads(q, kvh), [(ks, vs, None)], _sink_col(sink_ref, kvh))
        _store_heads(o_ref, og, kvh)


def ctx_attention(P, sink, B, L):
    nb = L // BLOCK
    return pl.pallas_call(
        _ctx_attn_kernel,
        grid=(B, nb),
        in_specs=[pl.BlockSpec(memory_space=pltpu.SMEM),
                  pl.BlockSpec((BLOCK, BRANCH_W), lambda b, j: (b * nb + j, COL_CQ // BRANCH_W)),
                  pl.BlockSpec((L, KV_W), lambda b, j: (b, COL_CK // KV_W)),
                  pl.BlockSpec((L, KV_W), lambda b, j: (b, COL_CV // KV_W))],
        out_specs=pl.BlockSpec((BLOCK, BRANCH_W), lambda b, j: (b * nb + j, 0)),
        out_shape=jax.ShapeDtypeStruct((B * L, BRANCH_W), F32),
        compiler_params=_cparams(2),
        name="ctx_attention",
    )(sink.reshape(N_Q_HEADS), P, P, P)


def _lat_attn_kernel(sink_ref, q_ref, kp_ref, kc_ref, kn_ref, vp_ref, vc_ref, vn_ref, ck_ref, cv_ref,
                     cq_ref, saq_ref, sbq_ref, cp_ref, sap_ref, sbp_ref, cn_ref, san_ref, sbn_ref, o_ref, *, nb):
    j = pl.program_id(1)
    cq, saq, sbq = cq_ref[...], saq_ref[...], sbq_ref[...]
    q = jnp.concatenate([_rope(q_ref[:, c * LANES:(c + 1) * LANES], cq, saq, sbq)
                         for c in range(BRANCH_W // LANES)], axis=1)
    kp = _rope(kp_ref[...], cp_ref[...], sap_ref[...], sbp_ref[...])
    kc = _rope(kc_ref[...], cq, saq, sbq)
    kn = _rope(kn_ref[...], cn_ref[...], san_ref[...], sbn_ref[...])
    R = Q_PER_KV * BLOCK
    r = lax.broadcasted_iota(jnp.int32, (R, BLOCK), 0) % BLOCK
    c = lax.broadcasted_iota(jnp.int32, (R, BLOCK), 1)
    mask_p = (c >= r) & (j > 0)
    mask_n = (c <= r) & (j < nb - 1)
    ck = ck_ref[0, 0]
    cv = cv_ref[0, 0]
    for kvh in range(N_KV_HEADS):
        sl = slice(kvh * HEAD_DIM, (kvh + 1) * HEAD_DIM)
        segs = [(ck[:, sl], cv[:, sl], None),
                (kp[:, sl], vp_ref[:, sl], mask_p),
                (kc[:, sl], vc_ref[:, sl], None),
                (kn[:, sl], vn_ref[:, sl], mask_n)]
        og = _attend(_stack_heads(q, kvh), segs, _sink_col(sink_ref, kvh))
        _store_heads(o_ref, og, kvh)


def lat_attention(P, cache_k, cache_v, layer, sink, rope_tabs, B, L):
    nb = L // BLOCK
    kcol, vcol = COL_CK // KV_W, COL_CV // KV_W
    prev = lambda b, j: b * nb + jnp.maximum(j - 1, 0)
    nxt = lambda b, j: b * nb + jnp.minimum(j + 1, nb - 1)
    kv = lambda rowf, col: pl.BlockSpec((BLOCK, KV_W), lambda b, j: (rowf(b, j), col))
    cur = lambda b, j: b * nb + j
    tab = lambda f: pl.BlockSpec((BLOCK, LANES), f)
    tq = lambda b, j: (j, 0)
    tp = lambda b, j: (jnp.maximum(j - 1, 0), 0)
    tn = lambda b, j: (jnp.minimum(j + 1, nb - 1), 0)
    cos, sina, sinb = rope_tabs
    cache = pl.BlockSpec((1, 1, PAST_LEN, KV_W), lambda b, j: (b, layer, 0, 0))
    return pl.pallas_call(
        functools.partial(_lat_attn_kernel, nb=nb),
        grid=(B, nb),
        in_specs=[pl.BlockSpec(memory_space=pltpu.SMEM),
                  pl.BlockSpec((BLOCK, BRANCH_W), lambda b, j: (b * nb + j, COL_CQ // BRANCH_W)),
                  kv(prev, kcol), kv(cur, kcol), kv(nxt, kcol),
                  kv(prev, vcol), kv(cur, vcol), kv(nxt, vcol),
                  cache, cache,
                  tab(tq), tab(tq), tab(tq), tab(tp), tab(tp), tab(tp), tab(tn), tab(tn), tab(tn)],
        out_specs=pl.BlockSpec((BLOCK, BRANCH_W), lambda b, j: (b * nb + j, 0)),
        out_shape=jax.ShapeDtypeStruct((B * L, BRANCH_W), F32),
        compiler_params=_cparams(2),
        name="lat_attention",
    )(sink.reshape(N_Q_HEADS), P, P, P, P, P, P, P,
      cache_k.reshape(cache_k.shape[0], DEPTH, PAST_LEN, KV_W), cache_v.reshape(cache_v.shape[0], DEPTH, PAST_LEN, KV_W),
      cos, sina, sinb, cos, sina, sinb, cos, sina, sinb)


def rope_tables(L):
    t = jnp.arange(L)
    inv = ROPE_BASE ** (-jnp.arange(ROPE_NF, dtype=F32) / ROPE_NF)
    ang_r = (t // GRID_W).astype(F32)[:, None] * inv
    ang_c = (t % GRID_W).astype(F32)[:, None] * inv
    z = jnp.zeros((L, ROPE_NF), F32)
    cos = jnp.concatenate([jnp.cos(ang_r)] * 2 + [jnp.cos(ang_c)] * 2, axis=1)
    sina = jnp.concatenate([z, jnp.sin(ang_r), z, jnp.sin(ang_c)], axis=1)
    sinb = jnp.concatenate([-jnp.sin(ang_r), z, -jnp.sin(ang_c), z], axis=1)
    return tuple(jnp.tile(a, (1, LANES // HEAD_DIM)) for a in (cos, sina, sinb))


RWKV_CHUNK = 64
RWKV_PREP_CHUNKS = 8


def _split(a):
    hi = a.astype(BF16)
    return hi, (a - hi.astype(F32)).astype(BF16)


def _bdot3(a, b):
    ah, al = _split(a)
    bh, bl = _split(b)
    return jnp.dot(jnp.concatenate([ah, al, ah], axis=1), jnp.concatenate([bh, bh, bl], axis=0),
                   preferred_element_type=F32)


def _head_sum(x, head_ones):
    return _bdot3(x, head_ones)


def _unit_lower_inverse(ns, eye):
    xs = [eye + n for n in ns]
    pws = list(ns)
    for _ in range(5):
        pws = [_bdot(pw, pw) for pw in pws]
        xs = [x + _bdot(x, pw) for x, pw in zip(xs, pws)]
    resids = [eye - (x - _bdot3(n, x)) for n, x in zip(ns, xs)]
    return [x + _bdot(x, resid) for x, resid in zip(xs, resids)]


def _rwkv_kernel(xr_ref, xk_ref, xv_ref, lo_ref, mu_ref, w0_ref, w2_ref, a0_ref, a2_ref, g2_ref,
                 kk_ref, ka_ref, rk_ref, lnw_ref, lnb_ref, t0_ref,
                 y_ref, tfin_ref,
                 r_scr, kh_scr, v_scr, kn_scr, lw_scr, a_scr, kd_scr, yd_scr,
                 phi_scr, psi_scr, pcm_scr, rys_scr, gys_scr, *, L):
    C = RWKV_CHUNK
    nch = L // C
    lane = lax.broadcasted_iota(jnp.int32, (1, LANES), 1)
    m0 = (lane < RWKV_HEAD).astype(F32)
    m1 = 1.0 - m0
    head_ones = (lax.broadcasted_iota(jnp.int32, (LANES, LANES), 0) // RWKV_HEAD
                 == lax.broadcasted_iota(jnp.int32, (LANES, LANES), 1) // RWKV_HEAD).astype(F32)

    def shift(x, mu):
        return x + mu * (0.5 * (_shift_rows(x, L, True) + _shift_rows(x, L, False)) - x)

    r = shift(xr_ref[...], mu_ref[0:1, :])
    kh = shift(xk_ref[...], mu_ref[1:2, :])
    vh = shift(xv_ref[...], mu_ref[2:3, :])
    r_scr[...] = r
    kh_scr[...] = kh
    v_scr[...] = vh
    kk = kh * kk_ref[...]
    kn_scr[...] = kk * lax.rsqrt(_head_sum(kk * kk, head_ones) + 1e-12)
    lo = lo_ref[...]
    tw = jnp.tanh(lo)
    for d in range(2):
        z = w0_ref[d] + _bdot(tw, w2_ref[d])
        lw_scr[d] = -math.exp(-0.5) * _sigmoid(z)
        a = _sigmoid(a0_ref[d] + _bdot(lo, a2_ref[d]))
        a_scr[d] = a
        kd_scr[d] = kh * (1.0 + (a - 1.0) * ka_ref[...])

    ri = lax.broadcasted_iota(jnp.int32, (2 * C, 2 * C), 0)
    ci = lax.broadcasted_iota(jnp.int32, (2 * C, 2 * C), 1)
    same = (ri // C) == (ci // C)
    eye2 = (ri == ci).astype(F32)
    ti = lax.broadcasted_iota(jnp.int32, (C, C), 0)
    si = lax.broadcasted_iota(jnp.int32, (C, C), 1)
    rt, ct = ri % C, ci % C
    strict = (same & (ct < rt), same & (ct > rt))
    incl = (same & (ct <= rt), same & (ct >= rt))
    tri = ((si <= ti).astype(F32), (si >= ti).astype(F32))

    def stack(x):
        return jnp.concatenate([x * m0, x * m1], axis=0)

    def unstack(x):
        return x[0:C] + x[C:2 * C]

    def prepare(jobs):
        J = range(len(jobs))
        ds = [d for _, d in jobs]
        rows = [pl.ds(pl.multiple_of(c * C, C), C) for c, _ in jobs]
        lw = [lw_scr[ds[j], rows[j], :] for j in J]
        cum = [jnp.dot(tri[ds[j]], lw[j], preferred_element_type=F32, precision=HIGHEST) for j in J]
        p_in = [jnp.exp(cum[j]) for j in J]
        p_ex = [jnp.exp(cum[j] - lw[j]) for j in J]
        ip = [jnp.exp(-cum[j]) for j in J]
        pc = [p_in[j][C - 1:C, :] if ds[j] == 0 else p_in[j][0:1, :] for j in J]
        kn = [kn_scr[rows[j], :] for j in J]
        KAP = [stack(kn[j] * p_ex[j]) for j in J]
        RT = [stack(r_scr[rows[j], :] * p_in[j]) for j in J]
        BT = [stack(-(a_scr[ds[j], rows[j], :] * kn[j]) * ip[j]) for j in J]
        KT = [stack(kd_scr[ds[j], rows[j], :] * ip[j]) for j in J]
        V = [stack(v_scr[rows[j], :]) for j in J]
        G = [_bdot_nt(jnp.concatenate([KAP[j], RT[j]], axis=0), jnp.concatenate([BT[j], KT[j]], axis=0))
             for j in J]
        aab = [jnp.where(strict[ds[j]], G[j][0:2 * C, 0:2 * C], 0.0) for j in J]
        aak = [jnp.where(strict[ds[j]], G[j][0:2 * C, 2 * C:4 * C], 0.0) for j in J]
        grb = [jnp.where(incl[ds[j]], G[j][2 * C:4 * C, 0:2 * C], 0.0) for j in J]
        grk = [jnp.where(incl[ds[j]], G[j][2 * C:4 * C, 2 * C:4 * C], 0.0) for j in J]
        AV = [_bdot(aak[j], V[j]) for j in J]
        GV = [_bdot(grk[j], V[j]) for j in J]
        X = _unit_lower_inverse(aab, eye2)
        XK = [_bdot(X[j], KAP[j]) for j in J]
        XAV = [_bdot(X[j], AV[j]) for j in J]
        btp_t = [(BT[j] * pc[j]).T for j in J]
        ktp_t = [(KT[j] * pc[j]).T for j in J]
        phi = [_bdot(btp_t[j], XK[j]) for j in J]
        psi = [_bdot(btp_t[j], XAV[j]) + _bdot(ktp_t[j], V[j]) for j in J]
        rys = [unstack(RT[j] + _bdot(grb[j], XK[j])) for j in J]
        gys = [unstack(_bdot(grb[j], XAV[j]) + GV[j]) for j in J]
        for j, (c, d) in enumerate(jobs):
            phi_scr[d, c] = phi[j].astype(BF16)
            psi_scr[d, c] = psi[j]
            pcm_scr[d, c] = jnp.broadcast_to(pc[j], (LANES, LANES)).T
            rys_scr[d, c] = rys[j].astype(BF16)
            gys_scr[d, c] = gys[j]

    def prep_body(i, carry):
        prepare([(npre * i + u, d) for u in range(npre) for d in range(2)])
        return carry

    npre = min(RWKV_PREP_CHUNKS, nch)
    lax.fori_loop(0, nch // npre, prep_body, 0)

    def seq_body(i, carry):
        out = []
        for d, T in enumerate(carry):
            c = i if d == 0 else nch - 1 - i
            rows = pl.ds(pl.multiple_of(c * C, C), C)
            tb = T.astype(BF16)
            yd_scr[d, rows, :] = jnp.dot(rys_scr[d, c], tb, preferred_element_type=F32) + gys_scr[d, c]
            out.append(pcm_scr[d, c] * T + jnp.dot(phi_scr[d, c], tb, preferred_element_type=F32) + psi_scr[d, c])
        return tuple(out)

    tf, tb_ = lax.fori_loop(0, nch, seq_body, (t0_ref[0, 0], t0_ref[0, 1]))
    tfin_ref[0, 0] = tf
    tfin_ref[0, 1] = tb_

    y = yd_scr[0] + yd_scr[1]
    inv_n = 1.0 / RWKV_HEAD
    mu = _head_sum(y, head_ones) * inv_n
    yc = y - mu
    var = _head_sum(yc * yc, head_ones) * inv_n
    yn = yc * lax.rsqrt(var + RWKV_GN_EPS) * lnw_ref[...] + lnb_ref[...]
    bonus = _head_sum(r * kh * rk_ref[...], head_ones) * vh
    g = _bdot(_sigmoid(lo), g2_ref[...])
    y_ref[...] = (yn + bonus) * g


def rwkv_mixer(P, prm, t0, B, L):
    HP = BRANCH_W // LANES
    nl = L
    col = lambda c0: (lambda b, h: (b, c0 // LANES + h))
    vecs = lambda n: pl.BlockSpec((n, LANES), lambda b, h: (0, h))
    dvec = pl.BlockSpec((2, 1, LANES), lambda b, h: (0, 0, h))
    lora_w = pl.BlockSpec((2, LORA_W, LANES), lambda b, h: (0, 0, h))
    seq = lambda: pltpu.VMEM((L, LANES), F32)
    seq2 = lambda: pltpu.VMEM((2, L, LANES), F32)
    nch = L // RWKV_CHUNK
    return pl.pallas_call(
        functools.partial(_rwkv_kernel, L=L),
        grid=(B, HP),
        in_specs=[pl.BlockSpec((nl, LANES), col(COL_DR)),
                  pl.BlockSpec((nl, LANES), col(COL_DK)),
                  pl.BlockSpec((nl, LANES), col(COL_DV)),
                  pl.BlockSpec((nl, LORA_W), lambda b, h: (b, COL_LORA // LORA_W)),
                  vecs(3), dvec, lora_w, dvec, lora_w,
                  pl.BlockSpec((LORA_W, LANES), lambda b, h: (0, h)),
                  vecs(1), vecs(1), vecs(1), vecs(1), vecs(1),
                  pl.BlockSpec((1, 2, LANES, LANES), lambda b, h: (b * HP + h, 0, 0, 0))],
        out_specs=[pl.BlockSpec((nl, LANES), lambda b, h: (b, h)),
                   pl.BlockSpec((1, 2, LANES, LANES), lambda b, h: (b * HP + h, 0, 0, 0))],
        out_shape=[jax.ShapeDtypeStruct((B * L, BRANCH_W), F32),
                   jax.ShapeDtypeStruct((B * HP, 2, LANES, LANES), F32)],
        scratch_shapes=[seq(), seq(), seq(), seq(), seq2(), seq2(), seq2(), seq2(),
                        pltpu.VMEM((2, nch, LANES, LANES), BF16), pltpu.VMEM((2, nch, LANES, LANES), F32),
                        pltpu.VMEM((2, nch, LANES, LANES), F32), pltpu.VMEM((2, nch, RWKV_CHUNK, LANES), BF16),
                        pltpu.VMEM((2, nch, RWKV_CHUNK, LANES), F32)],
        compiler_params=_cparams(2),
        name="rwkv_mixer",
    )(P, P, P, P, prm['mu'], prm['w0'], prm['w2'], prm['a0'], prm['a2'], prm['g2'],
      prm['k_k'], prm['k_a'], prm['r_k'], prm['ln_w'], prm['ln_b'], t0)


def _merge_kernel(ag_ref, hs_ref, ys_ref, yc_ref, yd_ref, m0_ref, m1_ref, m2_ref, m3_ref, x_ref, g1_ref,
                  gw_ref, gb_ref, bw_ref, wo_ref, o_ref):
    y_a = _gelu(ag_ref[...]) * hs_ref[...]
    y_s = _gelu(ys_ref[...])
    y_b = y_s * _sigmoid(_bdot(y_s, gw_ref[...]) + gb_ref[...])
    acc = None
    for n, (y, m_ref) in enumerate(((y_a, m0_ref), (y_b, m1_ref), (yc_ref[...], m2_ref), (yd_ref[...], m3_ref))):
        t = _sigmoid(m_ref[...]) * _bdot(y, bw_ref[n])
        acc = t if acc is None else acc + t
    o_ref[...] = x_ref[...] + g1_ref[0] * _bdot(acc, wo_ref[...])


def merge_out(P, hs, ysum, yc, yd, x, g1, glu_w, glu_b, branch_w, w_out, L, bm):
    M, D = x.shape
    W = BRANCH_W
    nb = g1.shape[0]
    assert L % bm == 0
    per = L // bm
    bidx = (lambda i: (i // per, 0, 0)) if nb > 1 else (lambda i: (0, 0, 0))
    row = lambda w, c: pl.BlockSpec((bm, w), lambda i: (i, c))
    const = lambda shape: pl.BlockSpec(shape, lambda i: (0,) * len(shape), pipeline_mode=pl.Buffered(1))
    mcol = COL_MERGE // D
    return pl.pallas_call(
        _merge_kernel,
        grid=(M // bm,),
        in_specs=[row(W, COL_AGATE // W), row(W, 0), row(W, 0), row(W, 0), row(W, 0),
                  row(D, mcol), row(D, mcol + 1), row(D, mcol + 2), row(D, mcol + 3),
                  row(D, 0), pl.BlockSpec((1, 1, D), bidx),
                  const((W, W)), const((1, W)), const((N_BRANCH, W, D)), const((D, D))],
        out_specs=row(D, 0),
        out_shape=jax.ShapeDtypeStruct((M, D), F32),
        compiler_params=_cparams(1),
        name="merge_out",
    )(P, hs, ysum, yc, yd, P, P, P, P, x, g1, glu_w, glu_b.reshape(1, W), branch_w, w_out)


def _shift_rows(g, L, back):
    bm = g.shape[0]
    rolled = pltpu.roll(g, 1 if back else bm - 1, 0)
    r8 = lax.broadcasted_iota(jnp.int32, (SUBLANES, 1), 0)
    pieces = []
    for s in range(0, bm, L):
        if back:
            pieces += [jnp.where(r8 == 0, 0.0, rolled[s:s + SUBLANES]), rolled[s + SUBLANES:s + L]]
        else:
            e = s + L - SUBLANES
            pieces += [rolled[s:e], jnp.where(r8 == SUBLANES - 1, 0.0, rolled[e:s + L])]
    return jnp.concatenate(pieces, axis=0)


def _shifted_chunk(ref, r0, n, L, back):
    r8 = lax.broadcasted_iota(jnp.int32, (SUBLANES, 1), 0)
    halo = SUBLANES * (4 // ref.dtype.itemsize)
    load = lambda a, b: ref[a:b, :].astype(F32)
    if back:
        if r0 % L == 0:
            body = pltpu.roll(load(r0, r0 + n), 1, 0)
            return jnp.concatenate([jnp.where(r8 == 0, 0.0, body[0:SUBLANES]), body[SUBLANES:]], axis=0)
        return pltpu.roll(load(r0 - halo, r0 + n), 1, 0)[halo:]
    if (r0 + n) % L == 0:
        body = pltpu.roll(load(r0, r0 + n), n - 1, 0)
        return jnp.concatenate([body[:n - SUBLANES], jnp.where(r8 == SUBLANES - 1, 0.0, body[n - SUBLANES:])], axis=0)
    return pltpu.roll(load(r0, r0 + n + halo), n + halo - 1, 0)[:n]


def _ffn_out_kernel(gc_ref, val_ref, cw_ref, cb_ref, w_ref, x_ref, g2_ref, gf_ref, o_ref, *, L, final):
    k = pl.program_id(1)
    last = pl.num_programs(1) - 1

    @pl.when(k == 0)
    def _():
        o_ref[...] = jnp.zeros_like(o_ref)

    n = min(FFN_ROW_CHUNK, L)
    gate_ref = gc_ref.at[0]
    for r0 in range(0, gate_ref.shape[0], n):
        rows = slice(r0, r0 + n)
        gate = (_shifted_chunk(gate_ref, r0, n, L, True) * cw_ref[0:1, :]
                + gate_ref[rows, :].astype(F32) * cw_ref[1:2, :]
                + _shifted_chunk(gate_ref, r0, n, L, False) * cw_ref[2:3, :] + cb_ref[...])
        half = 0.5 * gate
        act = (half + half * jnp.tanh(half)) * val_ref[0, rows, :].astype(F32)
        o_ref[rows, :] += jnp.dot(act.astype(BF16), w_ref[0], preferred_element_type=F32)

    @pl.when(k == last)
    def _():
        def rows_epilogue(r, carry):
            rows = pl.ds(pl.multiple_of(r * LANES, LANES), LANES)
            y = x_ref[rows, :] + g2_ref[0] * o_ref[rows, :]
            if final:
                y = y * lax.rsqrt(jnp.mean(y * y, axis=-1, keepdims=True) + EPS) * gf_ref[...]
            o_ref[rows, :] = y
            return carry

        lax.fori_loop(0, o_ref.shape[0] // LANES, rows_epilogue, 0)


def ffn_out(U, x, g2, conv_w, conv_b, w, l, gf, final, L, bm):
    M, D = x.shape
    bk = FFN_BLOCK
    nk = D_FF_PAD // bk
    nb = g2.shape[0]
    assert bm % L == 0 and (nb == 1 or bm == L)
    bidx = (lambda i, k: (i, 0, 0)) if nb > 1 else (lambda i, k: (0, 0, 0))
    return pl.pallas_call(
        functools.partial(_ffn_out_kernel, L=L, final=final),
        grid=(M // bm, nk),
        in_specs=[pl.BlockSpec((1, bm, bk), lambda i, k: (k, i, 0)),
                  pl.BlockSpec((1, bm, bk), lambda i, k: (nk + k, i, 0)),
                  pl.BlockSpec((3, bk), lambda i, k: (0, k)),
                  pl.BlockSpec((1, bk), lambda i, k: (0, k)),
                  pl.BlockSpec((1, bk, D), lambda i, k: (l, k, 0)),
                  pl.BlockSpec((bm, D), lambda i, k: (i, 0)),
                  pl.BlockSpec((1, 1, D), bidx),
                  pl.BlockSpec((1, D), lambda i, k: (0, 0))],
        out_specs=pl.BlockSpec((bm, D), lambda i, k: (i, 0)),
        out_shape=jax.ShapeDtypeStruct((M, D), F32),
        compiler_params=_cparams(2),
        name="ffn_out",
    )(U, U, conv_w, conv_b.reshape(1, D_FF_PAD), w, x, g2, gf.reshape(1, D))


def _block_diag(w):
    n, bs, _ = w.shape
    return jnp.einsum('ncd,nm->ncmd', w, jnp.eye(n, dtype=w.dtype)).reshape(n * bs, n * bs)


def _pad_rows(w, start, total):
    k = w.shape[-2]
    pad = [(0, 0)] * (w.ndim - 2) + [(start, total - start - k), (0, 0)]
    return jnp.pad(w, pad)


def prepare_layer(l, w):
    W = BRANCH_W
    p = {}
    p['lru_wa'] = jnp.stack([_block_diag(w['lru_wa'][l, d]) for d in range(2)]).astype(BF16)
    p['lru_wx'] = jnp.stack([_block_diag(w['lru_wx'][l, d]) for d in range(2)]).astype(BF16)
    p['lru_sp'] = jax.nn.softplus(-w['lru_lam'][l])
    p['s5'] = s5_params(w['s5_a_re'][l], w['s5_a_im'][l], w['s5_log_dt'][l], w['s5_b_re'][l], w['s5_b_im'][l],
                        w['s5_c_re'][l], w['s5_c_im'][l])
    p['rwkv'] = dict(
        mu=w['rwkv_mu'][l], w0=w['rwkv_w0'][l].reshape(2, 1, W), a0=w['rwkv_a0'][l].reshape(2, 1, W),
        w2=_pad_rows(w['rwkv_w2'][l], 0, LORA_W).astype(BF16),
        a2=_pad_rows(w['rwkv_a2'][l], W_LORA, LORA_W).astype(BF16),
        g2=_pad_rows(w['rwkv_g2'][l], W_LORA + A_LORA, LORA_W).astype(BF16),
        k_k=w['rwkv_k_k'][l].reshape(1, W), k_a=w['rwkv_k_a'][l].reshape(1, W),
        r_k=w['rwkv_r_k'][l].reshape(1, W), ln_w=w['rwkv_ln_w'][l].reshape(1, W), ln_b=w['rwkv_ln_b'][l].reshape(1, W))
    p['glu_w'] = w['s5_glu_w'][l].astype(BF16)
    p['branch_w'] = w['branch_w'][l].astype(BF16)
    p['w_out'] = w['w_out'][l].astype(BF16)
    fpad = D_FF_PAD - D_FF
    p['ffn_conv_w'] = jnp.pad(w['ffn_conv_w'][l], ((0, 0), (0, fpad)))
    p['ffn_conv_b'] = jnp.pad(w['ffn_conv_b'][l], (0, fpad))
    return p


def layer_step(x, mod, p, w, l, B, L, ctx, rope_tabs, bm):
    W = BRANCH_W
    D = D_MODEL
    lat = ctx is not None
    nb = mod.shape[0]
    sh1, sc1, g1, sh2, sc2, g2 = (mod[:, i * D:(i + 1) * D].reshape(nb, 1, D) for i in range(6))
    P = norm_mod_matmul(x, w['norm1'][l], sc1, sh1, w['w_in'], l, L, bm, 1024)

    lru0 = ctx['lru'].transpose(1, 0, 2) if lat else jnp.zeros((2, B, W), F32)
    hs, lru_fin = lru_mixer(P, w['lru_conv_w'][l], w['lru_conv_b'][l], p['lru_wa'], w['lru_ba'][l],
                            p['lru_wx'], w['lru_bx'][l], p['lru_sp'], lru0, B, L)

    gs = S5_NGROUP // S5_SLICES
    if lat:
        s50 = ctx['s5'].reshape(B, 2, 2, S5_SLICES, gs * S5_STATE).transpose(3, 1, 0, 2, 4).reshape(
            S5_SLICES, 2, B, 2 * S5_HALF)
    else:
        s50 = jnp.zeros((S5_SLICES, 2, B, 2 * S5_HALF), F32)
    wb, ar, ai, wc = p['s5']
    ysum, s5_fin = s5_mixer(P, wb, ar, ai, wc, w['s5_d'][l], s50, B, L)

    sink = w['attn_sink'][l]
    if lat:
        yc = lat_attention(P, ctx['k_all'], ctx['v_all'], l, sink, rope_tabs, B, L)
    else:
        yc = ctx_attention(P, sink, B, L)

    HP = W // LANES
    if lat:
        st = jnp.swapaxes(ctx['wkv'].reshape(B, 2, HP, 2, RWKV_HEAD, RWKV_HEAD), -1, -2)
        z = jnp.zeros_like(st[:, :, :, 0])
        t0 = jnp.concatenate([jnp.concatenate([st[:, :, :, 0], z], axis=-1),
                              jnp.concatenate([z, st[:, :, :, 1]], axis=-1)], axis=-2)
        t0 = t0.transpose(0, 2, 1, 3, 4).reshape(B * HP, 2, LANES, LANES)
    else:
        t0 = jnp.zeros((B * HP, 2, LANES, LANES), F32)
    yd, tfin = rwkv_mixer(P, p['rwkv'], t0, B, L)

    x = merge_out(P, hs, ysum, yc, yd, x, g1, p['glu_w'], w['s5_glu_b'][l], p['branch_w'], p['w_out'], L, 256)
    U = norm_mod_matmul(x, w['norm2'][l], sc2, sh2, w['ffn_w_in_b'], l, L, bm, FFN_BLOCK, tiled_out=True)
    x = ffn_out(U, x, g2, p['ffn_conv_w'], p['ffn_conv_b'], w['ffn_w_out_b'], l, w['norm_final'], l == DEPTH - 1,
                L, bm)

    new = None
    if not lat:
        k_new = P[:, COL_CK:COL_CK + KV_W].reshape(B, L, N_KV_HEADS, HEAD_DIM)
        v_new = P[:, COL_CV:COL_CV + KV_W].reshape(B, L, N_KV_HEADS, HEAD_DIM)
        lru_new = lru_fin.transpose(1, 0, 2)
        s5_new = s5_fin.reshape(S5_SLICES, 2, B, 2, gs, S5_STATE).transpose(2, 1, 3, 0, 4, 5).reshape(
            B, 2, 2, S5_NGROUP, S5_STATE)
        tf = tfin.reshape(B, HP, 2, LANES, LANES)
        heads = jnp.stack([tf[..., :RWKV_HEAD, :RWKV_HEAD], tf[..., RWKV_HEAD:, RWKV_HEAD:]], axis=3)
        wkv_new = jnp.swapaxes(heads, -1, -2).transpose(0, 2, 1, 3, 4, 5).reshape(
            B, 2, RWKV_NH, RWKV_HEAD, RWKV_HEAD)
        new = (k_new, v_new, lru_new, s5_new, wkv_new)
    return x, new


def kernel(x_prompt, x_sample, cache_k, cache_v, state_lru, state_s5, state_wkv, c, c_ctx, mod_w, mod_b, norm1, norm2, norm_final, w_in, lru_conv_w, lru_conv_b, lru_wa, lru_ba, lru_wx, lru_bx, lru_lam, s5_a_re, s5_a_im, s5_log_dt, s5_b_re, s5_b_im, s5_c_re, s5_c_im, s5_d, s5_glu_w, s5_glu_b, attn_sink, rwkv_mu, rwkv_w0, rwkv_w2, rwkv_a0, rwkv_a2, rwkv_g2, rwkv_k_k, rwkv_k_a, rwkv_r_k, rwkv_ln_w, rwkv_ln_b, branch_w, w_out, ffn_w_in, ffn_conv_w, ffn_conv_b, ffn_w_out):
    w = dict(norm1=norm1, norm2=norm2, norm_final=norm_final, w_in=w_in,
             lru_conv_w=lru_conv_w, lru_conv_b=lru_conv_b, lru_wa=lru_wa, lru_ba=lru_ba,
             lru_wx=lru_wx, lru_bx=lru_bx, lru_lam=lru_lam,
             s5_a_re=s5_a_re, s5_a_im=s5_a_im, s5_log_dt=s5_log_dt, s5_b_re=s5_b_re, s5_b_im=s5_b_im,
             s5_c_re=s5_c_re, s5_c_im=s5_c_im, s5_d=s5_d, s5_glu_w=s5_glu_w, s5_glu_b=s5_glu_b,
             attn_sink=attn_sink,
             rwkv_mu=rwkv_mu, rwkv_w0=rwkv_w0, rwkv_w2=rwkv_w2, rwkv_a0=rwkv_a0, rwkv_a2=rwkv_a2,
             rwkv_g2=rwkv_g2, rwkv_k_k=rwkv_k_k, rwkv_k_a=rwkv_k_a, rwkv_r_k=rwkv_r_k,
             rwkv_ln_w=rwkv_ln_w, rwkv_ln_b=rwkv_ln_b,
             branch_w=branch_w, w_out=w_out, ffn_w_in=ffn_w_in, ffn_conv_w=ffn_conv_w,
             ffn_conv_b=ffn_conv_b, ffn_w_out=ffn_w_out)
    Bc, Lc, D = x_prompt.shape
    Bl, Ll, _ = x_sample.shape
    nrow = 16
    c_all = jnp.concatenate([c_ctx[None, :], c, jnp.zeros((nrow - 1 - Bl, D), F32)], axis=0)
    mods = modulation(c_all, mod_w, mod_b)
    layers = [prepare_layer(l, w) for l in range(DEPTH)]
    w['ffn_w_in_b'], w['ffn_w_out_b'] = ffn_weights_bf16(ffn_w_in, ffn_w_out)
    rope_tabs = rope_tables(Ll)

    xp = x_prompt.reshape(Bc * Lc, D)
    news = []
    for l in range(DEPTH):
        xp, new = layer_step(xp, mods[l, 0:1], layers[l], w, l, Bc, Lc, None, None, 1024)
        news.append(new)
    y_prompt = xp.reshape(Bc, Lc, D)

    xs = x_sample.reshape(Bl * Ll, D)
    for l in range(DEPTH):
        ctx = dict(k_all=cache_k, v_all=cache_v, lru=state_lru[:, l], s5=state_s5[:, l], wkv=state_wkv[:, l])
        xs, _ = layer_step(xs, mods[l, 1:1 + Bl], layers[l], w, l, Bl, Ll, ctx, rope_tabs, min(1024, Ll))
    y_sample = xs.reshape(Bl, Ll, D)

    stack = lambda i: jnp.stack([n[i] for n in news], axis=1)
    return (y_prompt, y_sample, stack(0), stack(1), stack(2), stack(3), stack(4))
```
